```python
import functools
import math
import jax
import jax.numpy as jnp
from jax import lax
import numpy as np

D_MODEL = 1024
BATCH = 16
SEQ = 4096
DEPTH = 1
DEC_BATCH = 128
DEC_SEQ = 1
PAST_LEN = 8192
PAGE_SIZE = 128

D_SSM = D_MODEL // 2
SSM_GROUP = 16
N_SSM_GROUPS = D_SSM // SSM_GROUP
SSM_STATE = 64
SSM_CHUNK = 128
DT_MIN = 1e-3
DT_MAX = 1e-1
HEAD_DIM = 64
D_ATTN = D_MODEL - D_SSM
N_HEADS = D_ATTN // HEAD_DIM
N_KV_HEADS = 2
GROUP_SIZE = N_HEADS // N_KV_HEADS
D_KV = N_KV_HEADS * HEAD_DIM
CMP_BLOCK = 32
CMP_STRIDE = 16
CMP_HIDDEN = 2 * HEAD_DIM
SLC_BLOCK = 64
N_SELECT = 16
N_LOCAL = 2
WINDOW = 512
Q_BLOCK = 32
D_FF = 4 * D_MODEL
D_IN = D_SSM + D_ATTN + 6 * D_KV + 3 * N_HEADS
EPS = 1e-6

kernel_name = 'hybrid_s5_nsa_adaln_decode_step'


def rmsnorm(x, g):
    xf = x.astype(jnp.float32)
    y = xf * lax.rsqrt(jnp.mean(xf * xf, axis=-1, keepdims=True) + EPS)
    return (y * g.astype(jnp.float32)).astype(x.dtype)


def modulate(x, g, shift, scale):
    return rmsnorm(x, g) * (1.0 + scale[:, None, :]) + shift[:, None, :]


def masked_softmax(s, mask, axes):
    s = jnp.where(mask, s.astype(jnp.float32), -jnp.inf)
    m = jnp.max(s, axis=axes, keepdims=True)
    m = jnp.where(jnp.isfinite(m), m, 0.0)
    e = jnp.exp(s - m)
    den = jnp.sum(e, axis=axes, keepdims=True)
    return e / jnp.where(den > 0, den, 1.0)


def ssm_mixer(u, s0, lp):
    bsz, t, _ = u.shape
    lam = lax.complex(lp['ssm_lambda_re'].astype(jnp.float32), lp['ssm_lambda_im'].astype(jnp.float32))
    lam_dt = lam * jnp.exp(lp['ssm_log_dt'].astype(jnp.float32))[:, None]
    lam_bar = jnp.exp(lam_dt)
    b_mat = lax.complex(lp['ssm_b_re'].astype(jnp.float32), lp['ssm_b_im'].astype(jnp.float32))
    b_bar = ((lam_bar - 1.0) / lam)[:, :, None] * b_mat
    c_mat = lax.complex(lp['ssm_c_re'].astype(jnp.float32), lp['ssm_c_im'].astype(jnp.float32))
    chunk = SSM_CHUNK if t % SSM_CHUNK == 0 else t
    n_chunks = t // chunk
    decay = jnp.exp(lam_dt[None] * jnp.arange(1, chunk + 1, dtype=jnp.float32)[:, None, None])

    def combine(l, r):
        return (l[0] * r[0], r[0] * l[1] + r[1])

    def step(s, u_c):
        bu = jnp.einsum('bcgi,gpi->bcgp', u_c, b_bar)
        a = jnp.broadcast_to(lam_bar, bu.shape)
        _, local = lax.associative_scan(combine, (a, bu), axis=1)
        s_all = local + decay[None] * s[:, None]
        y = jnp.einsum('bcgp,gip->bcgi', s_all, c_mat).real
        return s_all[:, -1], y

    u_f = u.astype(jnp.float32)
    u_chunks = u_f.reshape(bsz, n_chunks, chunk, N_SSM_GROUPS, SSM_GROUP).swapaxes(0, 1)
    s_last, y = lax.scan(step, s0, u_chunks)
    y = y.swapaxes(0, 1).reshape(bsz, t, D_SSM) + lp['ssm_d'].astype(jnp.float32) * u_f
    z = jax.nn.gelu(y)
    out = z * jax.nn.sigmoid(jnp.dot(z, lp['ssm_w_glu'].astype(jnp.float32)))
    return out.astype(u.dtype), s_last


def compress(kv, pe, w1, w2):
    bsz, l = kv.shape[:2]
    nc = l // CMP_STRIDE
    chunks = kv[:, :nc * CMP_STRIDE].reshape(bsz, nc, CMP_STRIDE, N_KV_HEADS, HEAD_DIM)
    h_first = jnp.einsum('bcjgd,jdh->bcgh', chunks, w1[:CMP_STRIDE])
    h_second = jnp.einsum('bcjgd,jdh->bcgh', chunks, w1[CMP_STRIDE:])
    h = h_first[:, :-1] + h_second[:, 1:] + jnp.einsum('jd,jdh->h', pe, w1)
    return jnp.einsum('bngh,hd->bngd', jax.nn.gelu(h), w2)


def to_blocks(kv, n_blocks):
    pad = n_blocks * SLC_BLOCK - kv.shape[1]
    kv = jnp.pad(kv, ((0, 0), (0, pad), (0, 0), (0, 0)))
    return kv.reshape(kv.shape[0], n_blocks, SLC_BLOCK, N_KV_HEADS, HEAD_DIM)


def nsa_block(q, gates, q_pos0, k_cmp, v_cmp, k_blk, v_blk, k_band, v_band, band_pos0):
    bsz, _, _, nq, _ = q.shape
    qpos = q_pos0 + jnp.arange(nq, dtype=jnp.int32)
    n_cmp = k_cmp.shape[1]
    cmp_end = jnp.arange(n_cmp, dtype=jnp.int32) * CMP_STRIDE + (CMP_BLOCK - 1)
    p_cmp = masked_softmax(jnp.einsum('bgrqd,bngd->bgrqn', q, k_cmp), cmp_end[None, :] <= qpos[:, None], -1)
    o_cmp = jnp.einsum('bgrqn,bngd->bgrqd', p_cmp, v_cmp)
    n_slc = k_blk.shape[1]
    ratio = SLC_BLOCK // CMP_STRIDE
    imp = jnp.sum(p_cmp, axis=2)
    imp = jnp.pad(imp, ((0, 0), (0, 0), (0, 0), (1, ratio * n_slc + ratio - 1 - n_cmp)))
    w_head = jnp.array([1.0] + [2.0] * (ratio - 1), dtype=jnp.float32)
    imp_slc = jnp.einsum('bgqji,i->bgqj', imp[..., :ratio * n_slc].reshape(bsz, N_KV_HEADS, nq, n_slc, ratio), w_head)
    imp_slc = imp_slc + imp[..., ratio::ratio][..., :n_slc]
    blk = jnp.arange(n_slc, dtype=jnp.int32)
    cur = qpos // SLC_BLOCK
    valid = blk[None, :] * SLC_BLOCK <= qpos[:, None]
    forced = (blk[None, :] == 0) | (valid & (blk[None, :] > cur[:, None] - N_LOCAL))
    score = jnp.where(forced, jnp.inf, jnp.where(valid, imp_slc, -jnp.inf))
    _, idx = lax.top_k(score, min(N_SELECT, n_slc))
    b_i = jnp.arange(bsz)[:, None, None, None]
    g_i = jnp.arange(N_KV_HEADS)[None, :, None, None]
    k_sel = k_blk[b_i, idx, :, g_i, :]
    v_sel = v_blk[b_i, idx, :, g_i, :]
    kpos = idx[..., None] * SLC_BLOCK + jnp.arange(SLC_BLOCK, dtype=jnp.int32)
    mask_slc = (kpos <= qpos[:, None, None])[:, :, None]
    p_slc = masked_softmax(jnp.einsum('bgrqd,bgqkjd->bgrqkj', q, k_sel), mask_slc, (-2, -1))
    o_slc = jnp.einsum('bgrqkj,bgqkjd->bgrqd', p_slc, v_sel)
    kpos_w = band_pos0 + jnp.arange(k_band.shape[1], dtype=jnp.int32)
    dpos = qpos[:, None] - kpos_w[None, :]
    mask_w = (dpos >= 0) & (dpos < WINDOW) & (kpos_w[None, :] >= 0)
    p_win = masked_softmax(jnp.einsum('bgrqd,bkgd->bgrqk', q, k_band), mask_w, -1)
    o_win = jnp.einsum('bgrqk,bkgd->bgrqd', p_win, v_band)
    g = gates.transpose(0, 2, 3, 1).reshape(bsz, 3, N_KV_HEADS, GROUP_SIZE, nq)[..., None]
    o = g[:, 0] * o_cmp + g[:, 1] * o_slc + g[:, 2] * o_win
    return o.transpose(0, 3, 1, 2, 4).reshape(bsz, nq, D_ATTN).astype(q.dtype)


def attend_prompt(q, gates, kv_cmp, kv_slc, kv_win, lp):
    bsz, t = q.shape[:2]
    k_cmp = compress(kv_cmp[:, :, 0], lp['cmp_pe_k'], lp['cmp_w1_k'], lp['cmp_w2_k'])
    v_cmp = compress(kv_cmp[:, :, 1], lp['cmp_pe_v'], lp['cmp_w1_v'], lp['cmp_w2_v'])
    n_slc = -(-t // SLC_BLOCK)
    k_blk = to_blocks(kv_slc[:, :, 0], n_slc)
    v_blk = to_blocks(kv_slc[:, :, 1], n_slc)
    kw_pad = jnp.pad(kv_win, ((0, 0), (WINDOW, 0), (0, 0), (0, 0), (0, 0)))
    qb = Q_BLOCK if t % Q_BLOCK == 0 else t
    nqb = t // qb
    q_blocks = q.reshape(bsz, nqb, qb, N_KV_HEADS, GROUP_SIZE, HEAD_DIM).transpose(1, 0, 3, 4, 2, 5)
    g_blocks = gates.reshape(bsz, nqb, qb, 3, N_HEADS).swapaxes(0, 1)
    starts = jnp.arange(nqb, dtype=jnp.int32) * qb

    def one_block(args):
        qq, gg, s0 = args
        band = lax.dynamic_slice_in_dim(kw_pad, s0, WINDOW + qb, axis=1)
        return nsa_block(qq, gg, s0, k_cmp, v_cmp, k_blk, v_blk, band[:, :, 0], band[:, :, 1], s0 - WINDOW)

    o = lax.map(one_block, (q_blocks, g_blocks, starts))
    return o.swapaxes(0, 1).reshape(bsz, t, D_ATTN), kv_win[:, -min(WINDOW, t):]


def attend_sample(q, gates, kv_cmp, kv_slc, kv_win, cache_cmp, cache_slc, win_buf, page_table, layer, lp):
    bsz, t = q.shape[:2]
    past_len = page_table.shape[1] * PAGE_SIZE

    def full_rows(pool, new):
        past = pool[layer, page_table].reshape(bsz, past_len, 2, N_KV_HEADS, HEAD_DIM)
        return jnp.concatenate([past, new.astype(past.dtype)], axis=1)

    full_cmp = full_rows(cache_cmp, kv_cmp)
    full_slc = full_rows(cache_slc, kv_slc)
    k_cmp = compress(full_cmp[:, :, 0], lp['cmp_pe_k'], lp['cmp_w1_k'], lp['cmp_w2_k'])
    v_cmp = compress(full_cmp[:, :, 1], lp['cmp_pe_v'], lp['cmp_w1_v'], lp['cmp_w2_v'])
    n_slc = -(-full_slc.shape[1] // SLC_BLOCK)
    k_blk = to_blocks(full_slc[:, :, 0], n_slc)
    v_blk = to_blocks(full_slc[:, :, 1], n_slc)
    n_buf = win_buf.shape[1]
    band = jnp.concatenate([win_buf, kv_win.astype(win_buf.dtype)], axis=1)
    o = nsa_block(q.transpose(0, 2, 3, 1, 4), gates, PAST_LEN, k_cmp, v_cmp, k_blk, v_blk,
                  band[:, :, 0], band[:, :, 1], PAST_LEN - n_buf)
    return o, band[:, -n_buf:]


def hybrid_layer(x, c, ssm_s0, attend, lp):
    bsz, t, _ = x.shape
    mod = jnp.dot(jax.nn.silu(c), lp['w_ada']) + lp['b_ada']
    sh1, sc1, gt1, sh2, sc2, gt2 = jnp.split(mod, 6, axis=-1)
    h = modulate(x, lp['norm_attn'], sh1, sc1)
    proj = jnp.dot(h, lp['w_in'])
    sizes = [D_SSM, D_ATTN, D_KV, D_KV, D_KV, D_KV, D_KV, D_KV]
    offsets = [sum(sizes[:i + 1]) for i in range(len(sizes))]
    u, q, k_c, v_c, k_s, v_s, k_w, v_w, g = jnp.split(proj, offsets, axis=-1)
    kv_shape = (bsz, t, N_KV_HEADS, HEAD_DIM)
    kv_cmp = jnp.stack([k_c.reshape(kv_shape), v_c.reshape(kv_shape)], axis=2)
    kv_slc = jnp.stack([k_s.reshape(kv_shape), v_s.reshape(kv_shape)], axis=2)
    kv_win = jnp.stack([k_w.reshape(kv_shape), v_w.reshape(kv_shape)], axis=2)
    q = (q * HEAD_DIM ** -0.5).reshape(bsz, t, N_KV_HEADS, GROUP_SIZE, HEAD_DIM)
    gates = jax.nn.sigmoid(g.astype(jnp.float32)).reshape(bsz, t, 3, N_HEADS)
    o_ssm, s_last = ssm_mixer(u, ssm_s0, lp)
    o_attn, win_state = attend(q, gates, kv_cmp, kv_slc, kv_win)
    mix = jnp.concatenate([rmsnorm(o_ssm, lp['norm_out_ssm']), rmsnorm(o_attn, lp['norm_out_attn'])], axis=-1)
    x = x + gt1[:, None, :] * jnp.dot(mix, lp['w_out'])
    h2 = modulate(x, lp['norm_mlp'], sh2, sc2)
    x = x + gt2[:, None, :] * jnp.dot(jnp.square(jax.nn.relu(jnp.dot(h2, lp['w_up']))), lp['w_down'])
    return x, kv_cmp, kv_slc, win_state, s_last


def setup_inputs(seed: int = 0) -> dict:
    key = jax.random.key(seed)
    keys = iter(jax.random.split(key, 48))

    def nrm(shape, scale):
        return scale * jax.random.normal(next(keys), shape, jnp.float32)

    n_pages = PAST_LEN // PAGE_SIZE
    n_phys = (DEC_BATCH * n_pages * 5) // 4
    win_buf = min(WINDOW, PAST_LEN)
    kv_row = (2, N_KV_HEADS, HEAD_DIM)
    page_table = jax.random.permutation(next(keys), n_phys)[:DEC_BATCH * n_pages]
    page_table = page_table.reshape(DEC_BATCH, n_pages).astype(jnp.int32)
    lam_im = jnp.pi * jnp.arange(SSM_STATE, dtype=jnp.float32)
    ssm_shape = (DEPTH, N_SSM_GROUPS, SSM_STATE)
    return {
        'x_prompt': nrm((BATCH, SEQ, D_MODEL), 1.0),
        'x_sample': nrm((DEC_BATCH, DEC_SEQ, D_MODEL), 1.0),
        'cache_cmp': nrm((DEPTH, n_phys, PAGE_SIZE) + kv_row, 1.0),
        'cache_slc': nrm((DEPTH, n_phys, PAGE_SIZE) + kv_row, 1.0),
        'state_win': nrm((DEPTH, DEC_BATCH, win_buf) + kv_row, 1.0),
        'state_ssm_re': nrm((DEPTH, DEC_BATCH, N_SSM_GROUPS, SSM_STATE), 0.1),
        'state_ssm_im': nrm((DEPTH, DEC_BATCH, N_SSM_GROUPS, SSM_STATE), 0.1),
        'page_table': page_table,
        'c_prompt': nrm((BATCH, D_MODEL), 1.0),
        'c_sample': nrm((DEC_BATCH, D_MODEL), 1.0),
        'w_ada': nrm((DEPTH, D_MODEL, 6 * D_MODEL), 0.5 * D_MODEL ** -0.5),
        'b_ada': nrm((DEPTH, 6 * D_MODEL), 0.01),
        'norm_attn': 1.0 + nrm((DEPTH, D_MODEL), 0.01),
        'w_in': nrm((DEPTH, D_MODEL, D_IN), D_MODEL ** -0.5),
        'ssm_lambda_re': -0.5 + nrm(ssm_shape, 0.01),
        'ssm_lambda_im': lam_im + nrm(ssm_shape, 0.01),
        'ssm_log_dt': jax.random.uniform(next(keys), (DEPTH, N_SSM_GROUPS), jnp.float32, math.log(DT_MIN), math.log(DT_MAX)),
        'ssm_b_re': nrm((DEPTH, N_SSM_GROUPS, SSM_STATE, SSM_GROUP), (2 * SSM_GROUP) ** -0.5),
        'ssm_b_im': nrm((DEPTH, N_SSM_GROUPS, SSM_STATE, SSM_GROUP), (2 * SSM_GROUP) ** -0.5),
        'ssm_c_re': nrm((DEPTH, N_SSM_GROUPS, SSM_GROUP, SSM_STATE), SSM_STATE ** -0.5),
        'ssm_c_im': nrm((DEPTH, N_SSM_GROUPS, SSM_GROUP, SSM_STATE), SSM_STATE ** -0.5),
        'ssm_d': nrm((DEPTH, D_SSM), 0.5),
        'ssm_w_glu': nrm((DEPTH, D_SSM, D_SSM), D_SSM ** -0.5),
        'cmp_pe_k': nrm((DEPTH, CMP_BLOCK, HEAD_DIM), 0.02),
        'cmp_w1_k': nrm((DEPTH, CMP_BLOCK, HEAD_DIM, CMP_HIDDEN), (CMP_BLOCK * HEAD_DIM) ** -0.5),
        'cmp_w2_k': nrm((DEPTH, CMP_HIDDEN, HEAD_DIM), CMP_HIDDEN ** -0.5),
        'cmp_pe_v': nrm((DEPTH, CMP_BLOCK, HEAD_DIM), 0.02),
        'cmp_w1_v': nrm((DEPTH, CMP_BLOCK, HEAD_DIM, CMP_HIDDEN), (CMP_BLOCK * HEAD_DIM) ** -0.5),
        'cmp_w2_v': nrm((DEPTH, CMP_HIDDEN, HEAD_DIM), CMP_HIDDEN ** -0.5),
        'norm_out_ssm': 1.0 + nrm((DEPTH, D_SSM), 0.01),
        'norm_out_attn': 1.0 + nrm((DEPTH, D_ATTN), 0.01),
        'w_out': nrm((DEPTH, D_MODEL, D_MODEL), D_MODEL ** -0.5),
        'norm_mlp': 1.0 + nrm((DEPTH, D_MODEL), 0.01),
        'w_up': nrm((DEPTH, D_MODEL, D_FF), D_MODEL ** -0.5),
        'w_down': nrm((DEPTH, D_FF, D_MODEL), D_FF ** -0.5),
        'norm_final': 1.0 + nrm((D_MODEL,), 0.01),
    }


def reference(x_prompt, x_sample, cache_cmp, cache_slc, state_win, state_ssm_re, state_ssm_im, page_table,
              c_prompt, c_sample, w_ada, b_ada, norm_attn, w_in, ssm_lambda_re, ssm_lambda_im, ssm_log_dt,
              ssm_b_re, ssm_b_im, ssm_c_re, ssm_c_im, ssm_d, ssm_w_glu, cmp_pe_k, cmp_w1_k, cmp_w2_k,
              cmp_pe_v, cmp_w1_v, cmp_w2_v, norm_out_ssm, norm_out_attn, w_out, norm_mlp, w_up, w_down,
              norm_final):
    x_p = x_prompt
    x_s = x_sample
    cmp_p, slc_p, win_p, ssm_p = [], [], [], []
    cmp_s, slc_s, win_s, ssm_s = [], [], [], []
    for l in range(DEPTH):
        lp = {
            'w_ada': w_ada[l], 'b_ada': b_ada[l], 'norm_attn': norm_attn[l], 'w_in': w_in[l],
            'ssm_lambda_re': ssm_lambda_re[l], 'ssm_lambda_im': ssm_lambda_im[l], 'ssm_log_dt': ssm_log_dt[l],
            'ssm_b_re': ssm_b_re[l], 'ssm_b_im': ssm_b_im[l], 'ssm_c_re': ssm_c_re[l], 'ssm_c_im': ssm_c_im[l],
            'ssm_d': ssm_d[l], 'ssm_w_glu': ssm_w_glu[l],
            'cmp_pe_k': cmp_pe_k[l], 'cmp_w1_k': cmp_w1_k[l], 'cmp_w2_k': cmp_w2_k[l],
            'cmp_pe_v': cmp_pe_v[l], 'cmp_w1_v': cmp_w1_v[l], 'cmp_w2_v': cmp_w2_v[l],
            'norm_out_ssm': norm_out_ssm[l], 'norm_out_attn': norm_out_attn[l], 'w_out': w_out[l],
            'norm_mlp': norm_mlp[l], 'w_up': w_up[l], 'w_down': w_down[l],
        }
        s0_p = jnp.zeros((x_p.shape[0], N_SSM_GROUPS, SSM_STATE), jnp.complex64)
        x_p, kc, ks, kw, sl = hybrid_layer(x_p, c_prompt, s0_p, functools.partial(attend_prompt, lp=lp), lp)
        cmp_p.append(kc)
        slc_p.append(ks)
        win_p.append(kw)
        ssm_p.append(sl)
        s0_s = lax.complex(state_ssm_re[l].astype(jnp.float32), state_ssm_im[l].astype(jnp.float32))
        attend_s = functools.partial(attend_sample, cache_cmp=cache_cmp, cache_slc=cache_slc, win_buf=state_win[l],
                                     page_table=page_table, layer=l, lp=lp)
        x_s, kc, ks, kw, sl = hybrid_layer(x_s, c_sample, s0_s, attend_s, lp)
        cmp_s.append(kc)
        slc_s.append(ks)
        win_s.append(kw)
        ssm_s.append(sl)
    y_prompt = rmsnorm(x_p, norm_final)
    y_sample = rmsnorm(x_s, norm_final)
    ssm_p_all = jnp.stack(ssm_p, axis=0)
    ssm_s_all = jnp.stack(ssm_s, axis=0)
    return (y_prompt, y_sample,
            jnp.stack(cmp_p, axis=0), jnp.stack(slc_p, axis=0), jnp.stack(win_p, axis=0),
            ssm_p_all.real, ssm_p_all.imag,
            jnp.stack(cmp_s, axis=0), jnp.stack(slc_s, axis=0), jnp.stack(win_s, axis=0),
            ssm_s_all.real, ssm_s_all.imag)
```

```python
import functools
import math

import jax
import jax.numpy as jnp
from jax import lax
from jax.experimental import pallas as pl
from jax.experimental.pallas import tpu as pltpu

D_MODEL = 1024
D_SSM = D_MODEL // 2
SSM_GROUP = 16
N_SSM_GROUPS = D_SSM // SSM_GROUP
SSM_STATE = 64
HEAD_DIM = 64
D_ATTN = D_MODEL - D_SSM
N_HEADS = D_ATTN // HEAD_DIM
N_KV_HEADS = 2
GROUP_SIZE = N_HEADS // N_KV_HEADS
D_KV = N_KV_HEADS * HEAD_DIM
CMP_BLOCK = 32
CMP_STRIDE = 16
CMP_HIDDEN = 2 * HEAD_DIM
SLC_BLOCK = 64
N_SELECT = 16
N_LOCAL = 2
WINDOW = 512
D_FF = 4 * D_MODEL
D_IN = D_SSM + D_ATTN + 6 * D_KV + 3 * N_HEADS
EPS = 1e-6
PAGE_SIZE = 128

LANES = 128
D_IN_PAD = -(-D_IN // LANES) * LANES
KV_SLOTS = 2 * N_KV_HEADS
KV_ROW = KV_SLOTS * HEAD_DIM
SSM_CHUNK = 16
SSM_TILE = SSM_CHUNK * SSM_GROUP
VMEM_LIMIT = 56 * 1024 * 1024
FORCED_SCORE = 1e30
NEG_INF = float("-inf")

_BF16 = jnp.bfloat16
_F32 = jnp.float32
_NT = (((1,), (1,)), ((), ()))
_HIGHEST = lax.Precision.HIGHEST


def _params(*sem):
    return pltpu.CompilerParams(dimension_semantics=sem, vmem_limit_bytes=VMEM_LIMIT)


def _rms(x, g):
    return x * lax.rsqrt(jnp.mean(x * x, axis=-1, keepdims=True) + EPS) * g


def _gelu_tanh(x):
    return x * (0.5 * (1.0 + jnp.tanh(math.sqrt(2.0 / math.pi) * (x + 0.044715 * (x * x * x)))))


def _sigmoid(x):
    return 1.0 / (1.0 + jnp.exp(-x))


def _bdot(a, b):
    return jnp.dot(a.astype(_BF16), b.astype(_BF16), preferred_element_type=_F32)


def _ada_kernel(c_ref, w_ref, b_ref, o_ref):
    c = c_ref[...]
    o_ref[...] = jnp.dot(c * _sigmoid(c), w_ref[...], precision=_HIGHEST,
                         preferred_element_type=_F32) + b_ref[...]


def _ada_mod(c, w_ada, b_ada):
    n, tn = c.shape[0], D_MODEL
    return pl.pallas_call(
        _ada_kernel,
        grid=(6 * D_MODEL // tn,),
        in_specs=[pl.BlockSpec((n, D_MODEL), lambda j: (0, 0)),
                  pl.BlockSpec((D_MODEL, tn), lambda j: (0, j)),
                  pl.BlockSpec((1, tn), lambda j: (0, j))],
        out_specs=pl.BlockSpec((n, tn), lambda j: (0, j)),
        out_shape=jax.ShapeDtypeStruct((n, 6 * D_MODEL), _F32),
        compiler_params=_params("parallel"),
        name="ada_mod",
    )(c, w_ada, b_ada.reshape(1, -1))


def _inproj_kernel(x_ref, mod_ref, g_ref, w_ref, u_ref, q_ref, kc_ref, ks_ref, kw_ref, gt_ref):
    x = x_ref[0]
    h = _rms(x, g_ref[...]) * (1.0 + mod_ref[0, 1]) + mod_ref[0, 0]
    proj = jnp.dot(h.astype(_BF16), w_ref[...], preferred_element_type=_F32)
    u_ref[0] = proj[:, :D_SSM]
    q_ref[0] = proj[:, D_SSM:D_MODEL] * (HEAD_DIM ** -0.5)
    kc_ref[0] = proj[:, D_MODEL:D_MODEL + KV_ROW]
    ks_ref[0] = proj[:, D_MODEL + KV_ROW:D_MODEL + 2 * KV_ROW]
    kw_ref[0] = proj[:, D_MODEL + 2 * KV_ROW:D_MODEL + 3 * KV_ROW]
    gt_ref[0] = _sigmoid(proj[:, D_MODEL + 3 * KV_ROW:])


def _in_proj(x, mod, norm_attn, w_in_b, tm):
    bsz, t, _ = x.shape
    r = mod.shape[2]
    rb = 1 if r == 1 else tm
    mod_map = (lambda b, i: (b, 0, 0, 0)) if r == 1 else (lambda b, i: (b, 0, i, 0))
    widths = (D_SSM, D_ATTN, KV_ROW, KV_ROW, KV_ROW, D_IN_PAD - D_MODEL - 3 * KV_ROW)
    return pl.pallas_call(
        _inproj_kernel,
        grid=(bsz, t // tm),
        in_specs=[pl.BlockSpec((1, tm, D_MODEL), lambda b, i: (b, i, 0)),
                  pl.BlockSpec((1, 6, rb, D_MODEL), mod_map),
                  pl.BlockSpec((1, D_MODEL), lambda b, i: (0, 0)),
                  pl.BlockSpec((D_MODEL, D_IN_PAD), lambda b, i: (0, 0))],
        out_specs=[pl.BlockSpec((1, tm, w), lambda b, i: (b, i, 0)) for w in widths],
        out_shape=[jax.ShapeDtypeStruct((bsz, t, w), _F32) for w in widths],
        compiler_params=_params("parallel", "parallel"),
        name="in_proj",
    )(x, mod, norm_attn.reshape(1, -1), w_in_b)


def _ssm_tables(lam_re, lam_im, log_dt, b_re, b_im, c_re, c_im):
    hp = dict(precision=_HIGHEST)
    lam = lax.complex(lam_re.astype(_F32), lam_im.astype(_F32))
    lam_dt = lam * jnp.exp(log_dt.astype(_F32))[:, None]
    lam_bar = jnp.exp(lam_dt)
    b_bar = ((lam_bar - 1.0) / lam)[:, :, None] * lax.complex(b_re.astype(_F32), b_im.astype(_F32))
    c_mat = lax.complex(c_re.astype(_F32), c_im.astype(_F32))
    n = SSM_CHUNK
    steps = jnp.arange(n + 1, dtype=_F32)
    pw = jnp.exp(lam_dt[None] * steps[:, None, None])
    kern = jnp.einsum('gip,dgp,gpj->dgij', c_mat, pw[:n], b_bar, **hp).real
    lag = jnp.arange(n)[None, :] - jnp.arange(n)[:, None]
    toep = jnp.where((lag >= 0)[:, :, None, None, None], kern[jnp.clip(lag, 0)], 0.0)
    toep = toep.transpose(2, 0, 4, 1, 3).reshape(N_SSM_GROUPS, SSM_TILE, SSM_TILE)
    w_in = pw[n - 1 - jnp.arange(n)][:, :, :, None] * b_bar[None]
    w_in = w_in.transpose(1, 0, 3, 2).reshape(N_SSM_GROUPS, SSM_TILE, SSM_STATE)
    w_in = jnp.concatenate([w_in.real, w_in.imag], axis=-1)
    cv = c_mat[None] * pw[1:, :, None, :]
    cv = cv.transpose(1, 3, 0, 2).reshape(N_SSM_GROUPS, SSM_STATE, SSM_TILE)
    v_out = jnp.concatenate([cv.real, -cv.imag], axis=1)
    a_chunk = jnp.stack([pw[n].real, pw[n].imag], axis=1)
    eye = jnp.eye(N_SSM_GROUPS, dtype=_F32)
    bd = jnp.einsum('gpi,gh->gihp', b_bar, eye.astype(b_bar.dtype)).reshape(D_SSM, N_SSM_GROUPS * SSM_STATE)
    cd = jnp.einsum('gip,gh->gphi', c_mat, eye.astype(c_mat.dtype)).reshape(N_SSM_GROUPS * SSM_STATE, D_SSM)
    lam_flat = lam_bar.reshape(1, -1)
    return dict(toep=toep.astype(_BF16), w_in=w_in.astype(_BF16), v_out=v_out.astype(_BF16), a_chunk=a_chunk,
                bd_re=bd.real.astype(_BF16), bd_im=bd.imag.astype(_BF16),
                cd_re=cd.real.astype(_BF16), cd_im=cd.imag.astype(_BF16),
                lam_re=lam_flat.real, lam_im=lam_flat.imag)


def _ssm_prompt_kernel(u_ref, toep_ref, w_ref, v_ref, a_ref, y_ref, sre_ref, sim_ref,
                       xre, xim, sin_re, sin_im, *, n_chunks, bsz):
    u = u_ref[0]
    x = jnp.dot(u, w_ref[0], preferred_element_type=_F32)
    xre[...] = x[:, :SSM_STATE]
    xim[...] = x[:, SSM_STATE:]
    a_re = a_ref[0, 0:1, :]
    a_im = a_ref[0, 1:2, :]

    def step(c, carry):
        s_re, s_im = carry
        rows = pl.ds(pl.multiple_of(c * bsz, bsz), bsz)
        sin_re[rows, :] = s_re
        sin_im[rows, :] = s_im
        return (a_re * s_re - a_im * s_im + xre[rows, :],
                a_re * s_im + a_im * s_re + xim[rows, :])

    zero = jnp.zeros((bsz, SSM_STATE), _F32)
    s_re, s_im = lax.fori_loop(0, n_chunks, step, (zero, zero))
    sre_ref[0] = s_re
    sim_ref[0] = s_im
    v = v_ref[0]
    y_ref[0] = (jnp.dot(u, toep_ref[0], preferred_element_type=_F32)
                + _bdot(sin_re[...], v[:SSM_STATE])
                + _bdot(sin_im[...], v[SSM_STATE:]))


def _ssm_prompt(u, tabs):
    bsz, t, _ = u.shape
    nch = t // SSM_CHUNK
    rows = nch * bsz
    u_t = (u.reshape(bsz, nch, SSM_CHUNK, N_SSM_GROUPS, SSM_GROUP).transpose(3, 1, 0, 2, 4)
           .reshape(N_SSM_GROUPS, rows, SSM_TILE).astype(_BF16))
    y_t, s_re, s_im = pl.pallas_call(
        functools.partial(_ssm_prompt_kernel, n_chunks=nch, bsz=bsz),
        grid=(N_SSM_GROUPS,),
        in_specs=[pl.BlockSpec((1, rows, SSM_TILE), lambda g: (g, 0, 0)),
                  pl.BlockSpec((1, SSM_TILE, SSM_TILE), lambda g: (g, 0, 0)),
                  pl.BlockSpec((1, SSM_TILE, 2 * SSM_STATE), lambda g: (g, 0, 0)),
                  pl.BlockSpec((1, 2 * SSM_STATE, SSM_TILE), lambda g: (g, 0, 0)),
                  pl.BlockSpec((1, 2, SSM_STATE), lambda g: (g, 0, 0))],
        out_specs=[pl.BlockSpec((1, rows, SSM_TILE), lambda g: (g, 0, 0)),
                   pl.BlockSpec((1, bsz, SSM_STATE), lambda g: (g, 0, 0)),
                   pl.BlockSpec((1, bsz, SSM_STATE), lambda g: (g, 0, 0))],
        out_shape=[jax.ShapeDtypeStruct((N_SSM_GROUPS, rows, SSM_TILE), _F32),
                   jax.ShapeDtypeStruct((N_SSM_GROUPS, bsz, SSM_STATE), _F32),
                   jax.ShapeDtypeStruct((N_SSM_GROUPS, bsz, SSM_STATE), _F32)],
        scratch_shapes=[pltpu.VMEM((rows, SSM_STATE), _F32) for _ in range(4)],
        compiler_params=_params("parallel"),
        name="ssm_prompt",
    )(u_t, tabs['toep'], tabs['w_in'], tabs['v_out'], tabs['a_chunk'])
    y = (y_t.reshape(N_SSM_GROUPS, nch, bsz, SSM_CHUNK, SSM_GROUP).transpose(2, 1, 3, 0, 4)
         .reshape(bsz, t, D_SSM))
    return y, s_re.transpose(1, 0, 2), s_im.transpose(1, 0, 2)


def _ssm_step_kernel(u_ref, s0re_ref, s0im_ref, lre_ref, lim_ref, bdre_ref, bdim_ref, cdre_ref, cdim_ref,
                     y_ref, sre_ref, sim_ref):
    u = u_ref[...]
    s_re, s_im = s0re_ref[...], s0im_ref[...]
    l_re, l_im = lre_ref[...], lim_ref[...]
    n_re = l_re * s_re - l_im * s_im + _bdot(u, bdre_ref[...])
    n_im = l_re * s_im + l_im * s_re + _bdot(u, bdim_ref[...])
    sre_ref[...] = n_re
    sim_ref[...] = n_im
    y_ref[...] = _bdot(n_re, cdre_ref[...]) - _bdot(n_im, cdim_ref[...])


def _ssm_step(u, s0_re, s0_im, tabs):
    n = u.shape[0]
    ns = N_SSM_GROUPS * SSM_STATE
    return pl.pallas_call(
        _ssm_step_kernel,
        out_shape=[jax.ShapeDtypeStruct((n, D_SSM), _F32),
                   jax.ShapeDtypeStruct((n, ns), _F32),
                   jax.ShapeDtypeStruct((n, ns), _F32)],
        compiler_params=pltpu.CompilerParams(vmem_limit_bytes=VMEM_LIMIT),
        name="ssm_step",
    )(u, s0_re, s0_im, tabs['lam_re'], tabs['lam_im'], tabs['bd_re'], tabs['bd_im'], tabs['cd_re'], tabs['cd_im'])


def _compress_tables(pe_k, w1_k, w2_k, pe_v, w1_v, w2_v):
    zeros = jnp.zeros((CMP_STRIDE, HEAD_DIM, CMP_HIDDEN), _F32)
    cols = []
    for half in range(2):
        for slot in range(KV_SLOTS):
            w1 = (w1_k if slot < N_KV_HEADS else w1_v)[half * CMP_STRIDE:(half + 1) * CMP_STRIDE]
            blk = jnp.stack([w1 if s == slot else zeros for s in range(KV_SLOTS)], axis=1)
            cols.append(blk.reshape(CMP_STRIDE * KV_ROW, CMP_HIDDEN))
    w_all = jnp.concatenate(cols, axis=1).astype(_BF16)
    z2 = jnp.zeros((CMP_HIDDEN, HEAD_DIM), _F32)
    w2 = jnp.concatenate(
        [jnp.concatenate([(w2_k if s < N_KV_HEADS else w2_v) if s == slot else z2 for s in range(KV_SLOTS)], axis=1)
         for slot in range(KV_SLOTS)], axis=0).astype(_BF16)
    pe = jnp.stack([pe_k.reshape(-1), pe_v.reshape(-1)], axis=0)
    pe = jnp.concatenate([pe, jnp.zeros((6, pe.shape[1]), _F32)], axis=0)
    w1f = jnp.stack([w1_k.reshape(-1, CMP_HIDDEN), w1_v.reshape(-1, CMP_HIDDEN)], axis=0)
    return w_all, w2, pe, w1f


def _compress_rows(x, wall_ref, w2_ref, pe_ref, w1f_ref, n_rows):
    hw = KV_SLOTS * CMP_HIDDEN
    h = jnp.dot(x.astype(_BF16), wall_ref[...], preferred_element_type=_F32)
    pe = pe_ref[...]
    bias_k = jnp.dot(pe, w1f_ref[0], precision=_HIGHEST, preferred_element_type=_F32)[0:1]
    bias_v = jnp.dot(pe, w1f_ref[1], precision=_HIGHEST, preferred_element_type=_F32)[1:2]
    bias = jnp.concatenate([bias_k, bias_k, bias_v, bias_v], axis=1)
    nxt = pltpu.roll(h[:, hw:], n_rows - 1, 0)
    hid = _gelu_tanh(h[:, :hw] + nxt + bias)
    out = jnp.dot(hid.astype(_BF16), w2_ref[...], preferred_element_type=_F32)
    row = lax.broadcasted_iota(jnp.int32, out.shape, 0)
    return jnp.where(row < n_rows - 1, out, 0.0)


def _compress_prompt_kernel(x_ref, wall_ref, w2_ref, pe_ref, w1f_ref, o_ref, *, n_rows):
    out = _compress_rows(x_ref[0], wall_ref, w2_ref, pe_ref, w1f_ref, n_rows)
    for s in range(KV_SLOTS):
        o_ref[0, s] = out[:, s * HEAD_DIM:(s + 1) * HEAD_DIM]


def _compress_prompt(kv_cmp, ctabs):
    bsz, t, _ = kv_cmp.shape
    nc = t // CMP_STRIDE
    w_all, w2, pe, w1f = ctabs
    const = lambda shape: pl.BlockSpec(shape, lambda b: (0,) * len(shape))
    return pl.pallas_call(
        functools.partial(_compress_prompt_kernel, n_rows=nc),
        grid=(bsz,),
        in_specs=[pl.BlockSpec((1, nc, CMP_STRIDE * KV_ROW), lambda b: (b, 0, 0)),
                  const(w_all.shape), const(w2.shape), const(pe.shape), const(w1f.shape)],
        out_specs=pl.BlockSpec((1, KV_SLOTS, nc, HEAD_DIM), lambda b: (b, 0, 0, 0)),
        out_shape=jax.ShapeDtypeStruct((bsz, KV_SLOTS, nc, HEAD_DIM), _F32),
        compiler_params=_params("parallel"),
        name="compress_prompt",
    )(kv_cmp.reshape(bsz, nc, CMP_STRIDE * KV_ROW), w_all, w2, pe, w1f)


def _page_copies(pt_ref, cache_hbm, buf, sem, seq, slot, n_pages, rows_per_page):
    return [pltpu.make_async_copy(cache_hbm.at[pt_ref[seq, j]],
                                  buf.at[slot, pl.ds(j * rows_per_page, rows_per_page)],
                                  sem.at[slot]) for j in range(n_pages)]


def _gather_pages(pt_ref, cache_hbm, buf, sem, n_pages, rows_per_page):
    b = pl.program_id(0)
    slot = lax.rem(b, 2)

    @pl.when(b == 0)
    def _():
        for cp in _page_copies(pt_ref, cache_hbm, buf, sem, 0, 0, n_pages, rows_per_page):
            cp.start()

    @pl.when(b + 1 < pl.num_programs(0))
    def _():
        for cp in _page_copies(pt_ref, cache_hbm, buf, sem, b + 1, 1 - slot, n_pages, rows_per_page):
            cp.start()

    for cp in _page_copies(pt_ref, cache_hbm, buf, sem, b, slot, n_pages, rows_per_page):
        cp.wait()
    return slot


def _compress_paged_kernel(pt_ref, cache_hbm, wall_ref, w2_ref, pe_ref, w1f_ref, o_ref, buf, sem,
                           *, n_pages, n_rows):
    slot = _gather_pages(pt_ref, cache_hbm, buf, sem, n_pages, PAGE_SIZE // CMP_STRIDE)
    out = _compress_rows(buf[slot], wall_ref, w2_ref, pe_ref, w1f_ref, n_rows)
    for s in range(KV_SLOTS):
        o_ref[0, s] = out[:, s * HEAD_DIM:(s + 1) * HEAD_DIM]


def _compress_paged(cache, page_table, ctabs):
    nseq, n_pages = page_table.shape
    rpp = PAGE_SIZE // CMP_STRIDE
    nc = n_pages * rpp
    w_all, w2, pe, w1f = ctabs
    const = lambda shape: pl.BlockSpec(shape, lambda b, pt: (0,) * len(shape))
    grid_spec = pltpu.PrefetchScalarGridSpec(
        num_scalar_prefetch=1,
        grid=(nseq,),
        in_specs=[pl.BlockSpec(memory_space=pl.ANY),
                  const(w_all.shape), const(w2.shape), const(pe.shape), const(w1f.shape)],
        out_specs=pl.BlockSpec((1, KV_SLOTS, nc, HEAD_DIM), lambda b, pt: (b, 0, 0, 0)),
        scratch_shapes=[pltpu.VMEM((2, nc, CMP_STRIDE * KV_ROW), _F32), pltpu.SemaphoreType.DMA((2,))],
    )
    return pl.pallas_call(
        functools.partial(_compress_paged_kernel, n_pages=n_pages, n_rows=nc),
        grid_spec=grid_spec,
        out_shape=jax.ShapeDtypeStruct((nseq, KV_SLOTS, nc, HEAD_DIM), _F32),
        compiler_params=_params("arbitrary"),
        name="compress_paged",
    )(page_table, cache.reshape(cache.shape[0], rpp, CMP_STRIDE * KV_ROW), w_all, w2, pe, w1f)


def _importance_map(n_cmp_rows, n_blocks_rows, n_cmp):
    ratio = SLC_BLOCK // CMP_STRIDE
    j = jnp.arange(n_blocks_rows)[:, None]
    n = jnp.arange(n_cmp_rows)[None, :]
    off = n - ratio * j
    w = jnp.where((off == -1) | (off == ratio - 1), 1.0, jnp.where((off >= 0) & (off < ratio - 1), 2.0, 0.0))
    return jnp.where(n < n_cmp, w, 0.0).astype(_F32)


def _softmax_rows(s, ok):
    s = jnp.where(ok, s, NEG_INF)
    m = jnp.max(s, axis=-1, keepdims=True)
    m = jnp.where(m == NEG_INF, 0.0, m)
    e = jnp.exp(s - m)
    den = jnp.sum(e, axis=-1, keepdims=True)
    return e / jnp.where(den > 0, den, 1.0)


def _nsa_prompt_kernel(q_ref, gt_ref, kc_ref, vc_ref, ks_ref, vs_ref, kw_ref, vw_ref, mt_ref, e_ref, o_ref,
                       *, tq, kc_len, n_cmp, n_blk, blk_pad):
    t0 = pl.program_id(2) * tq
    qf = q_ref[0]
    q4 = jnp.concatenate([qf[:, r * HEAD_DIM:(r + 1) * HEAD_DIM] for r in range(GROUP_SIZE)], axis=0)
    q4b = q4.astype(_BF16)
    rows = GROUP_SIZE * tq
    qpos = t0 + lax.broadcasted_iota(jnp.int32, (tq, 1), 0)

    ncr = kc_ref.shape[2]
    s_c = lax.dot_general(q4, kc_ref[0, 0], _NT, precision=_HIGHEST, preferred_element_type=_F32)
    n_idx = lax.broadcasted_iota(jnp.int32, (tq, ncr), 1)
    cmp_end = jnp.where(n_idx < n_cmp, n_idx * CMP_STRIDE + (CMP_BLOCK - 1), jnp.int32(2 ** 30))
    p_c = _softmax_rows(s_c.reshape(GROUP_SIZE, tq, ncr), (cmp_end <= qpos)[None])
    o_c = _bdot(p_c.reshape(rows, ncr), vc_ref[0, 0])

    imp = p_c[0] + p_c[1] + p_c[2] + p_c[3]
    imp_t = lax.dot_general(mt_ref[...], imp, _NT, precision=_HIGHEST, preferred_element_type=_F32)
    qrow = t0 + lax.broadcasted_iota(jnp.int32, (1, tq), 1)
    j_idx = lax.broadcasted_iota(jnp.int32, (n_blk, tq), 0)
    valid = j_idx * SLC_BLOCK <= qrow
    cur = lax.shift_right_arithmetic(qrow, int(math.log2(SLC_BLOCK)))
    forced = (j_idx == 0) | (valid & (j_idx > cur - N_LOCAL))
    score = jnp.where(forced, FORCED_SCORE, jnp.where(valid, imp_t, -1.0))
    rank = jnp.zeros((n_blk, tq), jnp.int32)
    for j in range(n_blk):
        sj = score[j:j + 1, :]
        rank = rank + jnp.where(j_idx > j, jnp.where(sj >= score, 1, 0), jnp.where(sj > score, 1, 0))
    sel = jnp.where(valid & (rank < N_SELECT), 1.0, 0.0)
    sel = jnp.concatenate([sel, jnp.zeros((blk_pad - n_blk, tq), _F32)], axis=0)
    sel_q = sel.T.astype(_BF16)

    def chunk(c, carry):
        m, l, acc = carry
        k0 = pl.multiple_of(c * kc_len, kc_len)
        s = lax.dot_general(q4b, ks_ref[0, 0, pl.ds(k0, kc_len), :], _NT, preferred_element_type=_F32)
        picked = jnp.dot(sel_q, e_ref[c], preferred_element_type=_F32)
        kpos = k0 + lax.broadcasted_iota(jnp.int32, (tq, kc_len), 1)
        ok = jnp.where(kpos <= qpos, picked, 0.0) > 0.5
        s = jnp.where(ok[None], s.reshape(GROUP_SIZE, tq, kc_len), NEG_INF).reshape(rows, kc_len)
        m_new = jnp.maximum(m, jnp.max(s, axis=-1, keepdims=True))
        alpha = jnp.exp(m - m_new)
        p = jnp.exp(s - m_new)
        l = alpha * l + jnp.sum(p, axis=-1, keepdims=True)
        acc = alpha * acc + jnp.dot(p.astype(_BF16), vs_ref[0, 0, pl.ds(k0, kc_len), :],
                                    preferred_element_type=_F32)
        return m_new, l, acc

    n_k = (t0 + tq + kc_len - 1) // kc_len
    init = (jnp.full((rows, 1), NEG_INF, _F32), jnp.zeros((rows, 1), _F32), jnp.zeros((rows, HEAD_DIM), _F32))
    _, l_s, acc_s = lax.fori_loop(0, n_k, chunk, init)
    o_s = acc_s / l_s

    wk = tq + WINDOW
    w0 = pl.multiple_of(jnp.maximum(t0 - WINDOW, 0), tq)
    s_w = lax.dot_general(q4b, kw_ref[0, 0, pl.ds(w0, wk), :], _NT, preferred_element_type=_F32)
    dpos = qpos - (w0 + lax.broadcasted_iota(jnp.int32, (tq, wk), 1))
    ok_w = jnp.where(dpos >= 0, dpos, WINDOW) < WINDOW
    p_w = _softmax_rows(s_w.reshape(GROUP_SIZE, tq, wk), ok_w[None]).reshape(rows, wk)
    o_w = jnp.dot(p_w.astype(_BF16), vw_ref[0, 0, pl.ds(w0, wk), :], preferred_element_type=_F32)

    gt = gt_ref[0, 0]
    outs = []
    for r in range(GROUP_SIZE):
        sl = slice(r * tq, (r + 1) * tq)
        outs.append(gt[:, r:r + 1] * o_c[sl]
                    + gt[:, GROUP_SIZE + r:GROUP_SIZE + r + 1] * o_s[sl]
                    + gt[:, 2 * GROUP_SIZE + r:2 * GROUP_SIZE + r + 1] * o_w[sl])
    o_ref[0] = jnp.concatenate(outs, axis=1)


def _expand_table(n_chunks, kc_len, blk_pad):
    blk = (jnp.arange(n_chunks)[:, None, None] * kc_len + jnp.arange(kc_len)[None, None, :]) // SLC_BLOCK
    return (blk == jnp.arange(blk_pad)[None, :, None]).astype(_BF16)


def _group_gates(gates, t):
    bsz = gates.shape[0]
    g = gates[..., :3 * N_HEADS].reshape(bsz, t, 3, N_KV_HEADS, GROUP_SIZE).transpose(0, 3, 1, 2, 4)
    g = g.reshape(bsz, N_KV_HEADS, t, 3 * GROUP_SIZE)
    return jnp.pad(g, ((0, 0), (0, 0), (0, 0), (0, LANES - 3 * GROUP_SIZE)))


def _slot_major(kv, dtype):
    bsz, t, _ = kv.shape
    return kv.reshape(bsz, t, KV_SLOTS, HEAD_DIM).transpose(0, 2, 1, 3).astype(dtype)


def _nsa_prompt(q, gates, kcmp, kv_slc, kv_win, tq, kc_len):
    bsz, t, _ = q.shape
    ncr = kcmp.shape[2]
    n_blk = t // SLC_BLOCK
    blk_pad = -(-n_blk // LANES) * LANES
    mt = _importance_map(ncr, n_blk, ncr - 1)
    e = _expand_table(t // kc_len, kc_len, blk_pad)
    kvs = _slot_major(kv_slc, _BF16)
    kvw = _slot_major(kv_win, _BF16)
    kspec = lambda n, off: pl.BlockSpec((1, 1, n, HEAD_DIM), lambda b, g, i: (b, g + off, 0, 0))
    return pl.pallas_call(
        functools.partial(_nsa_prompt_kernel, tq=tq, kc_len=kc_len, n_cmp=ncr - 1, n_blk=n_blk, blk_pad=blk_pad),
        grid=(bsz, N_KV_HEADS, t // tq),
        in_specs=[pl.BlockSpec((1, tq, GROUP_SIZE * HEAD_DIM), lambda b, g, i: (b, i, g)),
                  pl.BlockSpec((1, 1, tq, LANES), lambda b, g, i: (b, g, i, 0)),
                  kspec(ncr, 0), kspec(ncr, N_KV_HEADS),
                  kspec(t, 0), kspec(t, N_KV_HEADS),
                  kspec(t, 0), kspec(t, N_KV_HEADS),
                  pl.BlockSpec(mt.shape, lambda b, g, i: (0, 0)),
                  pl.BlockSpec(e.shape, lambda b, g, i: (0, 0, 0))],
        out_specs=pl.BlockSpec((1, tq, GROUP_SIZE * HEAD_DIM), lambda b, g, i: (b, i, g)),
        out_shape=jax.ShapeDtypeStruct((bsz, t, D_ATTN), _F32),
        compiler_params=_params("parallel", "parallel", "parallel"),
        name="nsa_prompt",
    )(q, _group_gates(gates, t), kcmp, kcmp, kvs, kvs, kvw, kvw, mt, e)


def _nsa_sample_kernel(pt_ref, q_ref, gt_ref, kcmp_ref, slc_hbm, ksn_ref, win_ref, kwn_ref, m_ref, e_ref,
                       o_ref, nw_ref, buf, sem, xb,
                       *, n_pages, past, n_cmp, n_blk, blk_pad, n_buf):
    slot = _gather_pages(pt_ref, slc_hbm, buf, sem, n_pages, PAGE_SIZE)
    xb[...] = buf[slot].astype(_BF16)
    q = q_ref[0]
    gt = gt_ref[0]
    ks_new = ksn_ref[0]
    kw_new = kwn_ref[0]
    win = win_ref[0]
    rolled = pltpu.roll(win, n_buf - 1, 0)
    row = lax.broadcasted_iota(jnp.int32, win.shape, 0)
    band = jnp.where(row == n_buf - 1, kw_new, rolled)
    nw_ref[0] = band
    band_b = band.astype(_BF16)
    ncr = kcmp_ref.shape[2]
    pad_rows = 8 - GROUP_SIZE
    lane = lax.broadcasted_iota(jnp.int32, (8, LANES), 1)
    hrow = lax.broadcasted_iota(jnp.int32, (8, LANES), 0)
    heads = []
    for g in range(N_KV_HEADS):
        q8 = jnp.concatenate([q[:, (g * GROUP_SIZE + r) * HEAD_DIM:(g * GROUP_SIZE + r + 1) * HEAD_DIM]
                              for r in range(GROUP_SIZE)] + [jnp.zeros((pad_rows, HEAD_DIM), _F32)], axis=0)
        z = lambda n: jnp.zeros((8, n * HEAD_DIM), _F32)
        wq = jnp.concatenate(([z(g)] if g else []) + [q8, z(KV_SLOTS - 1 - g)], axis=1).astype(_BF16)
        v_lo = (N_KV_HEADS + g) * HEAD_DIM

        s_c = lax.dot_general(q8, kcmp_ref[0, g], _NT, precision=_HIGHEST, preferred_element_type=_F32)
        n_idx = lax.broadcasted_iota(jnp.int32, (8, ncr), 1)
        cmp_end = jnp.where(n_idx < n_cmp, n_idx * CMP_STRIDE + (CMP_BLOCK - 1), jnp.int32(2 ** 30))
        p_c = _softmax_rows(s_c, cmp_end <= past)
        o_c = _bdot(p_c, kcmp_ref[0, N_KV_HEADS + g])

        imp = jnp.sum(p_c[:GROUP_SIZE], axis=0, keepdims=True)
        imp_s = jnp.dot(jnp.broadcast_to(imp, (8, ncr)), m_ref[...], precision=_HIGHEST,
                        preferred_element_type=_F32)[0:1]
        j_idx = lax.broadcasted_iota(jnp.int32, (1, blk_pad), 1)
        valid = j_idx * SLC_BLOCK <= past
        forced = (j_idx == 0) | (valid & (j_idx > past // SLC_BLOCK - N_LOCAL))
        score = jnp.where(forced, FORCED_SCORE, jnp.where(valid, imp_s, -1.0))
        s_k = jnp.broadcast_to(score, (blk_pad, blk_pad))
        s_j = s_k.T
        jj = lax.broadcasted_iota(jnp.int32, (blk_pad, blk_pad), 0)
        kk = lax.broadcasted_iota(jnp.int32, (blk_pad, blk_pad), 1)
        beats = jnp.where(kk > jj, jnp.where(s_j >= s_k, 1.0, 0.0), jnp.where(s_j > s_k, 1.0, 0.0))
        rank = jnp.sum(beats, axis=0, keepdims=True)
        sel = jnp.where(valid & (rank < N_SELECT), 1.0, 0.0)
        picked = jnp.dot(jnp.broadcast_to(sel, (8, blk_pad)).astype(_BF16), e_ref[...],
                         preferred_element_type=_F32) > 0.5

        s_s = lax.dot_general(wq, xb[...], _NT, preferred_element_type=_F32)
        s_s = jnp.where(picked, s_s, NEG_INF)
        s_n = jnp.sum(q8 * ks_new[:, g * HEAD_DIM:(g + 1) * HEAD_DIM], axis=-1, keepdims=True)
        m = jnp.maximum(jnp.max(s_s, axis=-1, keepdims=True), s_n)
        e_p, e_n = jnp.exp(s_s - m), jnp.exp(s_n - m)
        num = jnp.dot(e_p.astype(_BF16), xb[...], preferred_element_type=_F32)[:, v_lo:v_lo + HEAD_DIM]
        o_s = (num + e_n * ks_new[:, v_lo:v_lo + HEAD_DIM]) / (jnp.sum(e_p, axis=-1, keepdims=True) + e_n)

        s_w = lax.dot_general(wq, band_b, _NT, preferred_element_type=_F32)
        i_idx = lax.broadcasted_iota(jnp.int32, (8, n_buf), 1)
        p_w = _softmax_rows(s_w, (n_buf - 1 - i_idx) < WINDOW)
        o_w = jnp.dot(p_w.astype(_BF16), band_b, preferred_element_type=_F32)[:, v_lo:v_lo + HEAD_DIM]

        def gate(branch):
            tgt = branch * N_HEADS + g * GROUP_SIZE + hrow
            return jnp.sum(jnp.where(lane == tgt, jnp.broadcast_to(gt, (8, LANES)), 0.0), axis=-1, keepdims=True)

        o = gate(0) * o_c + gate(1) * o_s + gate(2) * o_w
        heads += [o[r:r + 1, :] for r in range(GROUP_SIZE)]
    o_ref[0] = jnp.concatenate(heads, axis=1)


def _nsa_sample(q, gates, kcmp, cache_slc, page_table, ks_new, state_win, kw_new):
    nseq, n_pages = page_table.shape
    past = n_pages * PAGE_SIZE
    ncr = kcmp.shape[2]
    n_buf = state_win.shape[1]
    n_blk = past // SLC_BLOCK + 1
    blk_pad = -(-n_blk // LANES) * LANES
    m_tab = _importance_map(ncr, blk_pad, ncr - 1).T
    e = _expand_table(1, past, blk_pad)[0]
    row = lambda w: pl.BlockSpec((1, 1, w), lambda b, pt: (b, 0, 0))
    grid_spec = pltpu.PrefetchScalarGridSpec(
        num_scalar_prefetch=1,
        grid=(nseq,),
        in_specs=[row(D_ATTN), row(LANES),
                  pl.BlockSpec((1, KV_SLOTS, ncr, HEAD_DIM), lambda b, pt: (b, 0, 0, 0)),
                  pl.BlockSpec(memory_space=pl.ANY),
                  row(KV_ROW),
                  pl.BlockSpec((1, n_buf, KV_ROW), lambda b, pt: (b, 0, 0)),
                  row(KV_ROW),
                  pl.BlockSpec(m_tab.shape, lambda b, pt: (0, 0)),
                  pl.BlockSpec(e.shape, lambda b, pt: (0, 0))],
        out_specs=[row(D_ATTN), pl.BlockSpec((1, n_buf, KV_ROW), lambda b, pt: (b, 0, 0))],
        scratch_shapes=[pltpu.VMEM((2, past, KV_ROW), _F32), pltpu.SemaphoreType.DMA((2,)),
                        pltpu.VMEM((past, KV_ROW), _BF16)],
    )
    return pl.pallas_call(
        functools.partial(_nsa_sample_kernel, n_pages=n_pages, past=past, n_cmp=ncr - 1, n_blk=n_blk,
                          blk_pad=blk_pad, n_buf=n_buf),
        grid_spec=grid_spec,
        out_shape=[jax.ShapeDtypeStruct((nseq, 1, D_ATTN), _F32),
                   jax.ShapeDtypeStruct((nseq, n_buf, KV_ROW), _F32)],
        compiler_params=_params("arbitrary"),
        name="nsa_sample",
    )(page_table, q, gates, kcmp, cache_slc, ks_new, state_win, kw_new, m_tab, e)


def _post_kernel(x_ref, y_ref, u_ref, oa_ref, mod_ref, d_ref, wglu_ref, gs_ref, ga_ref, wout_ref, gm_ref,
                 wup_ref, wdn_ref, gf_ref, o_ref, *, ff_chunk):
    z = _gelu_tanh(y_ref[0] + d_ref[...] * u_ref[0])
    o_ssm = z * _sigmoid(jnp.dot(z.astype(_BF16), wglu_ref[...], preferred_element_type=_F32))
    mix = (jnp.dot(_rms(o_ssm, gs_ref[...]).astype(_BF16), wout_ref[:D_SSM, :], preferred_element_type=_F32)
           + jnp.dot(_rms(oa_ref[0], ga_ref[...]).astype(_BF16), wout_ref[D_SSM:, :], preferred_element_type=_F32))
    x1 = x_ref[0] + mod_ref[0, 2] * mix
    h2 = (_rms(x1, gm_ref[...]) * (1.0 + mod_ref[0, 4]) + mod_ref[0, 3]).astype(_BF16)
    acc = jnp.zeros_like(x1)
    for k in range(D_FF // ff_chunk):
        cols = slice(k * ff_chunk, (k + 1) * ff_chunk)
        hid = jnp.maximum(jnp.dot(h2, wup_ref[:, cols], preferred_element_type=_F32), 0.0)
        acc = acc + jnp.dot((hid * hid).astype(_BF16), wdn_ref[cols, :], preferred_element_type=_F32)
    o_ref[0] = _rms(x1 + mod_ref[0, 5] * acc, gf_ref[...])


def _post(x, y_ssm, u, o_attn, mod, lp, norm_final, tm):
    bsz, t, _ = x.shape
    r = mod.shape[2]
    rb = 1 if r == 1 else tm
    mod_map = (lambda b, i: (b, 0, 0, 0)) if r == 1 else (lambda b, i: (b, 0, i, 0))
    tok = lambda w: pl.BlockSpec((1, tm, w), lambda b, i: (b, i, 0))
    const = lambda shape: pl.BlockSpec(shape, lambda b, i: (0,) * len(shape), pipeline_mode=pl.Buffered(1))
    vec = lambda v: v.reshape(1, -1).astype(_F32)
    weights = [vec(lp['ssm_d']), lp['ssm_w_glu'].astype(_BF16), vec(lp['norm_out_ssm']), vec(lp['norm_out_attn']),
               lp['w_out'].astype(_BF16), vec(lp['norm_mlp']), lp['w_up'].astype(_BF16), lp['w_down'].astype(_BF16),
               vec(norm_final)]
    return pl.pallas_call(
        functools.partial(_post_kernel, ff_chunk=D_MODEL),
        grid=(bsz, t // tm),
        in_specs=[tok(D_MODEL), tok(D_SSM), tok(D_SSM), tok(D_ATTN),
                  pl.BlockSpec((1, 6, rb, D_MODEL), mod_map)] + [const(w.shape) for w in weights],
        out_specs=tok(D_MODEL),
        out_shape=jax.ShapeDtypeStruct((bsz, t, D_MODEL), _F32),
        compiler_params=_params("parallel", "parallel"),
        name="post_mlp",
    )(x, y_ssm, u, o_attn, mod, *weights)


def kernel(x_prompt, x_sample, cache_cmp, cache_slc, state_win, state_ssm_re, state_ssm_im, page_table,
           c_prompt, c_sample, w_ada, b_ada, norm_attn, w_in, ssm_lambda_re, ssm_lambda_im, ssm_log_dt,
           ssm_b_re, ssm_b_im, ssm_c_re, ssm_c_im, ssm_d, ssm_w_glu, cmp_pe_k, cmp_w1_k, cmp_w2_k,
           cmp_pe_v, cmp_w1_v, cmp_w2_v, norm_out_ssm, norm_out_attn, w_out, norm_mlp, w_up, w_down,
           norm_final):
    depth = w_ada.shape[0]
    assert depth == 1, "single-layer trunk"
    l = 0
    bsz, t, _ = x_prompt.shape
    nseq = x_sample.shape[0]
    assert x_sample.shape[1] == 1
    n_phys = cache_cmp.shape[1]
    n_buf = state_win.shape[2]
    tm = min(512, t)
    tq = min(128, t)
    kc_len = min(512, t)

    lp = dict(ssm_d=ssm_d[l], ssm_w_glu=ssm_w_glu[l], norm_out_ssm=norm_out_ssm[l],
              norm_out_attn=norm_out_attn[l], w_out=w_out[l], norm_mlp=norm_mlp[l], w_up=w_up[l], w_down=w_down[l])
    w_in_b = jnp.pad(w_in[l], ((0, 0), (0, D_IN_PAD - D_IN))).astype(_BF16)
    tabs = _ssm_tables(ssm_lambda_re[l], ssm_lambda_im[l], ssm_log_dt[l], ssm_b_re[l], ssm_b_im[l],
                       ssm_c_re[l], ssm_c_im[l])
    ctabs = _compress_tables(cmp_pe_k[l], cmp_w1_k[l], cmp_w2_k[l], cmp_pe_v[l], cmp_w1_v[l], cmp_w2_v[l])

    mod = _ada_mod(jnp.concatenate([c_prompt, c_sample], axis=0), w_ada[l], b_ada[l])
    mod_p = mod[:bsz].reshape(bsz, 6, 1, D_MODEL)
    mod_s = mod[bsz:].reshape(nseq, 6, D_MODEL).transpose(1, 0, 2)[None]

    u, q, kv_cmp, kv_slc, kv_win, gates = _in_proj(x_prompt, mod_p, norm_attn[l], w_in_b, tm)
    y_ssm, sp_re, sp_im = _ssm_prompt(u, tabs)
    kcmp = _compress_prompt(kv_cmp, ctabs)
    o_attn = _nsa_prompt(q, gates, kcmp, kv_slc, kv_win, tq, kc_len)
    y_prompt = _post(x_prompt, y_ssm, u, o_attn, mod_p, lp, norm_final, tm)
    kv_shape = (1, bsz, t, 2, N_KV_HEADS, HEAD_DIM)
    w_keep = min(WINDOW, t)

    xs = x_sample.reshape(1, nseq, D_MODEL)
    u_s, q_s, kvc_s, kvs_s, kvw_s, gates_s = _in_proj(xs, mod_s, norm_attn[l], w_in_b, nseq)
    ns = N_SSM_GROUPS * SSM_STATE
    y_s, ss_re, ss_im = _ssm_step(u_s[0], state_ssm_re[l].reshape(nseq, ns), state_ssm_im[l].reshape(nseq, ns), tabs)
    kcmp_s = _compress_paged(cache_cmp[l].reshape(n_phys, PAGE_SIZE, KV_ROW), page_table, ctabs)
    o_attn_s, new_win = _nsa_sample(q_s.reshape(nseq, 1, D_ATTN), gates_s.reshape(nseq, 1, LANES), kcmp_s,
                                    cache_slc[l].reshape(n_phys, PAGE_SIZE, KV_ROW), page_table,
                                    kvs_s.reshape(nseq, 1, KV_ROW), state_win[l].reshape(nseq, n_buf, KV_ROW),
                                    kvw_s.reshape(nseq, 1, KV_ROW))
    y_sample = _post(xs, y_s[None], u_s, o_attn_s.reshape(1, nseq, D_ATTN), mod_s, lp, norm_final, nseq)
    kv_shape_s = (1, nseq, 1, 2, N_KV_HEADS, HEAD_DIM)
    st_shape = (1, nseq, N_SSM_GROUPS, SSM_STATE)

    return (y_prompt, y_sample.reshape(nseq, 1, D_MODEL),
            kv_cmp.reshape(kv_shape), kv_slc.reshape(kv_shape),
            kv_win[:, t - w_keep:].reshape(1, bsz, w_keep, 2, N_KV_HEADS, HEAD_DIM),
            sp_re[None], sp_im[None],
            kvc_s.reshape(kv_shape_s), kvs_s.reshape(kv_shape_s),
            new_win.reshape(1, nseq, n_buf, 2, N_KV_HEADS, HEAD_DIM),
            ss_re.reshape(st_shape), ss_im.reshape(st_shape))
```

```python
import functools
import math

import jax
import jax.numpy as jnp
from jax import lax
from jax.experimental import pallas as pl
from jax.experimental.pallas import tpu as pltpu

D_MODEL = 1024
D_SSM = D_MODEL // 2
SSM_GROUP = 16
N_SSM_GROUPS = D_SSM // SSM_GROUP
SSM_STATE = 64
HEAD_DIM = 64
D_ATTN = D_MODEL - D_SSM
N_HEADS = D_ATTN // HEAD_DIM
N_KV_HEADS = 2
GROUP_SIZE = N_HEADS // N_KV_HEADS
D_KV = N_KV_HEADS * HEAD_DIM
CMP_BLOCK = 32
CMP_STRIDE = 16
CMP_HIDDEN = 2 * HEAD_DIM
SLC_BLOCK = 64
N_SELECT = 16
N_LOCAL = 2
WINDOW = 512
D_FF = 4 * D_MODEL
D_IN = D_SSM + D_ATTN + 6 * D_KV + 3 * N_HEADS
EPS = 1e-6
PAGE_SIZE = 128

LANES = 128
SUBLANES = 8
D_IN_PAD = -(-D_IN // LANES) * LANES
KV_SLOTS = 2 * N_KV_HEADS
KV_ROW = KV_SLOTS * HEAD_DIM
GATE_ROWS = 32
SSM_CHUNK = 16
SSM_TILE = SSM_CHUNK * SSM_GROUP
VMEM_LIMIT = 56 * 1024 * 1024
FORCED_SCORE = 1e30
NEG_INF = float("-inf")

_Q0, _KC0, _KS0, _KW0, _G0 = D_SSM, D_MODEL, D_MODEL + KV_ROW, D_MODEL + 2 * KV_ROW, D_MODEL + 3 * KV_ROW

_BF16 = jnp.bfloat16
_F32 = jnp.float32
_NT = (((1,), (1,)), ((), ()))
_HIGHEST = lax.Precision.HIGHEST


def _params(*sem):
    return pltpu.CompilerParams(dimension_semantics=sem, vmem_limit_bytes=VMEM_LIMIT)


def _rms(x, g):
    return x * lax.rsqrt(jnp.mean(x * x, axis=-1, keepdims=True) + EPS) * g


def _gelu_tanh(x):
    return x * (0.5 * (1.0 + jnp.tanh(math.sqrt(2.0 / math.pi) * (x + 0.044715 * (x * x * x)))))


def _sigmoid(x):
    return 1.0 / (1.0 + jnp.exp(-x))


def _bdot(a, b):
    return jnp.dot(a.astype(_BF16), b.astype(_BF16), preferred_element_type=_F32)


def _bdot_nt(a, b):
    return lax.dot_general(a.astype(_BF16), b.astype(_BF16), _NT, preferred_element_type=_F32)


def _ada_kernel(c_ref, w_ref, b_ref, o_ref):
    c = c_ref[...]
    o_ref[...] = jnp.dot(c * _sigmoid(c), w_ref[...], precision=_HIGHEST,
                         preferred_element_type=_F32) + b_ref[...]


def _ada_mod(c, w_ada, b_ada):
    n, tn = c.shape[0], D_MODEL
    return pl.pallas_call(
        _ada_kernel,
        grid=(6 * D_MODEL // tn,),
        in_specs=[pl.BlockSpec((n, D_MODEL), lambda j: (0, 0)),
                  pl.BlockSpec((D_MODEL, tn), lambda j: (0, j)),
                  pl.BlockSpec((1, tn), lambda j: (0, j))],
        out_specs=pl.BlockSpec((n, tn), lambda j: (0, j)),
        out_shape=jax.ShapeDtypeStruct((n, 6 * D_MODEL), _F32),
        compiler_params=_params("parallel"),
        name="ada_mod",
    )(c, w_ada, b_ada.reshape(1, -1))


def _inproj_kernel(x_ref, mod_ref, g_ref, wn_ref, wt_ref, *out_refs, outs):
    x = x_ref[0]
    h = (_rms(x, g_ref[...]) * (1.0 + mod_ref[0, 1]) + mod_ref[0, 0]).astype(_BF16)
    pn = jnp.dot(h, wn_ref[...], preferred_element_type=_F32)
    pt = lax.dot_general(wt_ref[...], h, _NT, preferred_element_type=_F32)
    for ref, (kind, lo, hi, post) in zip(out_refs, outs):
        v = pn[:, lo:hi] if kind == 'n' else pt[lo:hi, :]
        if post == 'scale':
            v = v * (HEAD_DIM ** -0.5)
        elif post == 'sigmoid':
            v = _sigmoid(v)
        ref[0] = v.astype(ref.dtype)


def _in_proj(x, mod, norm_attn, wn, wt, tm, outs, dtypes):
    bsz, t, _ = x.shape
    r = mod.shape[2]
    rb = 1 if r == 1 else tm
    mod_map = (lambda b, i: (b, 0, 0, 0)) if r == 1 else (lambda b, i: (b, 0, i, 0))
    out_specs, out_shape = [], []
    for (kind, lo, hi, _), dt in zip(outs, dtypes):
        if kind == 'n':
            out_specs.append(pl.BlockSpec((1, tm, hi - lo), lambda b, i: (b, i, 0)))
            out_shape.append(jax.ShapeDtypeStruct((bsz, t, hi - lo), dt))
        else:
            out_specs.append(pl.BlockSpec((1, hi - lo, tm), lambda b, i: (b, 0, i)))
            out_shape.append(jax.ShapeDtypeStruct((bsz, hi - lo, t), dt))
    return pl.pallas_call(
        functools.partial(_inproj_kernel, outs=tuple(outs)),
        grid=(bsz, t // tm),
        in_specs=[pl.BlockSpec((1, tm, D_MODEL), lambda b, i: (b, i, 0)),
                  pl.BlockSpec((1, 6, rb, D_MODEL), mod_map),
                  pl.BlockSpec((1, D_MODEL), lambda b, i: (0, 0)),
                  pl.BlockSpec(wn.shape, lambda b, i: (0, 0)),
                  pl.BlockSpec(wt.shape, lambda b, i: (0, 0))],
        out_specs=out_specs,
        out_shape=out_shape,
        compiler_params=_params("parallel", "parallel"),
        name="in_proj",
    )(x, mod, norm_attn.reshape(1, -1), wn, wt)


def _ssm_tables(lam_re, lam_im, log_dt, b_re, b_im, c_re, c_im):
    hp = dict(precision=_HIGHEST)
    lr, li = lam_re.astype(_F32), lam_im.astype(_F32)
    dt = jnp.exp(log_dt.astype(_F32))[:, None]
    ar, ai = lr * dt, li * dt

    def power(k):
        mag = jnp.exp(ar * k)
        return mag * jnp.cos(ai * k), mag * jnp.sin(ai * k)

    n = SSM_CHUNK
    steps = jnp.arange(n + 1, dtype=_F32)[:, None, None]
    pwr, pwi = power(steps)
    lbr, lbi = pwr[1], pwi[1]
    den = lr * lr + li * li
    fr = ((lbr - 1.0) * lr + lbi * li) / den
    fi = (lbi * lr - (lbr - 1.0) * li) / den
    br_, bi_ = b_re.astype(_F32), b_im.astype(_F32)
    bbr = fr[:, :, None] * br_ - fi[:, :, None] * bi_
    bbi = fr[:, :, None] * bi_ + fi[:, :, None] * br_
    cr, ci = c_re.astype(_F32), c_im.astype(_F32)
    er = pwr[:, :, :, None] * bbr[None] - pwi[:, :, :, None] * bbi[None]
    ei = pwr[:, :, :, None] * bbi[None] + pwi[:, :, :, None] * bbr[None]
    kern = (jnp.einsum('gip,dgpj->dgij', cr, er[:n], **hp)
            - jnp.einsum('gip,dgpj->dgij', ci, ei[:n], **hp))
    lag = jnp.arange(n)[None, :] - jnp.arange(n)[:, None]
    toep = jnp.where((lag >= 0)[:, :, None, None, None], kern[jnp.clip(lag, 0)], 0.0)
    toep = toep.transpose(2, 0, 4, 1, 3).reshape(N_SSM_GROUPS, SSM_TILE, SSM_TILE)
    rev = n - 1 - jnp.arange(n)
    w_in = jnp.concatenate([er[rev].transpose(1, 0, 3, 2), ei[rev].transpose(1, 0, 3, 2)], axis=-1)
    w_in = w_in.reshape(N_SSM_GROUPS, SSM_TILE, 2 * SSM_STATE)
    cvr = cr[None] * pwr[1:, :, None, :] - ci[None] * pwi[1:, :, None, :]
    cvi = cr[None] * pwi[1:, :, None, :] + ci[None] * pwr[1:, :, None, :]
    to_rows = lambda a: a.transpose(1, 3, 0, 2).reshape(N_SSM_GROUPS, SSM_STATE, SSM_TILE)
    v_out = jnp.concatenate([to_rows(cvr), -to_rows(cvi)], axis=1)
    a_chunk = jnp.stack([pwr[n], pwi[n]], axis=1)
    eye = jnp.eye(N_SSM_GROUPS, dtype=_F32)
    ns = N_SSM_GROUPS * SSM_STATE
    bd = lambda a: jnp.einsum('gpi,gh->gihp', a, eye).reshape(D_SSM, ns)
    cd = lambda a: jnp.einsum('gip,gh->gphi', a, eye).reshape(ns, D_SSM)
    return dict(toep=toep.astype(_BF16), w_in=w_in.astype(_BF16), v_out=v_out.astype(_BF16), a_chunk=a_chunk,
                bd_re=bd(bbr).astype(_BF16), bd_im=bd(bbi).astype(_BF16),
                cd_re=cd(cr).astype(_BF16), cd_im=cd(ci).astype(_BF16),
                lam_re=lbr.reshape(1, ns), lam_im=lbi.reshape(1, ns))


def _ssm_prompt_kernel(u_ref, toep_ref, w_ref, v_ref, a_ref, y_ref, sre_ref, sim_ref,
                       xre, xim, sin_re, sin_im, *, n_chunks, bsz):
    u = u_ref[0]
    x = jnp.dot(u, w_ref[0], preferred_element_type=_F32)
    xre[...] = x[:, :SSM_STATE]
    xim[...] = x[:, SSM_STATE:]
    a_re = a_ref[0, 0:1, :]
    a_im = a_ref[0, 1:2, :]

    def step(c, carry):
        s_re, s_im = carry
        rows = pl.ds(pl.multiple_of(c * bsz, bsz), bsz)
        sin_re[rows, :] = s_re
        sin_im[rows, :] = s_im
        return (a_re * s_re - a_im * s_im + xre[rows, :],
                a_re * s_im + a_im * s_re + xim[rows, :])

    zero = jnp.zeros((bsz, SSM_STATE), _F32)
    s_re, s_im = lax.fori_loop(0, n_chunks, step, (zero, zero))
    sre_ref[0] = s_re
    sim_ref[0] = s_im
    v = v_ref[0]
    y_ref[0] = (jnp.dot(u, toep_ref[0], preferred_element_type=_F32)
                + _bdot(sin_re[...], v[:SSM_STATE])
                + _bdot(sin_im[...], v[SSM_STATE:]))


def _ssm_prompt(u, tabs):
    bsz, t, _ = u.shape
    nch = t // SSM_CHUNK
    rows = nch * bsz
    u_t = (u.reshape(bsz, nch, SSM_CHUNK, N_SSM_GROUPS, SSM_GROUP).transpose(3, 1, 0, 2, 4)
           .reshape(N_SSM_GROUPS, rows, SSM_TILE).astype(_BF16))
    y_t, s_re, s_im = pl.pallas_call(
        functools.partial(_ssm_prompt_kernel, n_chunks=nch, bsz=bsz),
        grid=(N_SSM_GROUPS,),
        in_specs=[pl.BlockSpec((1, rows, SSM_TILE), lambda g: (g, 0, 0)),
                  pl.BlockSpec((1, SSM_TILE, SSM_TILE), lambda g: (g, 0, 0)),
                  pl.BlockSpec((1, SSM_TILE, 2 * SSM_STATE), lambda g: (g, 0, 0)),
                  pl.BlockSpec((1, 2 * SSM_STATE, SSM_TILE), lambda g: (g, 0, 0)),
                  pl.BlockSpec((1, 2, SSM_STATE), lambda g: (g, 0, 0))],
        out_specs=[pl.BlockSpec((1, rows, SSM_TILE), lambda g: (g, 0, 0)),
                   pl.BlockSpec((1, bsz, SSM_STATE), lambda g: (g, 0, 0)),
                   pl.BlockSpec((1, bsz, SSM_STATE), lambda g: (g, 0, 0))],
        out_shape=[jax.ShapeDtypeStruct((N_SSM_GROUPS, rows, SSM_TILE), _F32),
                   jax.ShapeDtypeStruct((N_SSM_GROUPS, bsz, SSM_STATE), _F32),
                   jax.ShapeDtypeStruct((N_SSM_GROUPS, bsz, SSM_STATE), _F32)],
        scratch_shapes=[pltpu.VMEM((rows, SSM_STATE), _F32) for _ in range(4)],
        compiler_params=_params("parallel"),
        name="ssm_prompt",
    )(u_t, tabs['toep'], tabs['w_in'], tabs['v_out'], tabs['a_chunk'])
    y = (y_t.reshape(N_SSM_GROUPS, nch, bsz, SSM_CHUNK, SSM_GROUP).transpose(2, 1, 3, 0, 4)
         .reshape(bsz, t, D_SSM))
    return y, s_re.transpose(1, 0, 2), s_im.transpose(1, 0, 2)


def _ssm_step_kernel(u_ref, s0re_ref, s0im_ref, lre_ref, lim_ref, bdre_ref, bdim_ref, cdre_ref, cdim_ref,
                     y_ref, sre_ref, sim_ref):
    u = u_ref[...]
    s_re, s_im = s0re_ref[...], s0im_ref[...]
    l_re, l_im = lre_ref[...], lim_ref[...]
    n_re = l_re * s_re - l_im * s_im + _bdot(u, bdre_ref[...])
    n_im = l_re * s_im + l_im * s_re + _bdot(u, bdim_ref[...])
    sre_ref[...] = n_re
    sim_ref[...] = n_im
    y_ref[...] = _bdot(n_re, cdre_ref[...]) - _bdot(n_im, cdim_ref[...])


def _ssm_step(u, s0_re, s0_im, tabs):
    n = u.shape[0]
    ns = N_SSM_GROUPS * SSM_STATE
    return pl.pallas_call(
        _ssm_step_kernel,
        out_shape=[jax.ShapeDtypeStruct((n, D_SSM), _F32),
                   jax.ShapeDtypeStruct((n, ns), _F32),
                   jax.ShapeDtypeStruct((n, ns), _F32)],
        compiler_params=pltpu.CompilerParams(vmem_limit_bytes=VMEM_LIMIT),
        name="ssm_step",
    )(u, s0_re, s0_im, tabs['lam_re'], tabs['lam_im'], tabs['bd_re'], tabs['bd_im'], tabs['cd_re'], tabs['cd_im'])


def _compress_tables(pe_k, w1_k, w2_k, pe_v, w1_v, w2_v):
    zeros = jnp.zeros((CMP_STRIDE, HEAD_DIM, CMP_HIDDEN), _F32)
    cols = []
    for half in range(2):
        for slot in range(KV_SLOTS):
            w1 = (w1_k if slot < N_KV_HEADS else w1_v)[half * CMP_STRIDE:(half + 1) * CMP_STRIDE]
            blk = jnp.stack([w1 if s == slot else zeros for s in range(KV_SLOTS)], axis=1)
            cols.append(blk.reshape(CMP_STRIDE * KV_ROW, CMP_HIDDEN))
    w_all = jnp.concatenate(cols, axis=1).astype(_BF16)
    z2 = jnp.zeros((CMP_HIDDEN, HEAD_DIM), _F32)
    w2 = jnp.concatenate(
        [jnp.concatenate([(w2_k if s < N_KV_HEADS else w2_v) if s == slot else z2 for s in range(KV_SLOTS)], axis=1)
         for slot in range(KV_SLOTS)], axis=0).astype(_BF16)
    pe = jnp.stack([pe_k.reshape(-1), pe_v.reshape(-1)], axis=0)
    pe = jnp.concatenate([pe, jnp.zeros((SUBLANES - 2, pe.shape[1]), _F32)], axis=0)
    w1f = jnp.stack([w1_k.reshape(-1, CMP_HIDDEN), w1_v.reshape(-1, CMP_HIDDEN)], axis=0)
    return w_all, w2, w2.T, pe, w1f


def _compress_tail(h, w2_ref, w2t_ref, pe_ref, w1f_ref, n_rows):
    hw = KV_SLOTS * CMP_HIDDEN
    pe = pe_ref[...]
    bias_k = jnp.dot(pe, w1f_ref[0], precision=_HIGHEST, preferred_element_type=_F32)[0:1]
    bias_v = jnp.dot(pe, w1f_ref[1], precision=_HIGHEST, preferred_element_type=_F32)[1:2]
    bias = jnp.concatenate([bias_k, bias_k, bias_v, bias_v], axis=1)
    nxt = pltpu.roll(h[:, hw:], n_rows - 1, 0)
    hid = _gelu_tanh(h[:, :hw] + nxt + bias)
    row = lax.broadcasted_iota(jnp.int32, hid.shape, 0)
    hid = jnp.where(row < n_rows - 1, hid, 0.0).astype(_BF16)
    return (jnp.dot(hid, w2_ref[...], preferred_element_type=_F32),
            lax.dot_general(w2t_ref[...], hid, _NT, preferred_element_type=_F32))


def _store_compressed(out, out_t, o_ref, ot_ref):
    for s in range(KV_SLOTS):
        o_ref[0, s] = out[:, s * HEAD_DIM:(s + 1) * HEAD_DIM]
        ot_ref[0, s] = out_t[s * HEAD_DIM:(s + 1) * HEAD_DIM, :]


def _compress_prompt_kernel(x_ref, wall_ref, w2_ref, w2t_ref, pe_ref, w1f_ref, o_ref, ot_ref, *, n_rows):
    h = jnp.dot(x_ref[0].astype(_BF16), wall_ref[...], preferred_element_type=_F32)
    out, out_t = _compress_tail(h, w2_ref, w2t_ref, pe_ref, w1f_ref, n_rows)
    _store_compressed(out, out_t, o_ref, ot_ref)


def _compress_out(nseq, nc, index_map, index_map_t):
    specs = [pl.BlockSpec((1, KV_SLOTS, nc, HEAD_DIM), index_map),
             pl.BlockSpec((1, KV_SLOTS, HEAD_DIM, nc), index_map_t)]
    shapes = [jax.ShapeDtypeStruct((nseq, KV_SLOTS, nc, HEAD_DIM), _F32),
              jax.ShapeDtypeStruct((nseq, KV_SLOTS, HEAD_DIM, nc), _F32)]
    return specs, shapes


def _compress_prompt(kv_cmp, ctabs):
    bsz, t, _ = kv_cmp.shape
    nc = t // CMP_STRIDE
    const = lambda a: pl.BlockSpec(a.shape, lambda b: (0,) * a.ndim)
    out_specs, out_shape = _compress_out(bsz, nc, lambda b: (b, 0, 0, 0), lambda b: (b, 0, 0, 0))
    return pl.pallas_call(
        functools.partial(_compress_prompt_kernel, n_rows=nc),
        grid=(bsz,),
        in_specs=[pl.BlockSpec((1, nc, CMP_STRIDE * KV_ROW), lambda b: (b, 0, 0))] + [const(a) for a in ctabs],
        out_specs=out_specs,
        out_shape=out_shape,
        compiler_params=_params("parallel"),
        name="compress_prompt",
    )(kv_cmp.reshape(bsz, nc, CMP_STRIDE * KV_ROW), *ctabs)


def _page_copies(pt_ref, cache_hbm, buf, sem, seq, slot, n_pages):
    return [pltpu.make_async_copy(cache_hbm.at[pt_ref[seq, j]],
                                  buf.at[slot, :, pl.ds(j * PAGE_SIZE, PAGE_SIZE)],
                                  sem.at[slot]) for j in range(n_pages)]


def _gather_pages(pt_ref, cache_hbm, buf, sem, n_pages):
    b = pl.program_id(0)
    slot = lax.rem(b, 2)

    @pl.when(b == 0)
    def _():
        for cp in _page_copies(pt_ref, cache_hbm, buf, sem, 0, 0, n_pages):
            cp.start()

    @pl.when(b + 1 < pl.num_programs(0))
    def _():
        for cp in _page_copies(pt_ref, cache_hbm, buf, sem, b + 1, 1 - slot, n_pages):
            cp.start()

    for cp in _page_copies(pt_ref, cache_hbm, buf, sem, b, slot, n_pages):
        cp.wait()
    return slot


def _chunk_permutation():
    span = 2 * PAGE_SIZE
    r = jnp.arange(span)
    src = (r % (span // CMP_STRIDE)) * CMP_STRIDE + r // (span // CMP_STRIDE)
    return (src[:, None] == jnp.arange(span)[None, :]).astype(_BF16)


def _compress_paged_kernel(pt_ref, cache_hbm, perm_ref, wall_ref, w2_ref, w2t_ref, pe_ref, w1f_ref,
                           o_ref, ot_ref, buf, sem, xr, *, n_pages, n_rows):
    slot = _gather_pages(pt_ref, cache_hbm, buf, sem, n_pages)
    span = 2 * PAGE_SIZE
    cps = span // CMP_STRIDE

    def relayout(i, carry):
        xt = buf[slot, :, pl.ds(pl.multiple_of(i * span, span), span)].astype(_BF16)
        rows = lax.dot_general(perm_ref[...], xt, _NT, preferred_element_type=_F32).astype(_BF16)
        for j in range(CMP_STRIDE):
            xr[j, pl.ds(pl.multiple_of(i * cps, cps), cps), :] = rows[j * cps:(j + 1) * cps, :]
        return carry

    lax.fori_loop(0, n_pages // 2, relayout, 0)
    h = jnp.zeros((n_rows, 2 * KV_SLOTS * CMP_HIDDEN), _F32)
    for j in range(CMP_STRIDE):
        h = h + jnp.dot(xr[j], wall_ref[j * KV_ROW:(j + 1) * KV_ROW, :], preferred_element_type=_F32)
    out, out_t = _compress_tail(h, w2_ref, w2t_ref, pe_ref, w1f_ref, n_rows)
    _store_compressed(out, out_t, o_ref, ot_ref)


def _compress_paged(cache_t, page_table, ctabs):
    nseq, n_pages = page_table.shape
    past = n_pages * PAGE_SIZE
    nc = past // CMP_STRIDE
    perm = _chunk_permutation()
    const = lambda a: pl.BlockSpec(a.shape, lambda b, pt: (0,) * a.ndim)
    out_specs, out_shape = _compress_out(nseq, nc, lambda b, pt: (b, 0, 0, 0), lambda b, pt: (b, 0, 0, 0))
    grid_spec = pltpu.PrefetchScalarGridSpec(
        num_scalar_prefetch=1,
        grid=(nseq,),
        in_specs=[pl.BlockSpec(memory_space=pl.ANY), const(perm)] + [const(a) for a in ctabs],
        out_specs=out_specs,
        scratch_shapes=[pltpu.VMEM((2, KV_ROW, past), _F32), pltpu.SemaphoreType.DMA((2,)),
                        pltpu.VMEM((CMP_STRIDE, nc, KV_ROW), _BF16)],
    )
    return pl.pallas_call(
        functools.partial(_compress_paged_kernel, n_pages=n_pages, n_rows=nc),
        grid_spec=grid_spec,
        out_shape=out_shape,
        compiler_params=_params("arbitrary"),
        name="compress_paged",
    )(page_table, cache_t, perm, *ctabs)


def _importance_map(n_cmp_rows, n_blocks_rows, n_cmp):
    ratio = SLC_BLOCK // CMP_STRIDE
    j = jnp.arange(n_blocks_rows)[:, None]
    n = jnp.arange(n_cmp_rows)[None, :]
    off = n - ratio * j
    w = jnp.where((off == -1) | (off == ratio - 1), 1.0, jnp.where((off >= 0) & (off < ratio - 1), 2.0, 0.0))
    return jnp.where(n < n_cmp, w, 0.0).astype(_F32)


def _softmax(s, ok, axis):
    s = jnp.where(ok, s, NEG_INF)
    m = jnp.max(s, axis=axis, keepdims=True)
    m = jnp.where(m == NEG_INF, 0.0, m)
    e = jnp.exp(s - m)
    den = jnp.sum(e, axis=axis, keepdims=True)
    return e / jnp.where(den > 0, den, 1.0)


def _nsa_prompt_kernel(q_ref, gt_ref, kc_ref, vct_ref, ksn_ref, vst_ref, kwn_ref, vwt_ref, mt_ref, o_ref,
                       bias_ref, *, tq, kc_len, n_cmp, n_blk):
    g = pl.program_id(1)
    t0 = pl.program_id(2) * tq
    cols = GROUP_SIZE * tq
    q_t = q_ref[0]
    q4 = jnp.concatenate([q_t[r * HEAD_DIM:(r + 1) * HEAD_DIM, :] for r in range(GROUP_SIZE)], axis=1)
    q4b = q4.astype(_BF16)
    feat = lax.broadcasted_iota(jnp.int32, (KV_ROW, cols), 0)
    qp = jnp.where(lax.shift_right_logical(feat, int(math.log2(HEAD_DIM))) == g,
                   jnp.concatenate([q4b] * KV_SLOTS, axis=0), jnp.zeros((), _BF16))
    lane = lax.broadcasted_iota(jnp.int32, (1, cols), 1)
    qpos = t0 + (lane & (tq - 1))

    ncr = kc_ref.shape[2]
    s_c = jnp.dot(kc_ref[0, 0], q4, precision=_HIGHEST, preferred_element_type=_F32)
    n_idx = lax.broadcasted_iota(jnp.int32, (ncr, 1), 0)
    cmp_end = jnp.where(n_idx < n_cmp, n_idx * CMP_STRIDE + (CMP_BLOCK - 1), jnp.int32(2 ** 30))
    p_c = _softmax(s_c, cmp_end <= qpos, 0)
    o_c = _bdot(vct_ref[0, 0], p_c)

    imp = p_c[:, 0:tq]
    for r in range(1, GROUP_SIZE):
        imp = imp + p_c[:, r * tq:(r + 1) * tq]
    imp_s = jnp.dot(mt_ref[...], imp, precision=_HIGHEST, preferred_element_type=_F32)
    qrow = qpos[:, :tq]
    j_idx = lax.broadcasted_iota(jnp.int32, (n_blk, tq), 0)
    valid = j_idx * SLC_BLOCK <= qrow
    cur = lax.shift_right_arithmetic(qrow, int(math.log2(SLC_BLOCK)))
    forced = (j_idx == 0) | (valid & (j_idx > cur - N_LOCAL))
    score = jnp.where(forced, FORCED_SCORE, jnp.where(valid, imp_s, -1.0))
    rank = jnp.zeros((n_blk, tq), jnp.int32)
    for j in range(n_blk):
        sj = score[j:j + 1, :]
        rank = rank + jnp.where(j_idx > j, jnp.where(sj >= score, 1, 0), jnp.where(sj > score, 1, 0))
    bias = jnp.where(valid & (rank < N_SELECT), 0.0, NEG_INF)
    bias_ref[...] = jnp.concatenate([bias] * GROUP_SIZE, axis=1)

    bpc = kc_len // SLC_BLOCK

    def chunk(c, carry, causal):
        m, l, acc = carry
        k0 = pl.multiple_of(c * kc_len, kc_len)
        s = jnp.dot(ksn_ref[0, pl.ds(k0, kc_len), :], qp, preferred_element_type=_F32)
        b8 = bias_ref[pl.ds(pl.multiple_of(c * bpc, bpc), bpc), :]
        s = jnp.concatenate([s[i * SLC_BLOCK:(i + 1) * SLC_BLOCK, :] + b8[i:i + 1, :] for i in range(bpc)], axis=0)
        if causal:
            kpos = k0 + lax.broadcasted_iota(jnp.int32, (kc_len, 1), 0)
            s = jnp.where(kpos <= qpos, s, NEG_INF)
        m_new = jnp.maximum(m, jnp.max(s, axis=0, keepdims=True))
        alpha = jnp.exp(m - m_new)
        p = jnp.exp(s - m_new)
        l = alpha * l + jnp.sum(p, axis=0, keepdims=True)
        acc = alpha * acc + jnp.dot(vst_ref[0, :, pl.ds(k0, kc_len)], p.astype(_BF16),
                                    preferred_element_type=_F32)
        return m_new, l, acc

    c_diag = t0 // kc_len
    init = (jnp.full((1, cols), NEG_INF, _F32), jnp.zeros((1, cols), _F32), jnp.zeros((HEAD_DIM, cols), _F32))
    carry = lax.fori_loop(0, c_diag, functools.partial(chunk, causal=False), init)
    _, l_s, acc_s = chunk(c_diag, carry, True)
    o_s = acc_s / l_s

    wk = tq + WINDOW
    w0 = pl.multiple_of(jnp.maximum(t0 - WINDOW, 0), tq)
    s_w = jnp.dot(kwn_ref[0, pl.ds(w0, wk), :], qp, preferred_element_type=_F32)
    dpos = qpos - (w0 + lax.broadcasted_iota(jnp.int32, (wk, 1), 0))
    p_w = _softmax(s_w, jnp.where(dpos >= 0, dpos, WINDOW) < WINDOW, 0)
    o_w = jnp.dot(vwt_ref[0, :, pl.ds(w0, wk)], p_w.astype(_BF16), preferred_element_type=_F32)

    outs = []
    for r in range(GROUP_SIZE):
        sl = slice(r * tq, (r + 1) * tq)
        gate = lambda branch: gt_ref[0, pl.ds(branch * N_HEADS + g * GROUP_SIZE + r, 1), :]
        outs.append(gate(0) * o_c[:, sl] + gate(1) * o_s[:, sl] + gate(2) * o_w[:, sl])
    o_ref[0] = jnp.concatenate(outs, axis=0).T


def _nsa_prompt(q_t, gates_t, kcmp, kcmp_t, ks_n, ks_tb, kw_n, kw_tb, tq, kc_len):
    bsz, _, t = q_t.shape
    ncr = kcmp.shape[2]
    n_blk = t // SLC_BLOCK
    mt = _importance_map(ncr, n_blk, ncr - 1)
    rows_n = pl.BlockSpec((1, t, KV_ROW), lambda b, g, i: (b, 0, 0))
    vals_t = pl.BlockSpec((1, HEAD_DIM, t), lambda b, g, i: (b, N_KV_HEADS + g, 0))
    qw = GROUP_SIZE * HEAD_DIM
    return pl.pallas_call(
        functools.partial(_nsa_prompt_kernel, tq=tq, kc_len=kc_len, n_cmp=ncr - 1, n_blk=n_blk),
        grid=(bsz, N_KV_HEADS, t // tq),
        in_specs=[pl.BlockSpec((1, qw, tq), lambda b, g, i: (b, g, i)),
                  pl.BlockSpec((1, GATE_ROWS, tq), lambda b, g, i: (b, 0, i)),
                  pl.BlockSpec((1, 1, ncr, HEAD_DIM), lambda b, g, i: (b, g, 0, 0)),
                  pl.BlockSpec((1, 1, HEAD_DIM, ncr), lambda b, g, i: (b, N_KV_HEADS + g, 0, 0)),
                  rows_n, vals_t, rows_n, vals_t,
                  pl.BlockSpec(mt.shape, lambda b, g, i: (0, 0))],
        out_specs=pl.BlockSpec((1, tq, qw), lambda b, g, i: (b, i, g)),
        out_shape=jax.ShapeDtypeStruct((bsz, t, D_ATTN), _F32),
        scratch_shapes=[pltpu.VMEM((n_blk, GROUP_SIZE * tq), _F32)],
        compiler_params=_params("parallel", "parallel", "parallel"),
        name="nsa_prompt",
    )(q_t, gates_t, kcmp, kcmp_t, ks_n, ks_tb, kw_n, kw_tb, mt)


def _nsa_sample_kernel(pt_ref, q_ref, gt_ref, kcmp_ref, kcmpt_ref, slc_hbm, ksn_ref, win_ref, kwt_ref,
                       m_ref, e_ref, o_ref, nw_ref, buf, sem, xb,
                       *, n_pages, past, n_cmp, blk_pad, n_buf):
    b = pl.program_id(0)
    slot = _gather_pages(pt_ref, slc_hbm, buf, sem, n_pages)
    xb[...] = buf[slot].astype(_BF16)
    q = q_ref[0]
    gt = gt_ref[0]
    ks_new = ksn_ref[0]
    nseq = kwt_ref.shape[1]
    seq_lane = lax.broadcasted_iota(jnp.int32, (KV_ROW, nseq), 1)
    kw_col = jnp.sum(jnp.where(seq_lane == b, kwt_ref[...], 0.0), axis=1, keepdims=True)
    win = win_ref[0]
    pos = lax.broadcasted_iota(jnp.int32, win.shape, 1)
    band = jnp.where(pos == n_buf - 1, kw_col, pltpu.roll(win, n_buf - 1, 1))
    nw_ref[0] = band
    band_b = band.astype(_BF16)
    ncr = kcmp_ref.shape[2]
    lane = lax.broadcasted_iota(jnp.int32, (SUBLANES, LANES), 1)
    hrow = lax.broadcasted_iota(jnp.int32, (SUBLANES, LANES), 0)
    heads = []
    for g in range(N_KV_HEADS):
        q8 = jnp.concatenate([q[:, (g * GROUP_SIZE + r) * HEAD_DIM:(g * GROUP_SIZE + r + 1) * HEAD_DIM]
                              for r in range(GROUP_SIZE)]
                             + [jnp.zeros((SUBLANES - GROUP_SIZE, HEAD_DIM), _F32)], axis=0)
        q8b = q8.astype(_BF16)
        k_rows = slice(g * HEAD_DIM, (g + 1) * HEAD_DIM)
        v_rows = slice((N_KV_HEADS + g) * HEAD_DIM, (N_KV_HEADS + g + 1) * HEAD_DIM)

        s_c = lax.dot_general(q8, kcmp_ref[0, g], _NT, precision=_HIGHEST, preferred_element_type=_F32)
        n_idx = lax.broadcasted_iota(jnp.int32, (SUBLANES, ncr), 1)
        cmp_end = jnp.where(n_idx < n_cmp, n_idx * CMP_STRIDE + (CMP_BLOCK - 1), jnp.int32(2 ** 30))
        p_c = _softmax(s_c, cmp_end <= past, 1)
        o_c = _bdot_nt(p_c, kcmpt_ref[0, N_KV_HEADS + g])

        imp = jnp.sum(p_c[:GROUP_SIZE], axis=0, keepdims=True)
        imp_s = jnp.dot(jnp.broadcast_to(imp, (SUBLANES, ncr)), m_ref[...], precision=_HIGHEST,
                        preferred_element_type=_F32)[0:1]
        j_idx = lax.broadcasted_iota(jnp.int32, (1, blk_pad), 1)
        valid = j_idx * SLC_BLOCK <= past
        forced = (j_idx == 0) | (valid & (j_idx > past // SLC_BLOCK - N_LOCAL))
        score = jnp.where(forced, FORCED_SCORE, jnp.where(valid, imp_s, -1.0))
        s_k = jnp.broadcast_to(score, (blk_pad, blk_pad))
        s_j = s_k.T
        jj = lax.broadcasted_iota(jnp.int32, (blk_pad, blk_pad), 0)
        kk = lax.broadcasted_iota(jnp.int32, (blk_pad, blk_pad), 1)
        beats = jnp.where(kk > jj, jnp.where(s_j >= s_k, 1.0, 0.0), jnp.where(s_j > s_k, 1.0, 0.0))
        rank = jnp.sum(beats, axis=0, keepdims=True)
        sel = jnp.where(valid & (rank < N_SELECT), 1.0, 0.0)
        picked = jnp.dot(jnp.broadcast_to(sel, (SUBLANES, blk_pad)).astype(_BF16), e_ref[...],
                         preferred_element_type=_F32) > 0.5

        s_s = jnp.where(picked, jnp.dot(q8b, xb[k_rows, :], preferred_element_type=_F32), NEG_INF)
        s_n = jnp.sum(q8 * ks_new[:, k_rows], axis=-1, keepdims=True)
        m = jnp.maximum(jnp.max(s_s, axis=-1, keepdims=True), s_n)
        e_p, e_n = jnp.exp(s_s - m), jnp.exp(s_n - m)
        num = _bdot_nt(e_p, xb[v_rows, :])
        o_s = (num + e_n * ks_new[:, v_rows]) / (jnp.sum(e_p, axis=-1, keepdims=True) + e_n)

        s_w = jnp.dot(q8b, band_b[k_rows, :], preferred_element_type=_F32)
        i_idx = lax.broadcasted_iota(jnp.int32, (SUBLANES, n_buf), 1)
        p_w = _softmax(s_w, (n_buf - 1 - i_idx) < WINDOW, 1)
        o_w = _bdot_nt(p_w, band_b[v_rows, :])

        def gate(branch):
            tgt = branch * N_HEADS + g * GROUP_SIZE + hrow
            return jnp.sum(jnp.where(lane == tgt, jnp.broadcast_to(gt, (SUBLANES, LANES)), 0.0),
                           axis=-1, keepdims=True)

        o = gate(0) * o_c + gate(1) * o_s + gate(2) * o_w
        heads += [o[r:r + 1, :] for r in range(GROUP_SIZE)]
    o_ref[0] = jnp.concatenate(heads, axis=1)


def _expand_table(past, blk_pad):
    return (jnp.arange(past)[None, :] // SLC_BLOCK == jnp.arange(blk_pad)[:, None]).astype(_BF16)


def _nsa_sample(q, gates, kcmp, kcmp_t, cache_t, page_table, ks_new, win_t, kw_t):
    nseq, n_pages = page_table.shape
    past = n_pages * PAGE_SIZE
    ncr = kcmp.shape[2]
    n_buf = win_t.shape[2]
    n_blk = past // SLC_BLOCK + 1
    blk_pad = -(-n_blk // LANES) * LANES
    m_tab = _importance_map(ncr, blk_pad, ncr - 1).T
    e = _expand_table(past, blk_pad)
    row = lambda w: pl.BlockSpec((1, 1, w), lambda b, pt: (b, 0, 0))
    const = lambda a: pl.BlockSpec(a.shape, lambda b, pt: (0,) * a.ndim)
    win_spec = pl.BlockSpec((1, KV_ROW, n_buf), lambda b, pt: (b, 0, 0))
    grid_spec = pltpu.PrefetchScalarGridSpec(
        num_scalar_prefetch=1,
        grid=(nseq,),
        in_specs=[row(D_ATTN), row(LANES),
                  pl.BlockSpec((1, KV_SLOTS, ncr, HEAD_DIM), lambda b, pt: (b, 0, 0, 0)),
                  pl.BlockSpec((1, KV_SLOTS, HEAD_DIM, ncr), lambda b, pt: (b, 0, 0, 0)),
                  pl.BlockSpec(memory_space=pl.ANY),
                  row(KV_ROW), win_spec, const(kw_t), const(m_tab), const(e)],
        out_specs=[row(D_ATTN), win_spec],
        scratch_shapes=[pltpu.VMEM((2, KV_ROW, past), _F32), pltpu.SemaphoreType.DMA((2,)),
                        pltpu.VMEM((KV_ROW, past), _BF16)],
    )
    return pl.pallas_call(
        functools.partial(_nsa_sample_kernel, n_pages=n_pages, past=past, n_cmp=ncr - 1,
                          blk_pad=blk_pad, n_buf=n_buf),
        grid_spec=grid_spec,
        out_shape=[jax.ShapeDtypeStruct((nseq, 1, D_ATTN), _F32),
                   jax.ShapeDtypeStruct((nseq, KV_ROW, n_buf), _F32)],
        compiler_params=_params("arbitrary"),
        name="nsa_sample",
    )(page_table, q, gates, kcmp, kcmp_t, cache_t, ks_new, win_t, kw_t, m_tab, e)


def _post_kernel(x_ref, y_ref, u_ref, oa_ref, mod_ref, d_ref, wglu_ref, gs_ref, ga_ref, wout_ref, gm_ref,
                 wup_ref, wdn_ref, gf_ref, o_ref, *, ff_chunk):
    z = _gelu_tanh(y_ref[0] + d_ref[...] * u_ref[0])
    o_ssm = z * _sigmoid(jnp.dot(z.astype(_BF16), wglu_ref[...], preferred_element_type=_F32))
    mix = (jnp.dot(_rms(o_ssm, gs_ref[...]).astype(_BF16), wout_ref[:D_SSM, :], preferred_element_type=_F32)
           + jnp.dot(_rms(oa_ref[0], ga_ref[...]).astype(_BF16), wout_ref[D_SSM:, :], preferred_element_type=_F32))
    x1 = x_ref[0] + mod_ref[0, 2] * mix
    h2 = (_rms(x1, gm_ref[...]) * (1.0 + mod_ref[0, 4]) + mod_ref[0, 3]).astype(_BF16)
    acc = jnp.zeros_like(x1)
    for k in range(D_FF // ff_chunk):
        cols = slice(k * ff_chunk, (k + 1) * ff_chunk)
        hid = jnp.maximum(jnp.dot(h2, wup_ref[:, cols], preferred_element_type=_F32), 0.0)
        acc = acc + jnp.dot((hid * hid).astype(_BF16), wdn_ref[cols, :], preferred_element_type=_F32)
    o_ref[0] = _rms(x1 + mod_ref[0, 5] * acc, gf_ref[...])


def _post(x, y_ssm, u, o_attn, mod, lp, norm_final, tm):
    bsz, t, _ = x.shape
    r = mod.shape[2]
    rb = 1 if r == 1 else tm
    mod_map = (lambda b, i: (b, 0, 0, 0)) if r == 1 else (lambda b, i: (b, 0, i, 0))
    tok = lambda w: pl.BlockSpec((1, tm, w), lambda b, i: (b, i, 0))
    const = lambda shape: pl.BlockSpec(shape, lambda b, i: (0,) * len(shape), pipeline_mode=pl.Buffered(1))
    vec = lambda v: v.reshape(1, -1).astype(_F32)
    weights = [vec(lp['ssm_d']), lp['ssm_w_glu'].astype(_BF16), vec(lp['norm_out_ssm']), vec(lp['norm_out_attn']),
               lp['w_out'].astype(_BF16), vec(lp['norm_mlp']), lp['w_up'].astype(_BF16), lp['w_down'].astype(_BF16),
               vec(norm_final)]
    return pl.pallas_call(
        functools.partial(_post_kernel, ff_chunk=D_MODEL),
        grid=(bsz, t // tm),
        in_specs=[tok(D_MODEL), tok(D_SSM), tok(D_SSM), tok(D_ATTN),
                  pl.BlockSpec((1, 6, rb, D_MODEL), mod_map)] + [const(w.shape) for w in weights],
        out_specs=tok(D_MODEL),
        out_shape=jax.ShapeDtypeStruct((bsz, t, D_MODEL), _F32),
        compiler_params=_params("parallel", "parallel"),
        name="post_mlp",
    )(x, y_ssm, u, o_attn, mod, *weights)


def _feature_major(a):
    lead = a.shape[:-4]
    n = len(lead)
    return a.transpose(*range(n), n + 1, n + 2, n + 3, n).reshape(*lead, KV_ROW, a.shape[-4])


def _kv_output(a_t):
    n, _, t = a_t.shape
    return a_t.reshape(n, 2, N_KV_HEADS, HEAD_DIM, t).transpose(0, 4, 1, 2, 3)[None]


def kernel(x_prompt, x_sample, cache_cmp, cache_slc, state_win, state_ssm_re, state_ssm_im, page_table,
           c_prompt, c_sample, w_ada, b_ada, norm_attn, w_in, ssm_lambda_re, ssm_lambda_im, ssm_log_dt,
           ssm_b_re, ssm_b_im, ssm_c_re, ssm_c_im, ssm_d, ssm_w_glu, cmp_pe_k, cmp_w1_k, cmp_w2_k,
           cmp_pe_v, cmp_w1_v, cmp_w2_v, norm_out_ssm, norm_out_attn, w_out, norm_mlp, w_up, w_down,
           norm_final):
    depth = w_ada.shape[0]
    assert depth == 1, "single-layer trunk"
    l = 0
    bsz, t, _ = x_prompt.shape
    nseq = x_sample.shape[0]
    assert x_sample.shape[1] == 1
    tm = min(512, t)
    tq = min(128, t)
    kc_len = min(512, t)

    lp = dict(ssm_d=ssm_d[l], ssm_w_glu=ssm_w_glu[l], norm_out_ssm=norm_out_ssm[l],
              norm_out_attn=norm_out_attn[l], w_out=w_out[l], norm_mlp=norm_mlp[l], w_up=w_up[l], w_down=w_down[l])
    w_full = jnp.pad(w_in[l], ((0, 0), (0, D_IN_PAD - D_IN))).astype(_BF16)
    w_rows = jnp.concatenate([w_full[:, _Q0:_G0 + 3 * N_HEADS].T,
                              jnp.zeros((GATE_ROWS - 3 * N_HEADS, D_MODEL), _BF16)], axis=0)
    tq0, tkc0, tks0, tkw0, tg0 = 0, D_ATTN, D_ATTN + KV_ROW, D_ATTN + 2 * KV_ROW, D_ATTN + 3 * KV_ROW
    tabs = _ssm_tables(ssm_lambda_re[l], ssm_lambda_im[l], ssm_log_dt[l], ssm_b_re[l], ssm_b_im[l],
                       ssm_c_re[l], ssm_c_im[l])
    ctabs = _compress_tables(cmp_pe_k[l], cmp_w1_k[l], cmp_w2_k[l], cmp_pe_v[l], cmp_w1_v[l], cmp_w2_v[l])

    mod = _ada_mod(jnp.concatenate([c_prompt, c_sample], axis=0), w_ada[l], b_ada[l])
    mod_p = mod[:bsz].reshape(bsz, 6, 1, D_MODEL)
    mod_s = mod[bsz:].reshape(nseq, 6, D_MODEL).transpose(1, 0, 2)[None]

    w_tok = jnp.concatenate([w_full[:, :D_SSM], w_full[:, _KC0:_G0]], axis=1)
    outs_p = [('n', 0, D_SSM, None), ('n', D_SSM, D_SSM + KV_ROW, None),
              ('n', D_SSM + KV_ROW, D_SSM + 2 * KV_ROW, None), ('n', D_SSM + 2 * KV_ROW, D_SSM + 3 * KV_ROW, None),
              ('t', tq0, tkc0, 'scale'), ('t', tkc0, tks0, None), ('t', tks0, tkw0, None), ('t', tkw0, tg0, None),
              ('t', tks0, tkw0, None), ('t', tkw0, tg0, None), ('t', tg0, tg0 + GATE_ROWS, 'sigmoid')]
    dt_p = [_F32, _F32, _BF16, _BF16, _F32, _F32, _F32, _F32, _BF16, _BF16, _F32]
    (u, kc_n, ks_n, kw_n, q_t, kc_t, ks_t, kw_t, ks_tb, kw_tb, gates_t) = _in_proj(
        x_prompt, mod_p, norm_attn[l], w_tok, w_rows, tm, outs_p, dt_p)
    y_ssm, sp_re, sp_im = _ssm_prompt(u, tabs)
    kcmp, kcmp_t = _compress_prompt(kc_n, ctabs)
    o_attn = _nsa_prompt(q_t, gates_t, kcmp, kcmp_t, ks_n, ks_tb, kw_n, kw_tb, tq, kc_len)
    y_prompt = _post(x_prompt, y_ssm, u, o_attn, mod_p, lp, norm_final, tm)
    w_keep = min(WINDOW, t)

    xs = x_sample.reshape(1, nseq, D_MODEL)
    outs_s = [('n', 0, D_SSM, None), ('n', _Q0, _KC0, 'scale'), ('n', _KS0, _KW0, None), ('n', _G0, D_IN_PAD, 'sigmoid'),
              ('t', tkc0, tks0, None), ('t', tks0, tkw0, None), ('t', tkw0, tg0, None)]
    u_s, q_s, ks_row, gates_s, kc_ts, ks_ts, kw_ts = _in_proj(
        xs, mod_s, norm_attn[l], w_full, w_rows, nseq, outs_s, [_F32] * len(outs_s))
    ns = N_SSM_GROUPS * SSM_STATE
    y_s, ss_re, ss_im = _ssm_step(u_s[0], state_ssm_re[l].reshape(nseq, ns), state_ssm_im[l].reshape(nseq, ns), tabs)
    kcmp_s, kcmp_ts = _compress_paged(_feature_major(cache_cmp[l]), page_table, ctabs)
    o_attn_s, new_win_t = _nsa_sample(q_s.reshape(nseq, 1, D_ATTN), gates_s.reshape(nseq, 1, LANES), kcmp_s, kcmp_ts,
                                      _feature_major(cache_slc[l]), page_table, ks_row.reshape(nseq, 1, KV_ROW),
                                      _feature_major(state_win[l]), kw_ts[0])
    y_sample = _post(xs, y_s[None], u_s, o_attn_s.reshape(1, nseq, D_ATTN), mod_s, lp, norm_final, nseq)
    st_shape = (1, nseq, N_SSM_GROUPS, SSM_STATE)
    new_row = lambda a_t: _kv_output(a_t[0].T[:, :, None])

    return (y_prompt, y_sample.reshape(nseq, 1, D_MODEL),
            _kv_output(kc_t), _kv_output(ks_t), _kv_output(kw_t[:, :, t - w_keep:]),
            sp_re[None], sp_im[None],
            new_row(kc_ts), new_row(ks_ts), _kv_output(new_win_t),
            ss_re.reshape(st_shape), ss_im.reshape(st_shape))
```

```python
import functools
import math

import jax
import jax.numpy as jnp
from jax import lax
from jax.experimental import pallas as pl
from jax.experimental.pallas import tpu as pltpu

D_MODEL = 1024
D_SSM = D_MODEL // 2
SSM_GROUP = 16
N_SSM_GROUPS = D_SSM // SSM_GROUP
SSM_STATE = 64
HEAD_DIM = 64
D_ATTN = D_MODEL - D_SSM
N_HEADS = D_ATTN // HEAD_DIM
N_KV_HEADS = 2
GROUP_SIZE = N_HEADS // N_KV_HEADS
D_KV = N_KV_HEADS * HEAD_DIM
CMP_BLOCK = 32
CMP_STRIDE = 16
CMP_HIDDEN = 2 * HEAD_DIM
SLC_BLOCK = 64
N_SELECT = 16
N_LOCAL = 2
WINDOW = 512
D_FF = 4 * D_MODEL
D_IN = D_SSM + D_ATTN + 6 * D_KV + 3 * N_HEADS
EPS = 1e-6
PAGE_SIZE = 128

LANES = 128
SUBLANES = 8
D_IN_PAD = -(-D_IN // LANES) * LANES
KV_SLOTS = 2 * N_KV_HEADS
KV_ROW = KV_SLOTS * HEAD_DIM
GATE_ROWS = 32
SSM_CHUNK = 16
SSM_LANE_GROUPS = LANES // SSM_GROUP
VMEM_LIMIT = 56 * 1024 * 1024
FORCED_SCORE = 1e30
NEG_INF = float("-inf")
LOG2_E = math.log2(math.e)
RELAYOUT_UNROLL = 8

_Q0, _KC0, _KS0, _KW0, _G0 = D_SSM, D_MODEL, D_MODEL + KV_ROW, D_MODEL + 2 * KV_ROW, D_MODEL + 3 * KV_ROW

_BF16 = jnp.bfloat16
_F32 = jnp.float32
_NT = (((1,), (1,)), ((), ()))
_HIGHEST = lax.Precision.HIGHEST


def _params(*sem):
    return pltpu.CompilerParams(dimension_semantics=sem, vmem_limit_bytes=VMEM_LIMIT)


def _rms(x, g):
    return x * lax.rsqrt(jnp.mean(x * x, axis=-1, keepdims=True) + EPS) * g


def _gelu_tanh(x):
    return x * (0.5 * (1.0 + jnp.tanh(math.sqrt(2.0 / math.pi) * (x + 0.044715 * (x * x * x)))))


def _sigmoid(x):
    return 1.0 / (1.0 + jnp.exp(-x))


def _bdot(a, b):
    return jnp.dot(a.astype(_BF16), b.astype(_BF16), preferred_element_type=_F32)


def _bdot_nt(a, b):
    return lax.dot_general(a.astype(_BF16), b.astype(_BF16), _NT, preferred_element_type=_F32)


def _ada_kernel(c_ref, w_ref, b_ref, o_ref):
    c = c_ref[...]
    o_ref[...] = jnp.dot(c * _sigmoid(c), w_ref[...], precision=_HIGHEST,
                         preferred_element_type=_F32) + b_ref[...]


def _ada_mod(c, w_ada, b_ada):
    n, tn = c.shape[0], D_MODEL
    return pl.pallas_call(
        _ada_kernel,
        grid=(6 * D_MODEL // tn,),
        in_specs=[pl.BlockSpec((n, D_MODEL), lambda j: (0, 0)),
                  pl.BlockSpec((D_MODEL, tn), lambda j: (0, j)),
                  pl.BlockSpec((1, tn), lambda j: (0, j))],
        out_specs=pl.BlockSpec((n, tn), lambda j: (0, j)),
        out_shape=jax.ShapeDtypeStruct((n, 6 * D_MODEL), _F32),
        compiler_params=_params("parallel"),
        name="ada_mod",
    )(c, w_ada, b_ada.reshape(1, -1))


def _inproj_kernel(x_ref, mod_ref, g_ref, wn_ref, wt_ref, *out_refs, outs):
    x = x_ref[0]
    h = (_rms(x, g_ref[...]) * (1.0 + mod_ref[0, 1]) + mod_ref[0, 0]).astype(_BF16)
    pn = jnp.dot(h, wn_ref[...], preferred_element_type=_F32)
    pt = lax.dot_general(wt_ref[...], h, _NT, preferred_element_type=_F32)
    for ref, (kind, lo, hi, post) in zip(out_refs, outs):
        v = pn[:, lo:hi] if kind == 'n' else pt[lo:hi, :]
        if post == 'scale':
            v = v * (HEAD_DIM ** -0.5)
        elif post == 'sigmoid':
            v = _sigmoid(v)
        ref[0] = v.astype(ref.dtype)


def _in_proj(x, mod, norm_attn, wn, wt, tm, outs, dtypes):
    bsz, t, _ = x.shape
    r = mod.shape[2]
    rb = 1 if r == 1 else tm
    mod_map = (lambda b, i: (b, 0, 0, 0)) if r == 1 else (lambda b, i: (b, 0, i, 0))
    out_specs, out_shape = [], []
    for (kind, lo, hi, _), dt in zip(outs, dtypes):
        if kind == 'n':
            out_specs.append(pl.BlockSpec((1, tm, hi - lo), lambda b, i: (b, i, 0)))
            out_shape.append(jax.ShapeDtypeStruct((bsz, t, hi - lo), dt))
        else:
            out_specs.append(pl.BlockSpec((1, hi - lo, tm), lambda b, i: (b, 0, i)))
            out_shape.append(jax.ShapeDtypeStruct((bsz, hi - lo, t), dt))
    return pl.pallas_call(
        functools.partial(_inproj_kernel, outs=tuple(outs)),
        grid=(bsz, t // tm),
        in_specs=[pl.BlockSpec((1, tm, D_MODEL), lambda b, i: (b, i, 0)),
                  pl.BlockSpec((1, 6, rb, D_MODEL), mod_map),
                  pl.BlockSpec((1, D_MODEL), lambda b, i: (0, 0)),
                  pl.BlockSpec(wn.shape, lambda b, i: (0, 0)),
                  pl.BlockSpec(wt.shape, lambda b, i: (0, 0))],
        out_specs=out_specs,
        out_shape=out_shape,
        compiler_params=_params("parallel", "parallel"),
        name="in_proj",
    )(x, mod, norm_attn.reshape(1, -1), wn, wt)


def _ssm_tables(lam_re, lam_im, log_dt, b_re, b_im, c_re, c_im):
    hp = dict(precision=_HIGHEST)
    lr, li = lam_re.astype(_F32), lam_im.astype(_F32)
    dt = jnp.exp(log_dt.astype(_F32))[:, None]
    ar, ai = lr * dt, li * dt

    def power(k):
        mag = jnp.exp(ar * k)
        return mag * jnp.cos(ai * k), mag * jnp.sin(ai * k)

    n = SSM_CHUNK
    steps = jnp.arange(n + 1, dtype=_F32)[:, None, None]
    pwr, pwi = power(steps)
    lbr, lbi = pwr[1], pwi[1]
    den = lr * lr + li * li
    fr = ((lbr - 1.0) * lr + lbi * li) / den
    fi = (lbi * lr - (lbr - 1.0) * li) / den
    br_, bi_ = b_re.astype(_F32), b_im.astype(_F32)
    bbr = fr[:, :, None] * br_ - fi[:, :, None] * bi_
    bbi = fr[:, :, None] * bi_ + fi[:, :, None] * br_
    cr, ci = c_re.astype(_F32), c_im.astype(_F32)
    er = pwr[:, :, :, None] * bbr[None] - pwi[:, :, :, None] * bbi[None]
    ei = pwr[:, :, :, None] * bbi[None] + pwi[:, :, :, None] * bbr[None]
    kern = (jnp.einsum('gip,dgpj->dgij', cr, er[:n], **hp)
            - jnp.einsum('gip,dgpj->dgij', ci, ei[:n], **hp))
    lag = jnp.arange(n)[None, :] - jnp.arange(n)[:, None]
    toep = jnp.where((lag >= 0)[:, :, None, None, None], kern[jnp.clip(lag, 0)], 0.0)
    nb, gpb = N_SSM_GROUPS // SSM_LANE_GROUPS, SSM_LANE_GROUPS
    eye_b = jnp.eye(gpb, dtype=_F32)
    toep = jnp.einsum('atmgij,gh->magjthi', toep.reshape(n, n, nb, gpb, SSM_GROUP, SSM_GROUP), eye_b)
    toep = toep.reshape(nb, n * LANES, n * LANES)
    rev = n - 1 - jnp.arange(n)

    def state_in(e):
        w = e[rev].reshape(n, nb, gpb, SSM_STATE, SSM_GROUP)
        return jnp.einsum('amgpj,gh->magjhp', w, eye_b).reshape(nb, n * LANES, gpb * SSM_STATE)

    w_in = jnp.concatenate([state_in(er), state_in(ei)], axis=-1)
    cvr = cr[None] * pwr[1:, :, None, :] - ci[None] * pwi[1:, :, None, :]
    cvi = cr[None] * pwi[1:, :, None, :] + ci[None] * pwr[1:, :, None, :]

    def state_out(cv):
        v = cv.reshape(n, nb, gpb, SSM_GROUP, SSM_STATE)
        return jnp.einsum('tmgip,gh->mgpthi', v, eye_b).reshape(nb, gpb * SSM_STATE, n * LANES)

    v_out = jnp.concatenate([state_out(cvr), -state_out(cvi)], axis=1)
    a_chunk = jnp.stack([pwr[n].reshape(nb, -1), pwi[n].reshape(nb, -1)], axis=1)
    eye = jnp.eye(N_SSM_GROUPS, dtype=_F32)
    ns = N_SSM_GROUPS * SSM_STATE
    bd = lambda a: jnp.einsum('gpi,gh->gihp', a, eye).reshape(D_SSM, ns)
    cd = lambda a: jnp.einsum('gip,gh->gphi', a, eye).reshape(ns, D_SSM)
    return dict(toep=toep.astype(_BF16), w_in=w_in.astype(_BF16), v_out=v_out.astype(_BF16), a_chunk=a_chunk,
                bd_re=bd(bbr).astype(_BF16), bd_im=bd(bbi).astype(_BF16),
                cd_re=cd(cr).astype(_BF16), cd_im=cd(ci).astype(_BF16),
                lam_re=lbr.reshape(1, ns), lam_im=lbi.reshape(1, ns))


def _ssm_prompt_kernel(u_ref, toep_ref, w_ref, v_ref, a_ref, y_ref, s_ref, x_scr, sin_scr, *, n_chunks):
    n = SSM_CHUNK
    lhs = jnp.concatenate([u_ref[0, pl.ds(tau, n_chunks, stride=n), :] for tau in range(n)],
                          axis=1).astype(_BF16)
    x_scr[...] = jnp.dot(lhs, w_ref[0], preferred_element_type=_F32)
    half = x_scr.shape[1] // 2
    a_re = a_ref[0, 0:1, :]
    a_im = a_ref[0, 1:2, :]

    def step(c, carry):
        s_re, s_im = carry
        row = pl.ds(c, 1)
        sin_scr[row, :half] = s_re
        sin_scr[row, half:] = s_im
        x = x_scr[row, :]
        return (a_re * s_re - a_im * s_im + x[:, :half],
                a_re * s_im + a_im * s_re + x[:, half:])

    zero = jnp.zeros((1, half), _F32)
    s_re, s_im = lax.fori_loop(0, n_chunks, step, (zero, zero))
    s_ref[0, 0] = jnp.concatenate([s_re, s_im], axis=1)
    sin_b = sin_scr[...].astype(_BF16)
    for t2 in range(n // 2):
        k_hi = (t2 + 1) * 2 * LANES
        cols = slice(t2 * 2 * LANES, k_hi)
        y = (jnp.dot(lhs[:, :k_hi], toep_ref[0, :k_hi, cols], preferred_element_type=_F32)
             + jnp.dot(sin_b, v_ref[0, :, cols], preferred_element_type=_F32))
        for d in range(2):
            y_ref[0, pl.ds(2 * t2 + d, n_chunks, stride=n), :] = y[:, d * LANES:(d + 1) * LANES]


def _ssm_prompt(u, tabs):
    bsz, t, _ = u.shape
    nch = t // SSM_CHUNK
    nb = D_SSM // LANES
    sw = 2 * SSM_LANE_GROUPS * SSM_STATE
    table = lambda a: pl.BlockSpec((1,) + a.shape[1:], lambda m, b: (m, 0, 0), pipeline_mode=pl.Buffered(1))
    tok = pl.BlockSpec((1, t, LANES), lambda m, b: (b, 0, m))
    y, s = pl.pallas_call(
        functools.partial(_ssm_prompt_kernel, n_chunks=nch),
        grid=(nb, bsz),
        in_specs=[tok, table(tabs['toep']), table(tabs['w_in']), table(tabs['v_out']),
                  pl.BlockSpec((1, 2, sw // 2), lambda m, b: (m, 0, 0))],
        out_specs=[tok, pl.BlockSpec((1, 1, 1, sw), lambda m, b: (b, m, 0, 0))],
        out_shape=[jax.ShapeDtypeStruct((bsz, t, D_SSM), _F32),
                   jax.ShapeDtypeStruct((bsz, nb, 1, sw), _F32)],
        scratch_shapes=[pltpu.VMEM((nch, sw), _F32), pltpu.VMEM((nch, sw), _F32)],
        compiler_params=_params("parallel", "parallel"),
        name="ssm_prompt",
    )(u, tabs['toep'], tabs['w_in'], tabs['v_out'], tabs['a_chunk'])
    state = lambda a: a.reshape(bsz, N_SSM_GROUPS, SSM_STATE)
    return y, state(s[:, :, 0, :sw // 2]), state(s[:, :, 0, sw // 2:])


def _ssm_step_kernel(u_ref, s0re_ref, s0im_ref, lre_ref, lim_ref, bdre_ref, bdim_ref, cdre_ref, cdim_ref,
                     y_ref, sre_ref, sim_ref):
    u = u_ref[...]
    s_re, s_im = s0re_ref[...], s0im_ref[...]
    l_re, l_im = lre_ref[...], lim_ref[...]
    n_re = l_re * s_re - l_im * s_im + _bdot(u, bdre_ref[...])
    n_im = l_re * s_im + l_im * s_re + _bdot(u, bdim_ref[...])
    sre_ref[...] = n_re
    sim_ref[...] = n_im
    y_ref[...] = _bdot(n_re, cdre_ref[...]) - _bdot(n_im, cdim_ref[...])


def _ssm_step(u, s0_re, s0_im, tabs):
    n = u.shape[0]
    ns = N_SSM_GROUPS * SSM_STATE
    return pl.pallas_call(
        _ssm_step_kernel,
        out_shape=[jax.ShapeDtypeStruct((n, D_SSM), _F32),
                   jax.ShapeDtypeStruct((n, ns), _F32),
                   jax.ShapeDtypeStruct((n, ns), _F32)],
        compiler_params=pltpu.CompilerParams(vmem_limit_bytes=VMEM_LIMIT),
        name="ssm_step",
    )(u, s0_re, s0_im, tabs['lam_re'], tabs['lam_im'], tabs['bd_re'], tabs['bd_im'], tabs['cd_re'], tabs['cd_im'])


def _compress_tables(pe_k, w1_k, w2_k, pe_v, w1_v, w2_v):
    zeros = jnp.zeros((CMP_STRIDE, HEAD_DIM, CMP_HIDDEN), _F32)
    cols = []
    for half in range(2):
        for slot in range(KV_SLOTS):
            w1 = (w1_k if slot < N_KV_HEADS else w1_v)[half * CMP_STRIDE:(half + 1) * CMP_STRIDE]
            blk = jnp.stack([w1 if s == slot else zeros for s in range(KV_SLOTS)], axis=1)
            cols.append(blk.reshape(CMP_STRIDE * KV_ROW, CMP_HIDDEN))
    w_all = jnp.concatenate(cols, axis=1).astype(_BF16)
    z2 = jnp.zeros((CMP_HIDDEN, HEAD_DIM), _F32)
    w2 = jnp.concatenate(
        [jnp.concatenate([(w2_k if s < N_KV_HEADS else w2_v) if s == slot else z2 for s in range(KV_SLOTS)], axis=1)
         for slot in range(KV_SLOTS)], axis=0).astype(_BF16)
    pe = jnp.stack([pe_k.reshape(-1), pe_v.reshape(-1)], axis=0)
    pe = jnp.concatenate([pe, jnp.zeros((SUBLANES - 2, pe.shape[1]), _F32)], axis=0)
    w1f = jnp.stack([w1_k.reshape(-1, CMP_HIDDEN), w1_v.reshape(-1, CMP_HIDDEN)], axis=0)
    halves = lambda w1: jnp.concatenate([w1[:CMP_STRIDE].reshape(-1, CMP_HIDDEN),
                                         w1[CMP_STRIDE:].reshape(-1, CMP_HIDDEN)], axis=1)
    w_slot = jnp.stack([halves(w1_k if s < N_KV_HEADS else w1_v) for s in range(KV_SLOTS)], axis=0).astype(_BF16)
    return dict(w_all=w_all, w_slot=w_slot, tail=(w2, w2.T, pe, w1f))


def _compress_tail(h, w2_ref, w2t_ref, pe_ref, w1f_ref, n_rows):
    hw = KV_SLOTS * CMP_HIDDEN
    pe = pe_ref[...]
    bias_k = jnp.dot(pe, w1f_ref[0], precision=_HIGHEST, preferred_element_type=_F32)[0:1]
    bias_v = jnp.dot(pe, w1f_ref[1], precision=_HIGHEST, preferred_element_type=_F32)[1:2]
    bias = jnp.concatenate([bias_k, bias_k, bias_v, bias_v], axis=1)
    nxt = pltpu.roll(h[:, hw:], n_rows - 1, 0)
    hid = _gelu_tanh(h[:, :hw] + nxt + bias)
    row = lax.broadcasted_iota(jnp.int32, hid.shape, 0)
    hid = jnp.where(row < n_rows - 1, hid, 0.0).astype(_BF16)
    return (jnp.dot(hid, w2_ref[...], preferred_element_type=_F32),
            lax.dot_general(w2t_ref[...], hid, _NT, preferred_element_type=_F32))


def _store_compressed(out, out_t, o_ref, ot_ref):
    for s in range(KV_SLOTS):
        o_ref[0, s] = out[:, s * HEAD_DIM:(s + 1) * HEAD_DIM]
        ot_ref[0, s] = out_t[s * HEAD_DIM:(s + 1) * HEAD_DIM, :]


def _split3_keys(k):
    hi = k.astype(_BF16)
    lo = (k - hi.astype(_F32)).astype(_BF16)
    return jnp.concatenate([hi, lo, hi, jnp.zeros_like(hi)], axis=1)


def _split3_queries(q_t):
    hi = q_t.astype(_BF16)
    lo = (q_t - hi.astype(_F32)).astype(_BF16)
    return jnp.concatenate([hi, hi, lo, jnp.zeros_like(hi)], axis=0)


def _compress_prompt_kernel(x_ref, wall_ref, w2_ref, w2t_ref, pe_ref, w1f_ref, k3_ref, ot_ref, *, n_rows):
    h = jnp.dot(x_ref[0].astype(_BF16), wall_ref[...], preferred_element_type=_F32)
    out, out_t = _compress_tail(h, w2_ref, w2t_ref, pe_ref, w1f_ref, n_rows)
    for g in range(N_KV_HEADS):
        k3_ref[0, g] = _split3_keys(out[:, g * HEAD_DIM:(g + 1) * HEAD_DIM])
    for s in range(KV_SLOTS):
        ot_ref[0, s] = out_t[s * HEAD_DIM:(s + 1) * HEAD_DIM, :]


def _compress_out(nseq, nc, index_map, index_map_t):
    specs = [pl.BlockSpec((1, KV_SLOTS, nc, HEAD_DIM), index_map),
             pl.BlockSpec((1, KV_SLOTS, HEAD_DIM, nc), index_map_t)]
    shapes = [jax.ShapeDtypeStruct((nseq, KV_SLOTS, nc, HEAD_DIM), _F32),
              jax.ShapeDtypeStruct((nseq, KV_SLOTS, HEAD_DIM, nc), _F32)]
    return specs, shapes


def _compress_prompt(kv_cmp, ctabs):
    bsz, t, _ = kv_cmp.shape
    nc = t // CMP_STRIDE
    const = lambda a: pl.BlockSpec(a.shape, lambda b: (0,) * a.ndim)
    weights = (ctabs['w_all'],) + ctabs['tail']
    return pl.pallas_call(
        functools.partial(_compress_prompt_kernel, n_rows=nc),
        grid=(bsz,),
        in_specs=[pl.BlockSpec((1, nc, CMP_STRIDE * KV_ROW), lambda b: (b, 0, 0))] + [const(a) for a in weights],
        out_specs=[pl.BlockSpec((1, N_KV_HEADS, nc, KV_SLOTS * HEAD_DIM), lambda b: (b, 0, 0, 0)),
                   pl.BlockSpec((1, KV_SLOTS, HEAD_DIM, nc), lambda b: (b, 0, 0, 0))],
        out_shape=[jax.ShapeDtypeStruct((bsz, N_KV_HEADS, nc, KV_SLOTS * HEAD_DIM), _BF16),
                   jax.ShapeDtypeStruct((bsz, KV_SLOTS, HEAD_DIM, nc), _F32)],
        compiler_params=_params("parallel"),
        name="compress_prompt",
    )(kv_cmp.reshape(bsz, nc, CMP_STRIDE * KV_ROW), *weights)


def _page_copies(pt_ref, cache_hbm, buf, sem, seq, slot, n_pages):
    return [pltpu.make_async_copy(cache_hbm.at[pt_ref[seq, j]],
                                  buf.at[slot, :, pl.ds(j * PAGE_SIZE, PAGE_SIZE)],
                                  sem.at[slot]) for j in range(n_pages)]


def _gather_pages(pt_ref, cache_hbm, buf, sem, n_pages):
    b = pl.program_id(0)
    slot = lax.rem(b, 2)

    @pl.when(b == 0)
    def _():
        for cp in _page_copies(pt_ref, cache_hbm, buf, sem, 0, 0, n_pages):
            cp.start()

    @pl.when(b + 1 < pl.num_programs(0))
    def _():
        for cp in _page_copies(pt_ref, cache_hbm, buf, sem, b + 1, 1 - slot, n_pages):
            cp.start()

    for cp in _page_copies(pt_ref, cache_hbm, buf, sem, b, slot, n_pages):
        cp.wait()
    return slot


def _chunk_permutation():
    span = 2 * PAGE_SIZE
    r = jnp.arange(span)
    src = (r % (span // CMP_STRIDE)) * CMP_STRIDE + r // (span // CMP_STRIDE)
    return (src[:, None] == jnp.arange(span)[None, :]).astype(_BF16)


def _compress_paged_kernel(pt_ref, cache_hbm, perm_ref, wslot_ref, w2_ref, w2t_ref, pe_ref, w1f_ref,
                           o_ref, ot_ref, buf, sem, xr, *, n_pages, n_rows):
    slot = _gather_pages(pt_ref, cache_hbm, buf, sem, n_pages)
    span = 2 * PAGE_SIZE
    cps = span // CMP_STRIDE
    low = lax.broadcasted_iota(jnp.int32, (cps, LANES), 1) < HEAD_DIM

    def relayout(i, carry):
        xt = buf[slot, :, pl.ds(pl.multiple_of(i * span, span), span)].astype(_BF16)
        rows = lax.dot_general(perm_ref[...], xt, _NT, preferred_element_type=_F32)
        chunk_rows = pl.ds(pl.multiple_of(i * cps, cps), cps)
        for a in range(CMP_STRIDE // 2):
            for s in range(KV_SLOTS):
                tile = slice((s // 2) * LANES, (s // 2 + 1) * LANES)
                even = rows[2 * a * cps:(2 * a + 1) * cps, tile]
                odd = rows[(2 * a + 1) * cps:(2 * a + 2) * cps, tile]
                if s % 2 == 0:
                    pair = jnp.where(low, even, pltpu.roll(odd, HEAD_DIM, 1))
                else:
                    pair = jnp.where(low, pltpu.roll(even, HEAD_DIM, 1), odd)
                xr[s, chunk_rows, a * LANES:(a + 1) * LANES] = pair.astype(_BF16)
        return carry

    lax.fori_loop(0, n_pages // 2, relayout, 0, unroll=math.gcd(RELAYOUT_UNROLL, n_pages // 2))
    hs = [jnp.dot(xr[s], wslot_ref[s], preferred_element_type=_F32) for s in range(KV_SLOTS)]
    h = jnp.concatenate([v[:, :CMP_HIDDEN] for v in hs] + [v[:, CMP_HIDDEN:] for v in hs], axis=1)
    out, out_t = _compress_tail(h, w2_ref, w2t_ref, pe_ref, w1f_ref, n_rows)
    _store_compressed(out, out_t, o_ref, ot_ref)


def _compress_paged(cache_t, page_table, ctabs):
    nseq, n_pages = page_table.shape
    past = n_pages * PAGE_SIZE
    nc = past // CMP_STRIDE
    perm = _chunk_permutation()
    const = lambda a: pl.BlockSpec(a.shape, lambda b, pt: (0,) * a.ndim)
    out_specs, out_shape = _compress_out(nseq, nc, lambda b, pt: (b, 0, 0, 0), lambda b, pt: (b, 0, 0, 0))
    weights = (ctabs['w_slot'],) + ctabs['tail']
    grid_spec = pltpu.PrefetchScalarGridSpec(
        num_scalar_prefetch=1,
        grid=(nseq,),
        in_specs=[pl.BlockSpec(memory_space=pl.ANY), const(perm)] + [const(a) for a in weights],
        out_specs=out_specs,
        scratch_shapes=[pltpu.VMEM((2, KV_ROW, past), _F32), pltpu.SemaphoreType.DMA((2,)),
                        pltpu.VMEM((KV_SLOTS, nc, CMP_STRIDE * HEAD_DIM), _BF16)],
    )
    return pl.pallas_call(
        functools.partial(_compress_paged_kernel, n_pages=n_pages, n_rows=nc),
        grid_spec=grid_spec,
        out_shape=out_shape,
        compiler_params=_params("arbitrary"),
        name="compress_paged",
    )(page_table, cache_t, perm, *weights)


def _importance_map(n_cmp_rows, n_blocks_rows, n_cmp):
    ratio = SLC_BLOCK // CMP_STRIDE
    j = jnp.arange(n_blocks_rows)[:, None]
    n = jnp.arange(n_cmp_rows)[None, :]
    off = n - ratio * j
    w = jnp.where((off == -1) | (off == ratio - 1), 1.0, jnp.where((off >= 0) & (off < ratio - 1), 2.0, 0.0))
    return jnp.where(n < n_cmp, w, 0.0).astype(_F32)


def _softmax(s, ok, axis):
    s = jnp.where(ok, s, NEG_INF)
    m = jnp.max(s, axis=axis, keepdims=True)
    m = jnp.where(m == NEG_INF, 0.0, m)
    e = jnp.exp(s - m)
    den = jnp.sum(e, axis=axis, keepdims=True)
    return e / jnp.where(den > 0, den, 1.0)


def _nsa_prompt_kernel(q_ref, gt_ref, kc_ref, vct_ref, ksn_ref, vst_ref, kwn_ref, vwt_ref, mt_ref, o_ref,
                       bias_ref, s_scr, *, tq, kc_len, n_cmp, n_blk):
    g = pl.program_id(1)
    t0 = pl.program_id(2) * tq
    cols = GROUP_SIZE * tq
    q_t = q_ref[0]
    q4 = jnp.concatenate([q_t[r * HEAD_DIM:(r + 1) * HEAD_DIM, :] for r in range(GROUP_SIZE)], axis=1)
    q4b = (q4 * LOG2_E).astype(_BF16)
    feat = lax.broadcasted_iota(jnp.int32, (KV_ROW, cols), 0)
    qp = jnp.where(lax.shift_right_logical(feat, int(math.log2(HEAD_DIM))) == g,
                   jnp.concatenate([q4b] * KV_SLOTS, axis=0), jnp.zeros((), _BF16))
    lane = lax.broadcasted_iota(jnp.int32, (1, cols), 1)
    qpos = t0 + (lane & (tq - 1))

    ncr = kc_ref.shape[2]
    s_c = jnp.dot(kc_ref[0, 0], _split3_queries(q4), preferred_element_type=_F32)
    n_idx = lax.broadcasted_iota(jnp.int32, (ncr, 1), 0)
    cmp_end = jnp.where(n_idx < n_cmp, n_idx * CMP_STRIDE + (CMP_BLOCK - 1), jnp.int32(2 ** 30))
    p_c = _softmax(s_c, cmp_end <= qpos, 0)
    o_c = _bdot(vct_ref[0, 0], p_c)

    imp = p_c[:, 0:tq]
    for r in range(1, GROUP_SIZE):
        imp = imp + p_c[:, r * tq:(r + 1) * tq]
    imp_hi = imp.astype(_BF16)
    imp_r = imp - imp_hi.astype(_F32)
    imp_mid = imp_r.astype(_BF16)
    imp_lo = (imp_r - imp_mid.astype(_F32)).astype(_BF16)
    imp_s = jnp.dot(mt_ref[...], jnp.concatenate([imp_hi, imp_mid, imp_lo], axis=0),
                    preferred_element_type=_F32)
    qrow = qpos[:, :tq]
    j_idx = lax.broadcasted_iota(jnp.int32, (n_blk, tq), 0)
    valid = j_idx * SLC_BLOCK <= qrow
    cur = lax.shift_right_arithmetic(qrow, int(math.log2(SLC_BLOCK)))
    forced = (j_idx == 0) | (valid & (j_idx > cur - N_LOCAL))
    score = jnp.where(forced, FORCED_SCORE, jnp.where(valid, imp_s, -1.0))
    rank = jnp.zeros((n_blk, tq), jnp.int32)
    for j in range(n_blk):
        sj = score[j:j + 1, :]
        rank = rank + jnp.where(j_idx > j, jnp.where(sj >= score, 1, 0), jnp.where(sj > score, 1, 0))
    bias = jnp.where(valid & (rank < N_SELECT), 0.0, NEG_INF)
    bias_ref[...] = jnp.concatenate([bias] * GROUP_SIZE, axis=1)

    bpc = kc_len // SLC_BLOCK

    def scores(c, causal):
        k0 = pl.multiple_of(c * kc_len, kc_len)
        s = jnp.dot(ksn_ref[0, pl.ds(k0, kc_len), :], qp, preferred_element_type=_F32)
        b8 = bias_ref[pl.ds(pl.multiple_of(c * bpc, bpc), bpc), :]
        s = jnp.concatenate([s[i * SLC_BLOCK:(i + 1) * SLC_BLOCK, :] + b8[i:i + 1, :] for i in range(bpc)], axis=0)
        if causal:
            kpos = k0 + lax.broadcasted_iota(jnp.int32, (kc_len, 1), 0)
            s = jnp.where(kpos <= qpos, s, NEG_INF)
        return s

    fold = lambda s: jnp.max(s.reshape(s.shape[0] // SUBLANES, SUBLANES, cols), axis=0)

    def weighted(s, m, vals_t):
        ones = jnp.ones((2 * SUBLANES, vals_t.shape[1]), _BF16)
        return jnp.dot(jnp.concatenate([vals_t, ones], axis=0), jnp.exp2(s - m).astype(_BF16),
                       preferred_element_type=_F32)

    def pass1(c, m_run):
        s = scores(c, False)
        s_scr[c] = s
        return jnp.maximum(m_run, fold(s))

    c_diag = t0 // kc_len
    m_run = lax.fori_loop(0, c_diag, pass1, jnp.full((SUBLANES, cols), NEG_INF, _F32))
    s_d = scores(c_diag, True)
    m_s = jnp.max(jnp.maximum(m_run, fold(s_d)), axis=0, keepdims=True)

    def pass2(c, acc):
        k0 = pl.multiple_of(c * kc_len, kc_len)
        return acc + weighted(s_scr[c], m_s, vst_ref[0, :, pl.ds(k0, kc_len)])

    k_d = pl.multiple_of(c_diag * kc_len, kc_len)
    acc_s = lax.fori_loop(0, c_diag, pass2, weighted(s_d, m_s, vst_ref[0, :, pl.ds(k_d, kc_len)]))
    o_s = acc_s[:HEAD_DIM] / acc_s[HEAD_DIM:HEAD_DIM + 1]

    wk = tq + WINDOW
    w0 = pl.multiple_of(jnp.maximum(t0 - WINDOW, 0), tq)
    dpos = qpos - (w0 + lax.broadcasted_iota(jnp.int32, (wk, 1), 0))
    s_w = jnp.where(jnp.where(dpos >= 0, dpos, WINDOW) < WINDOW,
                    jnp.dot(kwn_ref[0, pl.ds(w0, wk), :], qp, preferred_element_type=_F32), NEG_INF)
    acc_w = weighted(s_w, jnp.max(s_w, axis=0, keepdims=True), vwt_ref[0, :, pl.ds(w0, wk)])
    o_w = acc_w[:HEAD_DIM] / acc_w[HEAD_DIM:HEAD_DIM + 1]

    outs = []
    for r in range(GROUP_SIZE):
        sl = slice(r * tq, (r + 1) * tq)
        gate = lambda branch: gt_ref[0, pl.ds(branch * N_HEADS + g * GROUP_SIZE + r, 1), :]
        outs.append(gate(0) * o_c[:, sl] + gate(1) * o_s[:, sl] + gate(2) * o_w[:, sl])
    o_ref[0] = jnp.concatenate(outs, axis=0).T


def _nsa_prompt(q_t, gates_t, kcmp3, kcmp_t, ks_n, ks_tb, kw_n, kw_tb, tq, kc_len):
    bsz, _, t = q_t.shape
    ncr = kcmp3.shape[2]
    n_blk = t // SLC_BLOCK
    mt = jnp.tile(_importance_map(ncr, n_blk, ncr - 1), (1, 3)).astype(_BF16)
    rows_n = pl.BlockSpec((1, t, KV_ROW), lambda b, g, i: (b, 0, 0))
    vals_t = pl.BlockSpec((1, HEAD_DIM, t), lambda b, g, i: (b, N_KV_HEADS + g, 0))
    qw = GROUP_SIZE * HEAD_DIM
    return pl.pallas_call(
        functools.partial(_nsa_prompt_kernel, tq=tq, kc_len=kc_len, n_cmp=ncr - 1, n_blk=n_blk),
        grid=(bsz, N_KV_HEADS, t // tq),
        in_specs=[pl.BlockSpec((1, qw, tq), lambda b, g, i: (b, g, i)),
                  pl.BlockSpec((1, GATE_ROWS, tq), lambda b, g, i: (b, 0, i)),
                  pl.BlockSpec((1, 1, ncr, KV_SLOTS * HEAD_DIM), lambda b, g, i: (b, g, 0, 0)),
                  pl.BlockSpec((1, 1, HEAD_DIM, ncr), lambda b, g, i: (b, N_KV_HEADS + g, 0, 0)),
                  rows_n, vals_t, rows_n, vals_t,
                  pl.BlockSpec(mt.shape, lambda b, g, i: (0, 0))],
        out_specs=pl.BlockSpec((1, tq, qw), lambda b, g, i: (b, i, g)),
        out_shape=jax.ShapeDtypeStruct((bsz, t, D_ATTN), _F32),
        scratch_shapes=[pltpu.VMEM((n_blk, GROUP_SIZE * tq), _F32),
                        pltpu.VMEM((t // kc_len, kc_len, GROUP_SIZE * tq), _F32)],
        compiler_params=_params("parallel", "parallel", "parallel"),
        name="nsa_prompt",
    )(q_t, gates_t, kcmp3, kcmp_t, ks_n, ks_tb, kw_n, kw_tb, mt)


def _nsa_sample_kernel(pt_ref, q_ref, gt_ref, kcmp_ref, kcmpt_ref, slc_hbm, ksn_ref, win_ref, kwt_ref,
                       m_ref, e_ref, o_ref, nw_ref, buf, sem, xb,
                       *, n_pages, past, n_cmp, blk_pad, n_buf):
    b = pl.program_id(0)
    slot = _gather_pages(pt_ref, slc_hbm, buf, sem, n_pages)
    xb[...] = buf[slot].astype(_BF16)
    q = q_ref[0]
    gt = gt_ref[0]
    ks_new = ksn_ref[0]
    nseq = kwt_ref.shape[1]
    seq_lane = lax.broadcasted_iota(jnp.int32, (KV_ROW, nseq), 1)
    kw_col = jnp.sum(jnp.where(seq_lane == b, kwt_ref[...], 0.0), axis=1, keepdims=True)
    win = win_ref[0]
    pos = lax.broadcasted_iota(jnp.int32, win.shape, 1)
    band = jnp.where(pos == n_buf - 1, kw_col, pltpu.roll(win, n_buf - 1, 1))
    nw_ref[0] = band
    band_b = band.astype(_BF16)
    ncr = kcmp_ref.shape[2]
    lane = lax.broadcasted_iota(jnp.int32, (SUBLANES, LANES), 1)
    hrow = lax.broadcasted_iota(jnp.int32, (SUBLANES, LANES), 0)
    heads = []
    for g in range(N_KV_HEADS):
        q8 = jnp.concatenate([q[:, (g * GROUP_SIZE + r) * HEAD_DIM:(g * GROUP_SIZE + r + 1) * HEAD_DIM]
                              for r in range(GROUP_SIZE)]
                             + [jnp.zeros((SUBLANES - GROUP_SIZE, HEAD_DIM), _F32)], axis=0)
        q8b = q8.astype(_BF16)
        k_rows = slice(g * HEAD_DIM, (g + 1) * HEAD_DIM)
        v_rows = slice((N_KV_HEADS + g) * HEAD_DIM, (N_KV_HEADS + g + 1) * HEAD_DIM)

        s_c = lax.dot_general(q8, kcmp_ref[0, g], _NT, precision=_HIGHEST, preferred_element_type=_F32)
        n_idx = lax.broadcasted_iota(jnp.int32, (SUBLANES, ncr), 1)
        cmp_end = jnp.where(n_idx < n_cmp, n_idx * CMP_STRIDE + (CMP_BLOCK - 1), jnp.int32(2 ** 30))
        p_c = _softmax(s_c, cmp_end <= past, 1)
        o_c = _bdot_nt(p_c, kcmpt_ref[0, N_KV_HEADS + g])

        imp = jnp.sum(p_c[:GROUP_SIZE], axis=0, keepdims=True)
        imp_s = jnp.dot(jnp.broadcast_to(imp, (SUBLANES, ncr)), m_ref[...], precision=_HIGHEST,
                        preferred_element_type=_F32)[0:1]
        j_idx = lax.broadcasted_iota(jnp.int32, (1, blk_pad), 1)
        valid = j_idx * SLC_BLOCK <= past
        forced = (j_idx == 0) | (valid & (j_idx > past // SLC_BLOCK - N_LOCAL))
        score = jnp.where(forced, FORCED_SCORE, jnp.where(valid, imp_s, -1.0))
        s_k = jnp.broadcast_to(score, (blk_pad, blk_pad))
        s_j = s_k.T
        jj = lax.broadcasted_iota(jnp.int32, (blk_pad, blk_pad), 0)
        kk = lax.broadcasted_iota(jnp.int32, (blk_pad, blk_pad), 1)
        beats = jnp.where(kk > jj, jnp.where(s_j >= s_k, 1.0, 0.0), jnp.where(s_j > s_k, 1.0, 0.0))
        rank = jnp.sum(beats, axis=0, keepdims=True)
        sel = jnp.where(valid & (rank < N_SELECT), 1.0, 0.0)
        picked = jnp.dot(jnp.broadcast_to(sel, (SUBLANES, blk_pad)).astype(_BF16), e_ref[...],
                         preferred_element_type=_F32) > 0.5

        s_s = jnp.where(picked, jnp.dot(q8b, xb[k_rows, :], preferred_element_type=_F32), NEG_INF)
        s_n = jnp.sum(q8 * ks_new[:, k_rows], axis=-1, keepdims=True)
        m = jnp.maximum(jnp.max(s_s, axis=-1, keepdims=True), s_n)
        e_p, e_n = jnp.exp(s_s - m), jnp.exp(s_n - m)
        num = _bdot_nt(e_p, xb[v_rows, :])
        o_s = (num + e_n * ks_new[:, v_rows]) / (jnp.sum(e_p, axis=-1, keepdims=True) + e_n)

        s_w = jnp.dot(q8b, band_b[k_rows, :], preferred_element_type=_F32)
        i_idx = lax.broadcasted_iota(jnp.int32, (SUBLANES, n_buf), 1)
        p_w = _softmax(s_w, (n_buf - 1 - i_idx) < WINDOW, 1)
        o_w = _bdot_nt(p_w, band_b[v_rows, :])

        def gate(branch):
            tgt = branch * N_HEADS + g * GROUP_SIZE + hrow
            return jnp.sum(jnp.where(lane == tgt, jnp.broadcast_to(gt, (SUBLANES, LANES)), 0.0),
                           axis=-1, keepdims=True)

        o = gate(0) * o_c + gate(1) * o_s + gate(2) * o_w
        heads += [o[r:r + 1, :] for r in range(GROUP_SIZE)]
    o_ref[0] = jnp.concatenate(heads, axis=1)


def _expand_table(past, blk_pad):
    return (jnp.arange(past)[None, :] // SLC_BLOCK == jnp.arange(blk_pad)[:, None]).astype(_BF16)


def _nsa_sample(q, gates, kcmp, kcmp_t, cache_t, page_table, ks_new, win_t, kw_t):
    nseq, n_pages = page_table.shape
    past = n_pages * PAGE_SIZE
    ncr = kcmp.shape[2]
    n_buf = win_t.shape[2]
    n_blk = past // SLC_BLOCK + 1
    blk_pad = -(-n_blk // LANES) * LANES
    m_tab = _importance_map(ncr, blk_pad, ncr - 1).T
    e = _expand_table(past, blk_pad)
    row = lambda w: pl.BlockSpec((1, 1, w), lambda b, pt: (b, 0, 0))
    const = lambda a: pl.BlockSpec(a.shape, lambda b, pt: (0,) * a.ndim)
    win_spec = pl.BlockSpec((1, KV_ROW, n_buf), lambda b, pt: (b, 0, 0))
    grid_spec = pltpu.PrefetchScalarGridSpec(
        num_scalar_prefetch=1,
        grid=(nseq,),
        in_specs=[row(D_ATTN), row(LANES),
                  pl.BlockSpec((1, KV_SLOTS, ncr, HEAD_DIM), lambda b, pt: (b, 0, 0, 0)),
                  pl.BlockSpec((1, KV_SLOTS, HEAD_DIM, ncr), lambda b, pt: (b, 0, 0, 0)),
                  pl.BlockSpec(memory_space=pl.ANY),
                  row(KV_ROW), win_spec, const(kw_t), const(m_tab), const(e)],
        out_specs=[row(D_ATTN), win_spec],
        scratch_shapes=[pltpu.VMEM((2, KV_ROW, past), _F32), pltpu.SemaphoreType.DMA((2,)),
                        pltpu.VMEM((KV_ROW, past), _BF16)],
    )
    return pl.pallas_call(
        functools.partial(_nsa_sample_kernel, n_pages=n_pages, past=past, n_cmp=ncr - 1,
                          blk_pad=blk_pad, n_buf=n_buf),
        grid_spec=grid_spec,
        out_shape=[jax.ShapeDtypeStruct((nseq, 1, D_ATTN), _F32),
                   jax.ShapeDtypeStruct((nseq, KV_ROW, n_buf), _F32)],
        compiler_params=_params("arbitrary"),
        name="nsa_sample",
    )(page_table, q, gates, kcmp, kcmp_t, cache_t, ks_new, win_t, kw_t, m_tab, e)


def _post_kernel(x_ref, y_ref, u_ref, oa_ref, mod_ref, d_ref, wglu_ref, gs_ref, ga_ref, wout_ref, gm_ref,
                 wup_ref, wdn_ref, gf_ref, o_ref, *, ff_chunk):
    z = _gelu_tanh(y_ref[0] + d_ref[...] * u_ref[0])
    o_ssm = z * _sigmoid(jnp.dot(z.astype(_BF16), wglu_ref[...], preferred_element_type=_F32))
    mix = (jnp.dot(_rms(o_ssm, gs_ref[...]).astype(_BF16), wout_ref[:D_SSM, :], preferred_element_type=_F32)
           + jnp.dot(_rms(oa_ref[0], ga_ref[...]).astype(_BF16), wout_ref[D_SSM:, :], preferred_element_type=_F32))
    x1 = x_ref[0] + mod_ref[0, 2] * mix
    h2 = (_rms(x1, gm_ref[...]) * (1.0 + mod_ref[0, 4]) + mod_ref[0, 3]).astype(_BF16)
    acc = jnp.zeros_like(x1)
    for k in range(D_FF // ff_chunk):
        cols = slice(k * ff_chunk, (k + 1) * ff_chunk)
        hid = jnp.maximum(jnp.dot(h2, wup_ref[:, cols], preferred_element_type=_F32), 0.0)
        acc = acc + jnp.dot((hid * hid).astype(_BF16), wdn_ref[cols, :], preferred_element_type=_F32)
    o_ref[0] = _rms(x1 + mod_ref[0, 5] * acc, gf_ref[...])


def _post(x, y_ssm, u, o_attn, mod, lp, norm_final, tm):
    bsz, t, _ = x.shape
    r = mod.shape[2]
    rb = 1 if r == 1 else tm
    mod_map = (lambda b, i: (b, 0, 0, 0)) if r == 1 else (lambda b, i: (b, 0, i, 0))
    tok = lambda w: pl.BlockSpec((1, tm, w), lambda b, i: (b, i, 0))
    const = lambda shape: pl.BlockSpec(shape, lambda b, i: (0,) * len(shape), pipeline_mode=pl.Buffered(1))
    vec = lambda v: v.reshape(1, -1).astype(_F32)
    weights = [vec(lp['ssm_d']), lp['ssm_w_glu'].astype(_BF16), vec(lp['norm_out_ssm']), vec(lp['norm_out_attn']),
               lp['w_out'].astype(_BF16), vec(lp['norm_mlp']), lp['w_up'].astype(_BF16), lp['w_down'].astype(_BF16),
               vec(norm_final)]
    return pl.pallas_call(
        functools.partial(_post_kernel, ff_chunk=D_MODEL),
        grid=(bsz, t // tm),
        in_specs=[tok(D_MODEL), tok(D_SSM), tok(D_SSM), tok(D_ATTN),
                  pl.BlockSpec((1, 6, rb, D_MODEL), mod_map)] + [const(w.shape) for w in weights],
        out_specs=tok(D_MODEL),
        out_shape=jax.ShapeDtypeStruct((bsz, t, D_MODEL), _F32),
        compiler_params=_params("parallel", "parallel"),
        name="post_mlp",
    )(x, y_ssm, u, o_attn, mod, *weights)


def _feature_major(a):
    lead = a.shape[:-4]
    n = len(lead)
    return a.transpose(*range(n), n + 1, n + 2, n + 3, n).reshape(*lead, KV_ROW, a.shape[-4])


def _kv_output(a_t):
    n, _, t = a_t.shape
    return a_t.reshape(n, 2, N_KV_HEADS, HEAD_DIM, t).transpose(0, 4, 1, 2, 3)[None]


def kernel(x_prompt, x_sample, cache_cmp, cache_slc, state_win, state_ssm_re, state_ssm_im, page_table,
           c_prompt, c_sample, w_ada, b_ada, norm_attn, w_in, ssm_lambda_re, ssm_lambda_im, ssm_log_dt,
           ssm_b_re, ssm_b_im, ssm_c_re, ssm_c_im, ssm_d, ssm_w_glu, cmp_pe_k, cmp_w1_k, cmp_w2_k,
           cmp_pe_v, cmp_w1_v, cmp_w2_v, norm_out_ssm, norm_out_attn, w_out, norm_mlp, w_up, w_down,
           norm_final):
    depth = w_ada.shape[0]
    assert depth == 1, "single-layer trunk"
    l = 0
    bsz, t, _ = x_prompt.shape
    nseq = x_sample.shape[0]
    assert x_sample.shape[1] == 1
    tm = min(512, t)
    tq = min(128, t)
    kc_len = min(512, t)

    lp = dict(ssm_d=ssm_d[l], ssm_w_glu=ssm_w_glu[l], norm_out_ssm=norm_out_ssm[l],
              norm_out_attn=norm_out_attn[l], w_out=w_out[l], norm_mlp=norm_mlp[l], w_up=w_up[l], w_down=w_down[l])
    w_full = jnp.pad(w_in[l], ((0, 0), (0, D_IN_PAD - D_IN))).astype(_BF16)
    w_rows = jnp.concatenate([w_full[:, _Q0:_G0 + 3 * N_HEADS].T,
                              jnp.zeros((GATE_ROWS - 3 * N_HEADS, D_MODEL), _BF16)], axis=0)
    tq0, tkc0, tks0, tkw0, tg0 = 0, D_ATTN, D_ATTN + KV_ROW, D_ATTN + 2 * KV_ROW, D_ATTN + 3 * KV_ROW
    tabs = _ssm_tables(ssm_lambda_re[l], ssm_lambda_im[l], ssm_log_dt[l], ssm_b_re[l], ssm_b_im[l],
                       ssm_c_re[l], ssm_c_im[l])
    ctabs = _compress_tables(cmp_pe_k[l], cmp_w1_k[l], cmp_w2_k[l], cmp_pe_v[l], cmp_w1_v[l], cmp_w2_v[l])

    mod = _ada_mod(jnp.concatenate([c_prompt, c_sample], axis=0), w_ada[l], b_ada[l])
    mod_p = mod[:bsz].reshape(bsz, 6, 1, D_MODEL)
    mod_s = mod[bsz:].reshape(nseq, 6, D_MODEL).transpose(1, 0, 2)[None]

    w_tok = jnp.concatenate([w_full[:, :D_SSM], w_full[:, _KC0:_G0]], axis=1)
    outs_p = [('n', 0, D_SSM, None), ('n', D_SSM, D_SSM + KV_ROW, None),
              ('n', D_SSM + KV_ROW, D_SSM + 2 * KV_ROW, None), ('n', D_SSM + 2 * KV_ROW, D_SSM + 3 * KV_ROW, None),
              ('t', tq0, tkc0, 'scale'), ('t', tkc0, tks0, None), ('t', tks0, tkw0, None), ('t', tkw0, tg0, None),
              ('t', tks0, tkw0, None), ('t', tkw0, tg0, None), ('t', tg0, tg0 + GATE_ROWS, 'sigmoid')]
    dt_p = [_F32, _F32, _BF16, _BF16, _F32, _F32, _F32, _F32, _BF16, _BF16, _F32]
    (u, kc_n, ks_n, kw_n, q_t, kc_t, ks_t, kw_t, ks_tb, kw_tb, gates_t) = _in_proj(
        x_prompt, mod_p, norm_attn[l], w_tok, w_rows, tm, outs_p, dt_p)
    y_ssm, sp_re, sp_im = _ssm_prompt(u, tabs)
    kcmp, kcmp_t = _compress_prompt(kc_n, ctabs)
    o_attn = _nsa_prompt(q_t, gates_t, kcmp, kcmp_t, ks_n, ks_tb, kw_n, kw_tb, tq, kc_len)
    y_prompt = _post(x_prompt, y_ssm, u, o_attn, mod_p, lp, norm_final, tm)
    w_keep = min(WINDOW, t)

    xs = x_sample.reshape(1, nseq, D_MODEL)
    outs_s = [('n', 0, D_SSM, None), ('n', _Q0, _KC0, 'scale'), ('n', _KS0, _KW0, None), ('n', _G0, D_IN_PAD, 'sigmoid'),
              ('t', tkc0, tks0, None), ('t', tks0, tkw0, None), ('t', tkw0, tg0, None)]
    u_s, q_s, ks_row, gates_s, kc_ts, ks_ts, kw_ts = _in_proj(
        xs, mod_s, norm_attn[l], w_full, w_rows, nseq, outs_s, [_F32] * len(outs_s))
    ns = N_SSM_GROUPS * SSM_STATE
    y_s, ss_re, ss_im = _ssm_step(u_s[0], state_ssm_re[l].reshape(nseq, ns), state_ssm_im[l].reshape(nseq, ns), tabs)
    kcmp_s, kcmp_ts = _compress_paged(_feature_major(cache_cmp[l]), page_table, ctabs)
    o_attn_s, new_win_t = _nsa_sample(q_s.reshape(nseq, 1, D_ATTN), gates_s.reshape(nseq, 1, LANES), kcmp_s, kcmp_ts,
                                      _feature_major(cache_slc[l]), page_table, ks_row.reshape(nseq, 1, KV_ROW),
                                      _feature_major(state_win[l]), kw_ts[0])
    y_sample = _post(xs, y_s[None], u_s, o_attn_s.reshape(1, nseq, D_ATTN), mod_s, lp, norm_final, nseq)
    st_shape = (1, nseq, N_SSM_GROUPS, SSM_STATE)
    new_row = lambda a_t: _kv_output(a_t[0].T[:, :, None])

    return (y_prompt, y_sample.reshape(nseq, 1, D_MODEL),
            _kv_output(kc_t), _kv_output(ks_t), _kv_output(kw_t[:, :, t - w_keep:]),
            sp_re[None], sp_im[None],
            new_row(kc_ts), new_row(ks_ts), _kv_output(new_win_t),
            ss_re.reshape(st_shape), ss_im.reshape(st_shape))
```

```python
import functools
import math

import jax
import jax.numpy as jnp
from jax import lax
from jax.experimental import pallas as pl
from jax.experimental.pallas import tpu as pltpu

D_MODEL = 1024
D_SSM = D_MODEL // 2
SSM_GROUP = 16
N_SSM_GROUPS = D_SSM // SSM_GROUP
SSM_STATE = 64
HEAD_DIM = 64
D_ATTN = D_MODEL - D_SSM
N_HEADS = D_ATTN // HEAD_DIM
N_KV_HEADS = 2
GROUP_SIZE = N_HEADS // N_KV_HEADS
D_KV = N_KV_HEADS * HEAD_DIM
CMP_BLOCK = 32
CMP_STRIDE = 16
CMP_HIDDEN = 2 * HEAD_DIM
SLC_BLOCK = 64
N_SELECT = 16
N_LOCAL = 2
WINDOW = 512
D_FF = 4 * D_MODEL
D_IN = D_SSM + D_ATTN + 6 * D_KV + 3 * N_HEADS
EPS = 1e-6
PAGE_SIZE = 128

LANES = 128
SUBLANES = 8
D_IN_PAD = -(-D_IN // LANES) * LANES
KV_SLOTS = 2 * N_KV_HEADS
KV_ROW = KV_SLOTS * HEAD_DIM
GATE_ROWS = 32
SSM_CHUNK = 16
SSM_LANE_GROUPS = LANES // SSM_GROUP
VMEM_LIMIT = 56 * 1024 * 1024
FORCED_SCORE = 1e30
NEG_INF = float("-inf")
LOG2_E = math.log2(math.e)
RELAYOUT_UNROLL = 8

_Q0, _KC0, _KS0, _KW0, _G0 = D_SSM, D_MODEL, D_MODEL + KV_ROW, D_MODEL + 2 * KV_ROW, D_MODEL + 3 * KV_ROW

_BF16 = jnp.bfloat16
_F32 = jnp.float32
_NT = (((1,), (1,)), ((), ()))
_HIGHEST = lax.Precision.HIGHEST


def _params(*sem):
    return pltpu.CompilerParams(dimension_semantics=sem, vmem_limit_bytes=VMEM_LIMIT)


def _rms(x, g):
    return x * lax.rsqrt(jnp.mean(x * x, axis=-1, keepdims=True) + EPS) * g


def _gelu_tanh(x):
    return x * (0.5 * (1.0 + jnp.tanh(math.sqrt(2.0 / math.pi) * (x + 0.044715 * (x * x * x)))))


def _sigmoid(x):
    return 1.0 / (1.0 + jnp.exp(-x))


def _bdot(a, b):
    return jnp.dot(a.astype(_BF16), b.astype(_BF16), preferred_element_type=_F32)


def _bdot_nt(a, b):
    return lax.dot_general(a.astype(_BF16), b.astype(_BF16), _NT, preferred_element_type=_F32)


def _ada_kernel(c_ref, w_ref, b_ref, o_ref):
    c = c_ref[...]
    o_ref[...] = jnp.dot(c * _sigmoid(c), w_ref[...], precision=_HIGHEST,
                         preferred_element_type=_F32) + b_ref[...]


def _ada_mod(c, w_ada, b_ada):
    n, tn = c.shape[0], D_MODEL
    return pl.pallas_call(
        _ada_kernel,
        grid=(6 * D_MODEL // tn,),
        in_specs=[pl.BlockSpec((n, D_MODEL), lambda j: (0, 0)),
                  pl.BlockSpec((D_MODEL, tn), lambda j: (0, j)),
                  pl.BlockSpec((1, tn), lambda j: (0, j))],
        out_specs=pl.BlockSpec((n, tn), lambda j: (0, j)),
        out_shape=jax.ShapeDtypeStruct((n, 6 * D_MODEL), _F32),
        compiler_params=_params("parallel"),
        name="ada_mod",
    )(c, w_ada, b_ada.reshape(1, -1))


def _inproj_kernel(x_ref, mod_ref, g_ref, wn_ref, wt_ref, *out_refs, outs):
    x = x_ref[0]
    h = (_rms(x, g_ref[...]) * (1.0 + mod_ref[0, 1]) + mod_ref[0, 0]).astype(_BF16)
    pn = jnp.dot(h, wn_ref[...], preferred_element_type=_F32)
    pt = lax.dot_general(wt_ref[...], h, _NT, preferred_element_type=_F32)
    for ref, (kind, lo, hi, post) in zip(out_refs, outs):
        v = pn[:, lo:hi] if kind == 'n' else pt[lo:hi, :]
        if post == 'scale':
            v = v * (HEAD_DIM ** -0.5)
        elif post == 'sigmoid':
            v = _sigmoid(v)
        ref[0] = v.astype(ref.dtype)


def _in_proj(x, mod, norm_attn, wn, wt, tm, outs, dtypes):
    bsz, t, _ = x.shape
    r = mod.shape[2]
    rb = 1 if r == 1 else tm
    mod_map = (lambda b, i: (b, 0, 0, 0)) if r == 1 else (lambda b, i: (b, 0, i, 0))
    out_specs, out_shape = [], []
    for (kind, lo, hi, _), dt in zip(outs, dtypes):
        if kind == 'n':
            out_specs.append(pl.BlockSpec((1, tm, hi - lo), lambda b, i: (b, i, 0)))
            out_shape.append(jax.ShapeDtypeStruct((bsz, t, hi - lo), dt))
        else:
            out_specs.append(pl.BlockSpec((1, hi - lo, tm), lambda b, i: (b, 0, i)))
            out_shape.append(jax.ShapeDtypeStruct((bsz, hi - lo, t), dt))
    return pl.pallas_call(
        functools.partial(_inproj_kernel, outs=tuple(outs)),
        grid=(bsz, t // tm),
        in_specs=[pl.BlockSpec((1, tm, D_MODEL), lambda b, i: (b, i, 0)),
                  pl.BlockSpec((1, 6, rb, D_MODEL), mod_map),
                  pl.BlockSpec((1, D_MODEL), lambda b, i: (0, 0)),
                  pl.BlockSpec(wn.shape, lambda b, i: (0, 0)),
                  pl.BlockSpec(wt.shape, lambda b, i: (0, 0))],
        out_specs=out_specs,
        out_shape=out_shape,
        compiler_params=_params("parallel", "parallel"),
        name="in_proj",
    )(x, mod, norm_attn.reshape(1, -1), wn, wt)


def _ssm_tables(lam_re, lam_im, log_dt, b_re, b_im, c_re, c_im):
    hp = dict(precision=_HIGHEST)
    lr, li = lam_re.astype(_F32), lam_im.astype(_F32)
    dt = jnp.exp(log_dt.astype(_F32))[:, None]
    ar, ai = lr * dt, li * dt

    def power(k):
        mag = jnp.exp(ar * k)
        return mag * jnp.cos(ai * k), mag * jnp.sin(ai * k)

    n = SSM_CHUNK
    steps = jnp.arange(n + 1, dtype=_F32)[:, None, None]
    pwr, pwi = power(steps)
    lbr, lbi = pwr[1], pwi[1]
    den = lr * lr + li * li
    fr = ((lbr - 1.0) * lr + lbi * li) / den
    fi = (lbi * lr - (lbr - 1.0) * li) / den
    br_, bi_ = b_re.astype(_F32), b_im.astype(_F32)
    bbr = fr[:, :, None] * br_ - fi[:, :, None] * bi_
    bbi = fr[:, :, None] * bi_ + fi[:, :, None] * br_
    cr, ci = c_re.astype(_F32), c_im.astype(_F32)
    er = pwr[:, :, :, None] * bbr[None] - pwi[:, :, :, None] * bbi[None]
    ei = pwr[:, :, :, None] * bbi[None] + pwi[:, :, :, None] * bbr[None]
    kern = (jnp.einsum('gip,dgpj->dgij', cr, er[:n], **hp)
            - jnp.einsum('gip,dgpj->dgij', ci, ei[:n], **hp))
    lag = jnp.arange(n)[None, :] - jnp.arange(n)[:, None]
    toep = jnp.where((lag >= 0)[:, :, None, None, None], kern[jnp.clip(lag, 0)], 0.0)
    nb, gpb = N_SSM_GROUPS // SSM_LANE_GROUPS, SSM_LANE_GROUPS

    def expand(a, g_axis):
        w = a.shape[-1]
        owner = jnp.arange(gpb * w) // w
        own = jnp.arange(gpb).reshape((gpb,) + (1,) * (a.ndim - 1 - g_axis))
        return jnp.where(owner == own, jnp.tile(a, (1,) * (a.ndim - 1) + (gpb,)), 0.0)

    toep = toep.reshape(n, n, nb, gpb, SSM_GROUP, SSM_GROUP).transpose(2, 0, 3, 5, 1, 4)
    toep = expand(toep, 2).reshape(nb, n * LANES, n * LANES)
    rev = n - 1 - jnp.arange(n)

    def state_in(e):
        w = e[rev].reshape(n, nb, gpb, SSM_STATE, SSM_GROUP).transpose(1, 0, 2, 4, 3)
        return expand(w, 2).reshape(nb, n * LANES, gpb * SSM_STATE)

    w_in = jnp.concatenate([state_in(er), state_in(ei)], axis=-1)
    cvr = cr[None] * pwr[1:, :, None, :] - ci[None] * pwi[1:, :, None, :]
    cvi = cr[None] * pwi[1:, :, None, :] + ci[None] * pwr[1:, :, None, :]

    def state_out(cv):
        v = cv.reshape(n, nb, gpb, SSM_GROUP, SSM_STATE).transpose(1, 2, 4, 0, 3)
        return expand(v, 1).reshape(nb, gpb * SSM_STATE, n * LANES)

    v_out = jnp.concatenate([state_out(cvr), -state_out(cvi)], axis=1)
    a_chunk = jnp.stack([pwr[n].reshape(nb, -1), pwi[n].reshape(nb, -1)], axis=1)
    eye = jnp.eye(N_SSM_GROUPS, dtype=_F32)
    ns = N_SSM_GROUPS * SSM_STATE
    bd = lambda a: jnp.einsum('gpi,gh->gihp', a, eye).reshape(D_SSM, ns)
    cd = lambda a: jnp.einsum('gip,gh->gphi', a, eye).reshape(ns, D_SSM)
    return dict(toep=toep.astype(_BF16), w_in=w_in.astype(_BF16), v_out=v_out.astype(_BF16), a_chunk=a_chunk,
                bd_re=bd(bbr).astype(_BF16), bd_im=bd(bbi).astype(_BF16),
                cd_re=cd(cr).astype(_BF16), cd_im=cd(ci).astype(_BF16),
                lam_re=lbr.reshape(1, ns), lam_im=lbi.reshape(1, ns))


def _ssm_prompt_kernel(u_ref, toep_ref, w_ref, v_ref, a_ref, y_ref, s_ref, x_scr, sin_scr, *, n_chunks):
    n = SSM_CHUNK
    lhs = jnp.concatenate([u_ref[0, pl.ds(tau, n_chunks, stride=n), :] for tau in range(n)],
                          axis=1).astype(_BF16)
    x_scr[...] = jnp.dot(lhs, w_ref[0], preferred_element_type=_F32)
    half = x_scr.shape[1] // 2
    a_re = a_ref[0, 0:1, :]
    a_im = a_ref[0, 1:2, :]

    def step(c, carry):
        s_re, s_im = carry
        row = pl.ds(c, 1)
        sin_scr[row, :half] = s_re
        sin_scr[row, half:] = s_im
        x = x_scr[row, :]
        return (a_re * s_re - a_im * s_im + x[:, :half],
                a_re * s_im + a_im * s_re + x[:, half:])

    zero = jnp.zeros((1, half), _F32)
    s_re, s_im = lax.fori_loop(0, n_chunks, step, (zero, zero))
    s_ref[0, 0] = jnp.concatenate([s_re, s_im], axis=1)
    sin_b = sin_scr[...].astype(_BF16)
    for t2 in range(n // 2):
        k_hi = (t2 + 1) * 2 * LANES
        cols = slice(t2 * 2 * LANES, k_hi)
        y = (jnp.dot(lhs[:, :k_hi], toep_ref[0, :k_hi, cols], preferred_element_type=_F32)
             + jnp.dot(sin_b, v_ref[0, :, cols], preferred_element_type=_F32))
        for d in range(2):
            y_ref[0, pl.ds(2 * t2 + d, n_chunks, stride=n), :] = y[:, d * LANES:(d + 1) * LANES]


def _ssm_prompt(u, tabs):
    bsz, t, _ = u.shape
    nch = t // SSM_CHUNK
    nb = D_SSM // LANES
    sw = 2 * SSM_LANE_GROUPS * SSM_STATE
    table = lambda a: pl.BlockSpec((1,) + a.shape[1:], lambda m, b: (m, 0, 0), pipeline_mode=pl.Buffered(1))
    tok = pl.BlockSpec((1, t, LANES), lambda m, b: (b, 0, m))
    y, s = pl.pallas_call(
        functools.partial(_ssm_prompt_kernel, n_chunks=nch),
        grid=(nb, bsz),
        in_specs=[tok, table(tabs['toep']), table(tabs['w_in']), table(tabs['v_out']),
                  pl.BlockSpec((1, 2, sw // 2), lambda m, b: (m, 0, 0))],
        out_specs=[tok, pl.BlockSpec((1, 1, 1, sw), lambda m, b: (b, m, 0, 0))],
        out_shape=[jax.ShapeDtypeStruct((bsz, t, D_SSM), _F32),
                   jax.ShapeDtypeStruct((bsz, nb, 1, sw), _F32)],
        scratch_shapes=[pltpu.VMEM((nch, sw), _F32), pltpu.VMEM((nch, sw), _F32)],
        compiler_params=_params("parallel", "parallel"),
        name="ssm_prompt",
    )(u, tabs['toep'], tabs['w_in'], tabs['v_out'], tabs['a_chunk'])
    state = lambda a: a.reshape(bsz, N_SSM_GROUPS, SSM_STATE)
    return y, state(s[:, :, 0, :sw // 2]), state(s[:, :, 0, sw // 2:])


def _ssm_step_kernel(u_ref, s0re_ref, s0im_ref, lre_ref, lim_ref, bdre_ref, bdim_ref, cdre_ref, cdim_ref,
                     y_ref, sre_ref, sim_ref):
    u = u_ref[...]
    s_re, s_im = s0re_ref[...], s0im_ref[...]
    l_re, l_im = lre_ref[...], lim_ref[...]
    n_re = l_re * s_re - l_im * s_im + _bdot(u, bdre_ref[...])
    n_im = l_re * s_im + l_im * s_re + _bdot(u, bdim_ref[...])
    sre_ref[...] = n_re
    sim_ref[...] = n_im
    y_ref[...] = _bdot(n_re, cdre_ref[...]) - _bdot(n_im, cdim_ref[...])


def _ssm_step(u, s0_re, s0_im, tabs):
    n = u.shape[0]
    ns = N_SSM_GROUPS * SSM_STATE
    return pl.pallas_call(
        _ssm_step_kernel,
        out_shape=[jax.ShapeDtypeStruct((n, D_SSM), _F32),
                   jax.ShapeDtypeStruct((n, ns), _F32),
                   jax.ShapeDtypeStruct((n, ns), _F32)],
        compiler_params=pltpu.CompilerParams(vmem_limit_bytes=VMEM_LIMIT),
        name="ssm_step",
    )(u, s0_re, s0_im, tabs['lam_re'], tabs['lam_im'], tabs['bd_re'], tabs['bd_im'], tabs['cd_re'], tabs['cd_im'])


def _compress_tables(pe_k, w1_k, w2_k, pe_v, w1_v, w2_v):
    zeros = jnp.zeros((CMP_STRIDE, HEAD_DIM, CMP_HIDDEN), _F32)
    cols = []
    for half in range(2):
        for slot in range(KV_SLOTS):
            w1 = (w1_k if slot < N_KV_HEADS else w1_v)[half * CMP_STRIDE:(half + 1) * CMP_STRIDE]
            blk = jnp.stack([w1 if s == slot else zeros for s in range(KV_SLOTS)], axis=1)
            cols.append(blk.reshape(CMP_STRIDE * KV_ROW, CMP_HIDDEN))
    w_all = jnp.concatenate(cols, axis=1).astype(_BF16)
    z2 = jnp.zeros((CMP_HIDDEN, HEAD_DIM), _F32)
    w2 = jnp.concatenate(
        [jnp.concatenate([(w2_k if s < N_KV_HEADS else w2_v) if s == slot else z2 for s in range(KV_SLOTS)], axis=1)
         for slot in range(KV_SLOTS)], axis=0).astype(_BF16)
    pe = jnp.stack([pe_k.reshape(-1), pe_v.reshape(-1)], axis=0)
    pe = jnp.concatenate([pe, jnp.zeros((SUBLANES - 2, pe.shape[1]), _F32)], axis=0)
    w1f = jnp.stack([w1_k.reshape(-1, CMP_HIDDEN), w1_v.reshape(-1, CMP_HIDDEN)], axis=0)
    halves = lambda w1: jnp.concatenate([w1[:CMP_STRIDE].reshape(-1, CMP_HIDDEN),
                                         w1[CMP_STRIDE:].reshape(-1, CMP_HIDDEN)], axis=1)
    w_slot = jnp.stack([halves(w1_k if s < N_KV_HEADS else w1_v) for s in range(KV_SLOTS)], axis=0).astype(_BF16)
    return dict(w_all=w_all, w_slot=w_slot, tail=(w2, w2.T, pe, w1f))


def _compress_tail(h, w2_ref, w2t_ref, pe_ref, w1f_ref, n_rows):
    hw = KV_SLOTS * CMP_HIDDEN
    pe = pe_ref[...]
    bias_k = jnp.dot(pe, w1f_ref[0], precision=_HIGHEST, preferred_element_type=_F32)[0:1]
    bias_v = jnp.dot(pe, w1f_ref[1], precision=_HIGHEST, preferred_element_type=_F32)[1:2]
    bias = jnp.concatenate([bias_k, bias_k, bias_v, bias_v], axis=1)
    nxt = pltpu.roll(h[:, hw:], n_rows - 1, 0)
    hid = _gelu_tanh(h[:, :hw] + nxt + bias)
    row = lax.broadcasted_iota(jnp.int32, hid.shape, 0)
    hid = jnp.where(row < n_rows - 1, hid, 0.0).astype(_BF16)
    return (jnp.dot(hid, w2_ref[...], preferred_element_type=_F32),
            lax.dot_general(w2t_ref[...], hid, _NT, preferred_element_type=_F32))


def _store_compressed(out, out_t, o_ref, ot_ref):
    for s in range(KV_SLOTS):
        o_ref[0, s] = out[:, s * HEAD_DIM:(s + 1) * HEAD_DIM]
        ot_ref[0, s] = out_t[s * HEAD_DIM:(s + 1) * HEAD_DIM, :]


def _split3_keys(k):
    hi = k.astype(_BF16)
    lo = (k - hi.astype(_F32)).astype(_BF16)
    return jnp.concatenate([hi, lo, hi, jnp.zeros_like(hi)], axis=1)


def _split3_queries(q_t):
    hi = q_t.astype(_BF16)
    lo = (q_t - hi.astype(_F32)).astype(_BF16)
    return jnp.concatenate([hi, hi, lo, jnp.zeros_like(hi)], axis=0)


def _compress_prompt_kernel(x_ref, wall_ref, w2_ref, w2t_ref, pe_ref, w1f_ref, k3_ref, ot_ref, *, n_rows):
    h = jnp.dot(x_ref[0].astype(_BF16), wall_ref[...], preferred_element_type=_F32)
    out, out_t = _compress_tail(h, w2_ref, w2t_ref, pe_ref, w1f_ref, n_rows)
    for g in range(N_KV_HEADS):
        k3_ref[0, g] = _split3_keys(out[:, g * HEAD_DIM:(g + 1) * HEAD_DIM])
    for s in range(KV_SLOTS):
        ot_ref[0, s] = out_t[s * HEAD_DIM:(s + 1) * HEAD_DIM, :]


def _compress_out(nseq, nc, index_map, index_map_t):
    specs = [pl.BlockSpec((1, KV_SLOTS, nc, HEAD_DIM), index_map),
             pl.BlockSpec((1, KV_SLOTS, HEAD_DIM, nc), index_map_t)]
    shapes = [jax.ShapeDtypeStruct((nseq, KV_SLOTS, nc, HEAD_DIM), _F32),
              jax.ShapeDtypeStruct((nseq, KV_SLOTS, HEAD_DIM, nc), _F32)]
    return specs, shapes


def _compress_prompt(kv_cmp, ctabs):
    bsz, t, _ = kv_cmp.shape
    nc = t // CMP_STRIDE
    const = lambda a: pl.BlockSpec(a.shape, lambda b: (0,) * a.ndim)
    weights = (ctabs['w_all'],) + ctabs['tail']
    return pl.pallas_call(
        functools.partial(_compress_prompt_kernel, n_rows=nc),
        grid=(bsz,),
        in_specs=[pl.BlockSpec((1, nc, CMP_STRIDE * KV_ROW), lambda b: (b, 0, 0))] + [const(a) for a in weights],
        out_specs=[pl.BlockSpec((1, N_KV_HEADS, nc, KV_SLOTS * HEAD_DIM), lambda b: (b, 0, 0, 0)),
                   pl.BlockSpec((1, KV_SLOTS, HEAD_DIM, nc), lambda b: (b, 0, 0, 0))],
        out_shape=[jax.ShapeDtypeStruct((bsz, N_KV_HEADS, nc, KV_SLOTS * HEAD_DIM), _BF16),
                   jax.ShapeDtypeStruct((bsz, KV_SLOTS, HEAD_DIM, nc), _F32)],
        compiler_params=_params("parallel"),
        name="compress_prompt",
    )(kv_cmp.reshape(bsz, nc, CMP_STRIDE * KV_ROW), *weights)


def _page_copies(pt_ref, cache_hbm, buf, sem, seq, slot, n_pages):
    return [pltpu.make_async_copy(cache_hbm.at[pt_ref[seq, j]],
                                  buf.at[slot, :, pl.ds(j * PAGE_SIZE, PAGE_SIZE)],
                                  sem.at[slot]) for j in range(n_pages)]


def _gather_pages(pt_ref, cache_hbm, buf, sem, n_pages):
    b = pl.program_id(0)
    slot = lax.rem(b, 2)

    @pl.when(b == 0)
    def _():
        for cp in _page_copies(pt_ref, cache_hbm, buf, sem, 0, 0, n_pages):
            cp.start()

    @pl.when(b + 1 < pl.num_programs(0))
    def _():
        for cp in _page_copies(pt_ref, cache_hbm, buf, sem, b + 1, 1 - slot, n_pages):
            cp.start()

    for cp in _page_copies(pt_ref, cache_hbm, buf, sem, b, slot, n_pages):
        cp.wait()
    return slot


def _chunk_permutation():
    span = 2 * PAGE_SIZE
    r = jnp.arange(span)
    src = (r % (span // CMP_STRIDE)) * CMP_STRIDE + r // (span // CMP_STRIDE)
    return (src[:, None] == jnp.arange(span)[None, :]).astype(_BF16)


def _compress_paged_kernel(pt_ref, cache_hbm, perm_ref, wslot_ref, w2_ref, w2t_ref, pe_ref, w1f_ref,
                           o_ref, ot_ref, buf, sem, xr, *, n_pages, n_rows):
    slot = _gather_pages(pt_ref, cache_hbm, buf, sem, n_pages)
    span = 2 * PAGE_SIZE
    cps = span // CMP_STRIDE
    low = lax.broadcasted_iota(jnp.int32, (cps, LANES), 1) < HEAD_DIM

    def relayout(i, carry):
        xt = buf[slot, :, pl.ds(pl.multiple_of(i * span, span), span)].astype(_BF16)
        rows = lax.dot_general(perm_ref[...], xt, _NT, preferred_element_type=_F32)
        chunk_rows = pl.ds(pl.multiple_of(i * cps, cps), cps)
        for a in range(CMP_STRIDE // 2):
            for s in range(KV_SLOTS):
                tile = slice((s // 2) * LANES, (s // 2 + 1) * LANES)
                even = rows[2 * a * cps:(2 * a + 1) * cps, tile]
                odd = rows[(2 * a + 1) * cps:(2 * a + 2) * cps, tile]
                if s % 2 == 0:
                    pair = jnp.where(low, even, pltpu.roll(odd, HEAD_DIM, 1))
                else:
                    pair = jnp.where(low, pltpu.roll(even, HEAD_DIM, 1), odd)
                xr[s, chunk_rows, a * LANES:(a + 1) * LANES] = pair.astype(_BF16)
        return carry

    lax.fori_loop(0, n_pages // 2, relayout, 0, unroll=math.gcd(RELAYOUT_UNROLL, n_pages // 2))
    hs = [jnp.dot(xr[s], wslot_ref[s], preferred_element_type=_F32) for s in range(KV_SLOTS)]
    h = jnp.concatenate([v[:, :CMP_HIDDEN] for v in hs] + [v[:, CMP_HIDDEN:] for v in hs], axis=1)
    out, out_t = _compress_tail(h, w2_ref, w2t_ref, pe_ref, w1f_ref, n_rows)
    _store_compressed(out, out_t, o_ref, ot_ref)


def _compress_paged(cache_t, page_table, ctabs):
    nseq, n_pages = page_table.shape
    past = n_pages * PAGE_SIZE
    nc = past // CMP_STRIDE
    perm = _chunk_permutation()
    const = lambda a: pl.BlockSpec(a.shape, lambda b, pt: (0,) * a.ndim)
    out_specs, out_shape = _compress_out(nseq, nc, lambda b, pt: (b, 0, 0, 0), lambda b, pt: (b, 0, 0, 0))
    weights = (ctabs['w_slot'],) + ctabs['tail']
    grid_spec = pltpu.PrefetchScalarGridSpec(
        num_scalar_prefetch=1,
        grid=(nseq,),
        in_specs=[pl.BlockSpec(memory_space=pl.ANY), const(perm)] + [const(a) for a in weights],
        out_specs=out_specs,
        scratch_shapes=[pltpu.VMEM((2, KV_ROW, past), _F32), pltpu.SemaphoreType.DMA((2,)),
                        pltpu.VMEM((KV_SLOTS, nc, CMP_STRIDE * HEAD_DIM), _BF16)],
    )
    return pl.pallas_call(
        functools.partial(_compress_paged_kernel, n_pages=n_pages, n_rows=nc),
        grid_spec=grid_spec,
        out_shape=out_shape,
        compiler_params=_params("arbitrary"),
        name="compress_paged",
    )(page_table, cache_t, perm, *weights)


def _importance_map(n_cmp_rows, n_blocks_rows, n_cmp):
    ratio = SLC_BLOCK // CMP_STRIDE
    j = jnp.arange(n_blocks_rows)[:, None]
    n = jnp.arange(n_cmp_rows)[None, :]
    off = n - ratio * j
    w = jnp.where((off == -1) | (off == ratio - 1), 1.0, jnp.where((off >= 0) & (off < ratio - 1), 2.0, 0.0))
    return jnp.where(n < n_cmp, w, 0.0).astype(_F32)


def _softmax(s, ok, axis):
    s = jnp.where(ok, s, NEG_INF)
    m = jnp.max(s, axis=axis, keepdims=True)
    m = jnp.where(m == NEG_INF, 0.0, m)
    e = jnp.exp(s - m)
    den = jnp.sum(e, axis=axis, keepdims=True)
    return e / jnp.where(den > 0, den, 1.0)


def _nsa_prompt_kernel(q_ref, gt_ref, kc_ref, vct_ref, ksn_ref, vst_ref, kwn_ref, vwt_ref, mt_ref, o_ref,
                       bias_ref, s_scr, *, tq, kc_len, n_cmp, n_blk):
    g = pl.program_id(1)
    t0 = pl.program_id(2) * tq
    cols = GROUP_SIZE * tq
    q_t = q_ref[0]
    q4 = jnp.concatenate([q_t[r * HEAD_DIM:(r + 1) * HEAD_DIM, :] for r in range(GROUP_SIZE)], axis=1)
    q4b = (q4 * LOG2_E).astype(_BF16)
    feat = lax.broadcasted_iota(jnp.int32, (KV_ROW, cols), 0)
    qp = jnp.where(lax.shift_right_logical(feat, int(math.log2(HEAD_DIM))) == g,
                   jnp.concatenate([q4b] * KV_SLOTS, axis=0), jnp.zeros((), _BF16))
    lane = lax.broadcasted_iota(jnp.int32, (1, cols), 1)
    qpos = t0 + (lane & (tq - 1))

    ncr = kc_ref.shape[2]
    s_c = jnp.dot(kc_ref[0, 0], _split3_queries(q4), preferred_element_type=_F32)
    n_idx = lax.broadcasted_iota(jnp.int32, (ncr, 1), 0)
    cmp_end = jnp.where(n_idx < n_cmp, n_idx * CMP_STRIDE + (CMP_BLOCK - 1), jnp.int32(2 ** 30))
    p_c = _softmax(s_c, cmp_end <= qpos, 0)
    o_c = _bdot(vct_ref[0, 0], p_c)

    imp = p_c[:, 0:tq]
    for r in range(1, GROUP_SIZE):
        imp = imp + p_c[:, r * tq:(r + 1) * tq]
    imp_hi = imp.astype(_BF16)
    imp_r = imp - imp_hi.astype(_F32)
    imp_mid = imp_r.astype(_BF16)
    imp_lo = (imp_r - imp_mid.astype(_F32)).astype(_BF16)
    imp_s = jnp.dot(mt_ref[...], jnp.concatenate([imp_hi, imp_mid, imp_lo], axis=0),
                    preferred_element_type=_F32)
    qrow = qpos[:, :tq]
    j_idx = lax.broadcasted_iota(jnp.int32, (n_blk, tq), 0)
    valid = j_idx * SLC_BLOCK <= qrow
    cur = lax.shift_right_arithmetic(qrow, int(math.log2(SLC_BLOCK)))
    forced = (j_idx == 0) | (valid & (j_idx > cur - N_LOCAL))
    score = jnp.where(forced, FORCED_SCORE, jnp.where(valid, imp_s, -1.0))
    rank = jnp.zeros((n_blk, tq), jnp.int32)
    row8 = lax.broadcasted_iota(jnp.int32, (SUBLANES, tq), 0)
    for j in range(n_blk):
        sj = score[j:j + 1, :]
        lo, hi = j // SUBLANES * SUBLANES, (j // SUBLANES + 1) * SUBLANES
        own = score[lo:hi]
        parts = [jnp.where(row8 > j - lo, jnp.where(sj >= own, 1, 0), jnp.where(sj > own, 1, 0))]
        if lo > 0:
            parts.insert(0, jnp.where(sj > score[:lo], 1, 0))
        if hi < n_blk:
            parts.append(jnp.where(sj >= score[hi:], 1, 0))
        rank = rank + jnp.concatenate(parts, axis=0)
    bias = jnp.where(valid & (rank < N_SELECT), 0.0, NEG_INF)
    bias_ref[...] = jnp.concatenate([bias] * GROUP_SIZE, axis=1)

    bpc = kc_len // SLC_BLOCK

    def scores(c, causal):
        k0 = pl.multiple_of(c * kc_len, kc_len)
        s = jnp.dot(ksn_ref[0, pl.ds(k0, kc_len), :], qp, preferred_element_type=_F32)
        b8 = bias_ref[pl.ds(pl.multiple_of(c * bpc, bpc), bpc), :]
        s = jnp.concatenate([s[i * SLC_BLOCK:(i + 1) * SLC_BLOCK, :] + b8[i:i + 1, :] for i in range(bpc)], axis=0)
        if causal:
            kpos = k0 + lax.broadcasted_iota(jnp.int32, (kc_len, 1), 0)
            s = jnp.where(kpos <= qpos, s, NEG_INF)
        return s

    fold = lambda s: jnp.max(s.reshape(s.shape[0] // SUBLANES, SUBLANES, cols), axis=0)

    def weighted(s, m, vals_t):
        ones = jnp.ones((2 * SUBLANES, vals_t.shape[1]), _BF16)
        return jnp.dot(jnp.concatenate([vals_t, ones], axis=0), jnp.exp2(s - m).astype(_BF16),
                       preferred_element_type=_F32)

    def pass1(c, m_run):
        s = scores(c, False)
        s_scr[c] = s
        return jnp.maximum(m_run, fold(s))

    c_diag = t0 // kc_len
    m_run = lax.fori_loop(0, c_diag, pass1, jnp.full((SUBLANES, cols), NEG_INF, _F32))
    s_d = scores(c_diag, True)
    m_s = jnp.max(jnp.maximum(m_run, fold(s_d)), axis=0, keepdims=True)

    def pass2(c, acc):
        k0 = pl.multiple_of(c * kc_len, kc_len)
        return acc + weighted(s_scr[c], m_s, vst_ref[0, :, pl.ds(k0, kc_len)])

    k_d = pl.multiple_of(c_diag * kc_len, kc_len)
    acc_s = lax.fori_loop(0, c_diag, pass2, weighted(s_d, m_s, vst_ref[0, :, pl.ds(k_d, kc_len)]))
    o_s = acc_s[:HEAD_DIM] / acc_s[HEAD_DIM:HEAD_DIM + 1]

    wk = tq + WINDOW
    w0 = pl.multiple_of(jnp.maximum(t0 - WINDOW, 0), tq)
    dpos = qpos - (w0 + lax.broadcasted_iota(jnp.int32, (wk, 1), 0))
    s_w = jnp.where(jnp.where(dpos >= 0, dpos, WINDOW) < WINDOW,
                    jnp.dot(kwn_ref[0, pl.ds(w0, wk), :], qp, preferred_element_type=_F32), NEG_INF)
    acc_w = weighted(s_w, jnp.max(s_w, axis=0, keepdims=True), vwt_ref[0, :, pl.ds(w0, wk)])
    o_w = acc_w[:HEAD_DIM] / acc_w[HEAD_DIM:HEAD_DIM + 1]

    outs = []
    for r in range(GROUP_SIZE):
        sl = slice(r * tq, (r + 1) * tq)
        gate = lambda branch: gt_ref[0, pl.ds(branch * N_HEADS + g * GROUP_SIZE + r, 1), :]
        outs.append(gate(0) * o_c[:, sl] + gate(1) * o_s[:, sl] + gate(2) * o_w[:, sl])
    o_ref[0] = jnp.concatenate(outs, axis=0).T


def _nsa_prompt(q_t, gates_t, kcmp3, kcmp_t, ks_n, ks_tb, kw_n, kw_tb, tq, kc_len):
    bsz, _, t = q_t.shape
    ncr = kcmp3.shape[2]
    n_blk = t // SLC_BLOCK
    mt = jnp.tile(_importance_map(ncr, n_blk, ncr - 1), (1, 3)).astype(_BF16)
    rows_n = pl.BlockSpec((1, t, KV_ROW), lambda b, g, i: (b, 0, 0))
    vals_t = pl.BlockSpec((1, HEAD_DIM, t), lambda b, g, i: (b, N_KV_HEADS + g, 0))
    qw = GROUP_SIZE * HEAD_DIM
    return pl.pallas_call(
        functools.partial(_nsa_prompt_kernel, tq=tq, kc_len=kc_len, n_cmp=ncr - 1, n_blk=n_blk),
        grid=(bsz, N_KV_HEADS, t // tq),
        in_specs=[pl.BlockSpec((1, qw, tq), lambda b, g, i: (b, g, i)),
                  pl.BlockSpec((1, GATE_ROWS, tq), lambda b, g, i: (b, 0, i)),
                  pl.BlockSpec((1, 1, ncr, KV_SLOTS * HEAD_DIM), lambda b, g, i: (b, g, 0, 0)),
                  pl.BlockSpec((1, 1, HEAD_DIM, ncr), lambda b, g, i: (b, N_KV_HEADS + g, 0, 0)),
                  rows_n, vals_t, rows_n, vals_t,
                  pl.BlockSpec(mt.shape, lambda b, g, i: (0, 0))],
        out_specs=pl.BlockSpec((1, tq, qw), lambda b, g, i: (b, i, g)),
        out_shape=jax.ShapeDtypeStruct((bsz, t, D_ATTN), _F32),
        scratch_shapes=[pltpu.VMEM((n_blk, GROUP_SIZE * tq), _F32),
                        pltpu.VMEM((t // kc_len, kc_len, GROUP_SIZE * tq), _F32)],
        compiler_params=_params("parallel", "parallel", "parallel"),
        name="nsa_prompt",
    )(q_t, gates_t, kcmp3, kcmp_t, ks_n, ks_tb, kw_n, kw_tb, mt)


def _nsa_sample_kernel(pt_ref, q_ref, gt_ref, kcmp_ref, kcmpt_ref, slc_hbm, ksn_ref, win_ref, kwt_ref,
                       m_ref, e_ref, o_ref, nw_ref, buf, sem,
                       *, n_pages, past, n_cmp, blk_pad, n_buf):
    b = pl.program_id(0)
    slot = _gather_pages(pt_ref, slc_hbm, buf, sem, n_pages)
    q = q_ref[0]
    gt = gt_ref[0]
    ks_new = ksn_ref[0]
    nseq = kwt_ref.shape[1]
    seq_lane = lax.broadcasted_iota(jnp.int32, (KV_ROW, nseq), 1)
    kw_col = jnp.sum(jnp.where(seq_lane == b, kwt_ref[...], 0.0), axis=1, keepdims=True)
    win = win_ref[0]
    pos = lax.broadcasted_iota(jnp.int32, win.shape, 1)
    band = jnp.where(pos == n_buf - 1, kw_col, pltpu.roll(win, n_buf - 1, 1))
    nw_ref[0] = band
    band_b = band.astype(_BF16)
    ncr = kcmp_ref.shape[2]
    lane = lax.broadcasted_iota(jnp.int32, (SUBLANES, LANES), 1)
    hrow = lax.broadcasted_iota(jnp.int32, (SUBLANES, LANES), 0)
    q8s, o_cs, sels = [], [], []
    for g in range(N_KV_HEADS):
        q8 = jnp.concatenate([q[:, (g * GROUP_SIZE + r) * HEAD_DIM:(g * GROUP_SIZE + r + 1) * HEAD_DIM]
                              for r in range(GROUP_SIZE)]
                             + [jnp.zeros((SUBLANES - GROUP_SIZE, HEAD_DIM), _F32)], axis=0)
        q8s.append(q8)
        s_c = lax.dot_general(q8, kcmp_ref[0, g], _NT, precision=_HIGHEST, preferred_element_type=_F32)
        n_idx = lax.broadcasted_iota(jnp.int32, (SUBLANES, ncr), 1)
        cmp_end = jnp.where(n_idx < n_cmp, n_idx * CMP_STRIDE + (CMP_BLOCK - 1), jnp.int32(2 ** 30))
        p_c = _softmax(s_c, cmp_end <= past, 1)
        o_cs.append(_bdot_nt(p_c, kcmpt_ref[0, N_KV_HEADS + g]))

        imp = jnp.sum(p_c[:GROUP_SIZE], axis=0, keepdims=True)
        imp_s = jnp.dot(jnp.broadcast_to(imp, (SUBLANES, ncr)), m_ref[...], precision=_HIGHEST,
                        preferred_element_type=_F32)[0:1]
        j_idx = lax.broadcasted_iota(jnp.int32, (1, blk_pad), 1)
        valid = j_idx * SLC_BLOCK <= past
        forced = (j_idx == 0) | (valid & (j_idx > past // SLC_BLOCK - N_LOCAL))
        score = jnp.where(forced, FORCED_SCORE, jnp.where(valid, imp_s, -1.0))
        s_k = jnp.broadcast_to(score, (blk_pad, blk_pad))
        s_j = s_k.T
        jj = lax.broadcasted_iota(jnp.int32, (blk_pad, blk_pad), 0)
        kk = lax.broadcasted_iota(jnp.int32, (blk_pad, blk_pad), 1)
        beats = jnp.where(kk > jj, jnp.where(s_j >= s_k, 1.0, 0.0), jnp.where(s_j > s_k, 1.0, 0.0))
        rank = jnp.sum(beats, axis=0, keepdims=True)
        sels.append(jnp.where(valid & (rank < N_SELECT), 1.0, 0.0))
    row_g = lax.broadcasted_iota(jnp.int32, (SUBLANES, blk_pad), 0)
    sel_rows = jnp.zeros((SUBLANES, blk_pad), _F32)
    for g in range(N_KV_HEADS):
        sel_rows = jnp.where(row_g == g, sels[g], sel_rows)
    picked = jnp.dot(sel_rows.astype(_BF16), e_ref[...], preferred_element_type=_F32)

    heads = []
    for g in range(N_KV_HEADS):
        q8 = q8s[g]
        q8b = q8.astype(_BF16)
        k_rows = slice(g * HEAD_DIM, (g + 1) * HEAD_DIM)
        v_rows = slice((N_KV_HEADS + g) * HEAD_DIM, (N_KV_HEADS + g + 1) * HEAD_DIM)
        s_s = jnp.where(picked[g:g + 1, :] > 0.5, _bdot(q8b, buf[slot, k_rows, :]), NEG_INF)
        s_n = jnp.sum(q8 * ks_new[:, k_rows], axis=-1, keepdims=True)
        m = jnp.maximum(jnp.max(s_s, axis=-1, keepdims=True), s_n)
        e_p, e_n = jnp.exp(s_s - m), jnp.exp(s_n - m)
        num = _bdot_nt(e_p, buf[slot, v_rows, :])
        o_s = (num + e_n * ks_new[:, v_rows]) / (jnp.sum(e_p, axis=-1, keepdims=True) + e_n)

        s_w = jnp.dot(q8b, band_b[k_rows, :], preferred_element_type=_F32)
        i_idx = lax.broadcasted_iota(jnp.int32, (SUBLANES, n_buf), 1)
        p_w = _softmax(s_w, (n_buf - 1 - i_idx) < WINDOW, 1)
        o_w = _bdot_nt(p_w, band_b[v_rows, :])

        def gate(branch):
            tgt = branch * N_HEADS + g * GROUP_SIZE + hrow
            return jnp.sum(jnp.where(lane == tgt, jnp.broadcast_to(gt, (SUBLANES, LANES)), 0.0),
                           axis=-1, keepdims=True)

        o = gate(0) * o_cs[g] + gate(1) * o_s + gate(2) * o_w
        heads += [o[r:r + 1, :] for r in range(GROUP_SIZE)]
    o_ref[0] = jnp.concatenate(heads, axis=1)


def _expand_table(past, blk_pad):
    return (jnp.arange(past)[None, :] // SLC_BLOCK == jnp.arange(blk_pad)[:, None]).astype(_BF16)


def _nsa_sample(q, gates, kcmp, kcmp_t, cache_t, page_table, ks_new, win_t, kw_t):
    nseq, n_pages = page_table.shape
    past = n_pages * PAGE_SIZE
    ncr = kcmp.shape[2]
    n_buf = win_t.shape[2]
    n_blk = past // SLC_BLOCK + 1
    blk_pad = -(-n_blk // LANES) * LANES
    m_tab = _importance_map(ncr, blk_pad, ncr - 1).T
    e = _expand_table(past, blk_pad)
    row = lambda w: pl.BlockSpec((1, 1, w), lambda b, pt: (b, 0, 0))
    const = lambda a: pl.BlockSpec(a.shape, lambda b, pt: (0,) * a.ndim)
    win_spec = pl.BlockSpec((1, KV_ROW, n_buf), lambda b, pt: (b, 0, 0))
    grid_spec = pltpu.PrefetchScalarGridSpec(
        num_scalar_prefetch=1,
        grid=(nseq,),
        in_specs=[row(D_ATTN), row(LANES),
                  pl.BlockSpec((1, KV_SLOTS, ncr, HEAD_DIM), lambda b, pt: (b, 0, 0, 0)),
                  pl.BlockSpec((1, KV_SLOTS, HEAD_DIM, ncr), lambda b, pt: (b, 0, 0, 0)),
                  pl.BlockSpec(memory_space=pl.ANY),
                  row(KV_ROW), win_spec, const(kw_t), const(m_tab), const(e)],
        out_specs=[row(D_ATTN), win_spec],
        scratch_shapes=[pltpu.VMEM((2, KV_ROW, past), _F32), pltpu.SemaphoreType.DMA((2,))],
    )
    return pl.pallas_call(
        functools.partial(_nsa_sample_kernel, n_pages=n_pages, past=past, n_cmp=ncr - 1,
                          blk_pad=blk_pad, n_buf=n_buf),
        grid_spec=grid_spec,
        out_shape=[jax.ShapeDtypeStruct((nseq, 1, D_ATTN), _F32),
                   jax.ShapeDtypeStruct((nseq, KV_ROW, n_buf), _F32)],
        compiler_params=_params("arbitrary"),
        name="nsa_sample",
    )(page_table, q, gates, kcmp, kcmp_t, cache_t, ks_new, win_t, kw_t, m_tab, e)


def _post_kernel(x_ref, y_ref, u_ref, oa_ref, mod_ref, d_ref, wglu_ref, gs_ref, ga_ref, wout_ref, gm_ref,
                 wup_ref, wdn_ref, gf_ref, o_ref, *, ff_chunk):
    z = _gelu_tanh(y_ref[0] + d_ref[...] * u_ref[0])
    o_ssm = z * _sigmoid(jnp.dot(z.astype(_BF16), wglu_ref[...], preferred_element_type=_F32))
    mix = (jnp.dot(_rms(o_ssm, gs_ref[...]).astype(_BF16), wout_ref[:D_SSM, :], preferred_element_type=_F32)
           + jnp.dot(_rms(oa_ref[0], ga_ref[...]).astype(_BF16), wout_ref[D_SSM:, :], preferred_element_type=_F32))
    x1 = x_ref[0] + mod_ref[0, 2] * mix
    h2 = (_rms(x1, gm_ref[...]) * (1.0 + mod_ref[0, 4]) + mod_ref[0, 3]).astype(_BF16)
    acc = jnp.zeros_like(x1)
    for k in range(D_FF // ff_chunk):
        cols = slice(k * ff_chunk, (k + 1) * ff_chunk)
        hid = jnp.maximum(jnp.dot(h2, wup_ref[:, cols], preferred_element_type=_F32), 0.0)
        acc = acc + jnp.dot((hid * hid).astype(_BF16), wdn_ref[cols, :], preferred_element_type=_F32)
    o_ref[0] = _rms(x1 + mod_ref[0, 5] * acc, gf_ref[...])


def _post(x, y_ssm, u, o_attn, mod, lp, norm_final, tm):
    bsz, t, _ = x.shape
    r = mod.shape[2]
    rb = 1 if r == 1 else tm
    mod_map = (lambda b, i: (b, 0, 0, 0)) if r == 1 else (lambda b, i: (b, 0, i, 0))
    tok = lambda w: pl.BlockSpec((1, tm, w), lambda b, i: (b, i, 0))
    const = lambda shape: pl.BlockSpec(shape, lambda b, i: (0,) * len(shape), pipeline_mode=pl.Buffered(1))
    vec = lambda v: v.reshape(1, -1).astype(_F32)
    weights = [vec(lp['ssm_d']), lp['ssm_w_glu'].astype(_BF16), vec(lp['norm_out_ssm']), vec(lp['norm_out_attn']),
               lp['w_out'].astype(_BF16), vec(lp['norm_mlp']), lp['w_up'].astype(_BF16), lp['w_down'].astype(_BF16),
               vec(norm_final)]
    return pl.pallas_call(
        functools.partial(_post_kernel, ff_chunk=D_MODEL),
        grid=(bsz, t // tm),
        in_specs=[tok(D_MODEL), tok(D_SSM), tok(D_SSM), tok(D_ATTN),
                  pl.BlockSpec((1, 6, rb, D_MODEL), mod_map)] + [const(w.shape) for w in weights],
        out_specs=tok(D_MODEL),
        out_shape=jax.ShapeDtypeStruct((bsz, t, D_MODEL), _F32),
        compiler_params=_params("parallel", "parallel"),
        name="post_mlp",
    )(x, y_ssm, u, o_attn, mod, *weights)


def _feature_major(a):
    lead = a.shape[:-4]
    n = len(lead)
    return a.transpose(*range(n), n + 1, n + 2, n + 3, n).reshape(*lead, KV_ROW, a.shape[-4])


def _kv_output(a_t):
    n, _, t = a_t.shape
    return a_t.reshape(n, 2, N_KV_HEADS, HEAD_DIM, t).transpose(0, 4, 1, 2, 3)[None]


def kernel(x_prompt, x_sample, cache_cmp, cache_slc, state_win, state_ssm_re, state_ssm_im, page_table,
           c_prompt, c_sample, w_ada, b_ada, norm_attn, w_in, ssm_lambda_re, ssm_lambda_im, ssm_log_dt,
           ssm_b_re, ssm_b_im, ssm_c_re, ssm_c_im, ssm_d, ssm_w_glu, cmp_pe_k, cmp_w1_k, cmp_w2_k,
           cmp_pe_v, cmp_w1_v, cmp_w2_v, norm_out_ssm, norm_out_attn, w_out, norm_mlp, w_up, w_down,
           norm_final):
    depth = w_ada.shape[0]
    assert depth == 1, "single-layer trunk"
    l = 0
    bsz, t, _ = x_prompt.shape
    nseq = x_sample.shape[0]
    assert x_sample.shape[1] == 1
    tm = min(512, t)
    tq = min(256, t)
    kc_len = min(512, t)

    lp = dict(ssm_d=ssm_d[l], ssm_w_glu=ssm_w_glu[l], norm_out_ssm=norm_out_ssm[l],
              norm_out_attn=norm_out_attn[l], w_out=w_out[l], norm_mlp=norm_mlp[l], w_up=w_up[l], w_down=w_down[l])
    w_full = jnp.pad(w_in[l], ((0, 0), (0, D_IN_PAD - D_IN))).astype(_BF16)
    w_rows = jnp.concatenate([w_full[:, _Q0:_G0 + 3 * N_HEADS].T,
                              jnp.zeros((GATE_ROWS - 3 * N_HEADS, D_MODEL), _BF16)], axis=0)
    tq0, tkc0, tks0, tkw0, tg0 = 0, D_ATTN, D_ATTN + KV_ROW, D_ATTN + 2 * KV_ROW, D_ATTN + 3 * KV_ROW
    tabs = _ssm_tables(ssm_lambda_re[l], ssm_lambda_im[l], ssm_log_dt[l], ssm_b_re[l], ssm_b_im[l],
                       ssm_c_re[l], ssm_c_im[l])
    ctabs = _compress_tables(cmp_pe_k[l], cmp_w1_k[l], cmp_w2_k[l], cmp_pe_v[l], cmp_w1_v[l], cmp_w2_v[l])

    mod = _ada_mod(jnp.concatenate([c_prompt, c_sample], axis=0), w_ada[l], b_ada[l])
    mod_p = mod[:bsz].reshape(bsz, 6, 1, D_MODEL)
    mod_s = mod[bsz:].reshape(nseq, 6, D_MODEL).transpose(1, 0, 2)[None]

    w_tok = jnp.concatenate([w_full[:, :D_SSM], w_full[:, _KC0:_G0]], axis=1)
    outs_p = [('n', 0, D_SSM, None), ('n', D_SSM, D_SSM + KV_ROW, None),
              ('n', D_SSM + KV_ROW, D_SSM + 2 * KV_ROW, None), ('n', D_SSM + 2 * KV_ROW, D_SSM + 3 * KV_ROW, None),
              ('t', tq0, tkc0, 'scale'), ('t', tkc0, tks0, None), ('t', tks0, tkw0, None), ('t', tkw0, tg0, None),
              ('t', tks0, tkw0, None), ('t', tkw0, tg0, None), ('t', tg0, tg0 + GATE_ROWS, 'sigmoid')]
    dt_p = [_F32, _F32, _BF16, _BF16, _F32, _F32, _F32, _F32, _BF16, _BF16, _F32]
    (u, kc_n, ks_n, kw_n, q_t, kc_t, ks_t, kw_t, ks_tb, kw_tb, gates_t) = _in_proj(
        x_prompt, mod_p, norm_attn[l], w_tok, w_rows, tm, outs_p, dt_p)
    y_ssm, sp_re, sp_im = _ssm_prompt(u, tabs)
    kcmp, kcmp_t = _compress_prompt(kc_n, ctabs)
    o_attn = _nsa_prompt(q_t, gates_t, kcmp, kcmp_t, ks_n, ks_tb, kw_n, kw_tb, tq, kc_len)
    y_prompt = _post(x_prompt, y_ssm, u, o_attn, mod_p, lp, norm_final, tm)
    w_keep = min(WINDOW, t)

    xs = x_sample.reshape(1, nseq, D_MODEL)
    outs_s = [('n', 0, D_SSM, None), ('n', _Q0, _KC0, 'scale'), ('n', _KS0, _KW0, None), ('n', _G0, D_IN_PAD, 'sigmoid'),
              ('t', tkc0, tks0, None), ('t', tks0, tkw0, None), ('t', tkw0, tg0, None)]
    u_s, q_s, ks_row, gates_s, kc_ts, ks_ts, kw_ts = _in_proj(
        xs, mod_s, norm_attn[l], w_full, w_rows, nseq, outs_s, [_F32] * len(outs_s))
    ns = N_SSM_GROUPS * SSM_STATE
    y_s, ss_re, ss_im = _ssm_step(u_s[0], state_ssm_re[l].reshape(nseq, ns), state_ssm_im[l].reshape(nseq, ns), tabs)
    kcmp_s, kcmp_ts = _compress_paged(_feature_major(cache_cmp[l]), page_table, ctabs)
    o_attn_s, new_win_t = _nsa_sample(q_s.reshape(nseq, 1, D_ATTN), gates_s.reshape(nseq, 1, LANES), kcmp_s, kcmp_ts,
                                      _feature_major(cache_slc[l]), page_table, ks_row.reshape(nseq, 1, KV_ROW),
                                      _feature_major(state_win[l]), kw_ts[0])
    y_sample = _post(xs, y_s[None], u_s, o_attn_s.reshape(1, nseq, D_ATTN), mod_s, lp, norm_final, nseq)
    st_shape = (1, nseq, N_SSM_GROUPS, SSM_STATE)
    new_row = lambda a_t: _kv_output(a_t[0].T[:, :, None])

    return (y_prompt, y_sample.reshape(nseq, 1, D_MODEL),
            _kv_output(kc_t), _kv_output(ks_t), _kv_output(kw_t[:, :, t - w_keep:]),
            sp_re[None], sp_im[None],
            new_row(kc_ts), new_row(ks_ts), _kv_output(new_win_t),
            ss_re.reshape(st_shape), ss_im.reshape(st_shape))
```

```python
import functools
import math

import jax
import jax.numpy as jnp
from jax import lax
from jax.experimental import pallas as pl
from jax.experimental.pallas import tpu as pltpu

D_MODEL = 1024
D_SSM = D_MODEL // 2
SSM_GROUP = 16
N_SSM_GROUPS = D_SSM // SSM_GROUP
SSM_STATE = 64
HEAD_DIM = 64
D_ATTN = D_MODEL - D_SSM
N_HEADS = D_ATTN // HEAD_DIM
N_KV_HEADS = 2
GROUP_SIZE = N_HEADS // N_KV_HEADS
D_KV = N_KV_HEADS * HEAD_DIM
CMP_BLOCK = 32
CMP_STRIDE = 16
CMP_HIDDEN = 2 * HEAD_DIM
SLC_BLOCK = 64
N_SELECT = 16
N_LOCAL = 2
WINDOW = 512
D_FF = 4 * D_MODEL
D_IN = D_SSM + D_ATTN + 6 * D_KV + 3 * N_HEADS
EPS = 1e-6
PAGE_SIZE = 128

LANES = 128
SUBLANES = 8
D_IN_PAD = -(-D_IN // LANES) * LANES
KV_SLOTS = 2 * N_KV_HEADS
KV_ROW = KV_SLOTS * HEAD_DIM
GATE_ROWS = 32
SSM_CHUNK = 16
SSM_LANE_GROUPS = LANES // SSM_GROUP
VMEM_LIMIT = 56 * 1024 * 1024
FORCED_SCORE = 1e30
NEG_INF = float("-inf")
LOG2_E = math.log2(math.e)
RELAYOUT_UNROLL = 8

_Q0, _KC0, _KS0, _KW0, _G0 = D_SSM, D_MODEL, D_MODEL + KV_ROW, D_MODEL + 2 * KV_ROW, D_MODEL + 3 * KV_ROW

_BF16 = jnp.bfloat16
_F32 = jnp.float32
_NT = (((1,), (1,)), ((), ()))
_HIGHEST = lax.Precision.HIGHEST


def _params(*sem):
    return pltpu.CompilerParams(dimension_semantics=sem, vmem_limit_bytes=VMEM_LIMIT)


def _rms(x, g):
    return x * lax.rsqrt(jnp.mean(x * x, axis=-1, keepdims=True) + EPS) * g


def _gelu_tanh(x):
    return x * (0.5 * (1.0 + jnp.tanh(math.sqrt(2.0 / math.pi) * (x + 0.044715 * (x * x * x)))))


def _sigmoid(x):
    return 1.0 / (1.0 + jnp.exp(-x))


def _bdot(a, b):
    return jnp.dot(a.astype(_BF16), b.astype(_BF16), preferred_element_type=_F32)


def _bdot_nt(a, b):
    return lax.dot_general(a.astype(_BF16), b.astype(_BF16), _NT, preferred_element_type=_F32)


def _ada_kernel(c_ref, w_ref, b_ref, o_ref):
    c = c_ref[...]
    o_ref[...] = jnp.dot(c * _sigmoid(c), w_ref[...], precision=_HIGHEST,
                         preferred_element_type=_F32) + b_ref[...]


def _ada_mod(c, w_ada, b_ada):
    n, tn = c.shape[0], D_MODEL
    return pl.pallas_call(
        _ada_kernel,
        grid=(6 * D_MODEL // tn,),
        in_specs=[pl.BlockSpec((n, D_MODEL), lambda j: (0, 0)),
                  pl.BlockSpec((D_MODEL, tn), lambda j: (0, j)),
                  pl.BlockSpec((1, tn), lambda j: (0, j))],
        out_specs=pl.BlockSpec((n, tn), lambda j: (0, j)),
        out_shape=jax.ShapeDtypeStruct((n, 6 * D_MODEL), _F32),
        compiler_params=_params("parallel"),
        name="ada_mod",
    )(c, w_ada, b_ada.reshape(1, -1))


def _inproj_kernel(x_ref, mod_ref, g_ref, wn_ref, wt_ref, *out_refs, outs):
    x = x_ref[0]
    h = (_rms(x, g_ref[...]) * (1.0 + mod_ref[0, 1]) + mod_ref[0, 0]).astype(_BF16)
    pn = jnp.dot(h, wn_ref[...], preferred_element_type=_F32)
    pt = lax.dot_general(wt_ref[...], h, _NT, preferred_element_type=_F32)
    for ref, (kind, lo, hi, post) in zip(out_refs, outs):
        v = pn[:, lo:hi] if kind == 'n' else pt[lo:hi, :]
        if post == 'scale':
            v = v * (HEAD_DIM ** -0.5)
        elif post == 'sigmoid':
            v = _sigmoid(v)
        ref[0] = v.astype(ref.dtype)


def _in_proj(x, mod, norm_attn, wn, wt, tm, outs, dtypes):
    bsz, t, _ = x.shape
    r = mod.shape[2]
    rb = 1 if r == 1 else tm
    mod_map = (lambda b, i: (b, 0, 0, 0)) if r == 1 else (lambda b, i: (b, 0, i, 0))
    out_specs, out_shape = [], []
    for (kind, lo, hi, _), dt in zip(outs, dtypes):
        if kind == 'n':
            out_specs.append(pl.BlockSpec((1, tm, hi - lo), lambda b, i: (b, i, 0)))
            out_shape.append(jax.ShapeDtypeStruct((bsz, t, hi - lo), dt))
        else:
            out_specs.append(pl.BlockSpec((1, hi - lo, tm), lambda b, i: (b, 0, i)))
            out_shape.append(jax.ShapeDtypeStruct((bsz, hi - lo, t), dt))
    return pl.pallas_call(
        functools.partial(_inproj_kernel, outs=tuple(outs)),
        grid=(bsz, t // tm),
        in_specs=[pl.BlockSpec((1, tm, D_MODEL), lambda b, i: (b, i, 0)),
                  pl.BlockSpec((1, 6, rb, D_MODEL), mod_map),
                  pl.BlockSpec((1, D_MODEL), lambda b, i: (0, 0)),
                  pl.BlockSpec(wn.shape, lambda b, i: (0, 0)),
                  pl.BlockSpec(wt.shape, lambda b, i: (0, 0))],
        out_specs=out_specs,
        out_shape=out_shape,
        compiler_params=_params("parallel", "parallel"),
        name="in_proj",
    )(x, mod, norm_attn.reshape(1, -1), wn, wt)


def _ssm_tables(lam_re, lam_im, log_dt, b_re, b_im, c_re, c_im):
    hp = dict(precision=_HIGHEST)
    lr, li = lam_re.astype(_F32), lam_im.astype(_F32)
    dt = jnp.exp(log_dt.astype(_F32))[:, None]
    ar, ai = lr * dt, li * dt

    def power(k):
        mag = jnp.exp(ar * k)
        return mag * jnp.cos(ai * k), mag * jnp.sin(ai * k)

    n = SSM_CHUNK
    steps = jnp.arange(n + 1, dtype=_F32)[:, None, None]
    pwr, pwi = power(steps)
    lbr, lbi = pwr[1], pwi[1]
    den = lr * lr + li * li
    fr = ((lbr - 1.0) * lr + lbi * li) / den
    fi = (lbi * lr - (lbr - 1.0) * li) / den
    br_, bi_ = b_re.astype(_F32), b_im.astype(_F32)
    bbr = fr[:, :, None] * br_ - fi[:, :, None] * bi_
    bbi = fr[:, :, None] * bi_ + fi[:, :, None] * br_
    cr, ci = c_re.astype(_F32), c_im.astype(_F32)
    er = pwr[:, :, :, None] * bbr[None] - pwi[:, :, :, None] * bbi[None]
    ei = pwr[:, :, :, None] * bbi[None] + pwi[:, :, :, None] * bbr[None]
    kern = (jnp.einsum('gip,dgpj->dgij', cr, er[:n], **hp)
            - jnp.einsum('gip,dgpj->dgij', ci, ei[:n], **hp))
    nb, gpb = N_SSM_GROUPS // SSM_LANE_GROUPS, SSM_LANE_GROUPS

    def expand(a, g_axis):
        w = a.shape[-1]
        owner = jnp.arange(gpb * w) // w
        own = jnp.arange(gpb).reshape((gpb,) + (1,) * (a.ndim - 1 - g_axis))
        return jnp.where(owner == own, jnp.tile(a, (1,) * (a.ndim - 1) + (gpb,)), 0.0)

    lag_tiles = kern.reshape(n, nb, gpb, SSM_GROUP, SSM_GROUP).transpose(1, 0, 2, 4, 3)
    lag_tiles = expand(lag_tiles, 2).reshape(nb, n, LANES, LANES)
    rev = n - 1 - jnp.arange(n)

    def state_in(e):
        w = e[rev].reshape(n, nb, gpb, SSM_STATE, SSM_GROUP).transpose(1, 0, 2, 4, 3)
        return expand(w, 2).reshape(nb, n * LANES, gpb * SSM_STATE)

    w_in = jnp.concatenate([state_in(er), state_in(ei)], axis=-1)
    cvr = cr[None] * pwr[1:, :, None, :] - ci[None] * pwi[1:, :, None, :]
    cvi = cr[None] * pwi[1:, :, None, :] + ci[None] * pwr[1:, :, None, :]

    def state_out(cv):
        v = cv.reshape(n, nb, gpb, SSM_GROUP, SSM_STATE).transpose(1, 0, 2, 4, 3)
        return expand(v, 2).reshape(nb, n, gpb * SSM_STATE, LANES)

    v_out = jnp.concatenate([state_out(cvr), -state_out(cvi)], axis=2)
    a_chunk = jnp.stack([pwr[n].reshape(nb, -1), pwi[n].reshape(nb, -1)], axis=1)
    eye = jnp.eye(N_SSM_GROUPS, dtype=_F32)
    ns = N_SSM_GROUPS * SSM_STATE
    bd = lambda a: jnp.einsum('gpi,gh->gihp', a, eye).reshape(D_SSM, ns)
    cd = lambda a: jnp.einsum('gip,gh->gphi', a, eye).reshape(ns, D_SSM)
    return dict(toep=lag_tiles.astype(_BF16), w_in=w_in.astype(_BF16), v_out=v_out.astype(_BF16), a_chunk=a_chunk,
                bd_re=bd(bbr).astype(_BF16), bd_im=bd(bbi).astype(_BF16),
                cd_re=cd(cr).astype(_BF16), cd_im=cd(ci).astype(_BF16),
                lam_re=lbr.reshape(1, ns), lam_im=lbi.reshape(1, ns))


def _ssm_prompt_kernel(u_ref, toep_ref, w_ref, v_ref, a_ref, y_ref, s_ref, x_scr, sin_scr, *, n_chunks):
    n = SSM_CHUNK
    lhs = jnp.concatenate([u_ref[0, pl.ds(tau, n_chunks, stride=n), :] for tau in range(n)],
                          axis=1).astype(_BF16)
    x_scr[...] = jnp.dot(lhs, w_ref[0], preferred_element_type=_F32)
    half = x_scr.shape[1] // 2
    a_re = a_ref[0, 0:1, :]
    a_im = a_ref[0, 1:2, :]

    def step(c, carry):
        s_re, s_im = carry
        row = pl.ds(c, 1)
        sin_scr[row, :half] = s_re
        sin_scr[row, half:] = s_im
        x = x_scr[row, :]
        return (a_re * s_re - a_im * s_im + x[:, :half],
                a_re * s_im + a_im * s_re + x[:, half:])

    zero = jnp.zeros((1, half), _F32)
    s_re, s_im = lax.fori_loop(0, n_chunks, step, (zero, zero))
    s_ref[0, 0] = jnp.concatenate([s_re, s_im], axis=1)
    sin_b = sin_scr[...].astype(_BF16)
    zero_tile = jnp.zeros((LANES, LANES), _BF16)
    for t2 in range(n // 2):
        t_lo = 2 * t2
        k_hi = (t_lo + 2) * LANES
        w_lag = jnp.concatenate(
            [jnp.concatenate([toep_ref[0, t_lo - tau] if tau <= t_lo else zero_tile, toep_ref[0, t_lo + 1 - tau]],
                             axis=1) for tau in range(t_lo + 2)], axis=0)
        w_state = jnp.concatenate([v_ref[0, t_lo], v_ref[0, t_lo + 1]], axis=1)
        y = (jnp.dot(lhs[:, :k_hi], w_lag, preferred_element_type=_F32)
             + jnp.dot(sin_b, w_state, preferred_element_type=_F32))
        for d in range(2):
            y_ref[0, pl.ds(t_lo + d, n_chunks, stride=n), :] = y[:, d * LANES:(d + 1) * LANES]


def _ssm_prompt(u, tabs):
    bsz, t, _ = u.shape
    nch = t // SSM_CHUNK
    nb = D_SSM // LANES
    sw = 2 * SSM_LANE_GROUPS * SSM_STATE
    table = lambda a: pl.BlockSpec((1,) + a.shape[1:], lambda m, b: (m,) + (0,) * (a.ndim - 1),
                                   pipeline_mode=pl.Buffered(1))
    tok = pl.BlockSpec((1, t, LANES), lambda m, b: (b, 0, m))
    y, s = pl.pallas_call(
        functools.partial(_ssm_prompt_kernel, n_chunks=nch),
        grid=(nb, bsz),
        in_specs=[tok, table(tabs['toep']), table(tabs['w_in']), table(tabs['v_out']),
                  pl.BlockSpec((1, 2, sw // 2), lambda m, b: (m, 0, 0))],
        out_specs=[tok, pl.BlockSpec((1, 1, 1, sw), lambda m, b: (b, m, 0, 0))],
        out_shape=[jax.ShapeDtypeStruct((bsz, t, D_SSM), _F32),
                   jax.ShapeDtypeStruct((bsz, nb, 1, sw), _F32)],
        scratch_shapes=[pltpu.VMEM((nch, sw), _F32), pltpu.VMEM((nch, sw), _F32)],
        compiler_params=_params("parallel", "parallel"),
        name="ssm_prompt",
    )(u, tabs['toep'], tabs['w_in'], tabs['v_out'], tabs['a_chunk'])
    state = lambda a: a.reshape(bsz, N_SSM_GROUPS, SSM_STATE)
    return y, state(s[:, :, 0, :sw // 2]), state(s[:, :, 0, sw // 2:])


def _ssm_step_kernel(u_ref, s0re_ref, s0im_ref, lre_ref, lim_ref, bdre_ref, bdim_ref, cdre_ref, cdim_ref,
                     y_ref, sre_ref, sim_ref):
    u = u_ref[...]
    s_re, s_im = s0re_ref[...], s0im_ref[...]
    l_re, l_im = lre_ref[...], lim_ref[...]
    n_re = l_re * s_re - l_im * s_im + _bdot(u, bdre_ref[...])
    n_im = l_re * s_im + l_im * s_re + _bdot(u, bdim_ref[...])
    sre_ref[...] = n_re
    sim_ref[...] = n_im
    y_ref[...] = _bdot(n_re, cdre_ref[...]) - _bdot(n_im, cdim_ref[...])


def _ssm_step(u, s0_re, s0_im, tabs):
    n = u.shape[0]
    ns = N_SSM_GROUPS * SSM_STATE
    return pl.pallas_call(
        _ssm_step_kernel,
        out_shape=[jax.ShapeDtypeStruct((n, D_SSM), _F32),
                   jax.ShapeDtypeStruct((n, ns), _F32),
                   jax.ShapeDtypeStruct((n, ns), _F32)],
        compiler_params=pltpu.CompilerParams(vmem_limit_bytes=VMEM_LIMIT),
        name="ssm_step",
    )(u, s0_re, s0_im, tabs['lam_re'], tabs['lam_im'], tabs['bd_re'], tabs['bd_im'], tabs['cd_re'], tabs['cd_im'])


def _compress_tables(pe_k, w1_k, w2_k, pe_v, w1_v, w2_v):
    zeros = jnp.zeros((CMP_STRIDE, HEAD_DIM, CMP_HIDDEN), _F32)
    cols = []
    for half in range(2):
        for slot in range(KV_SLOTS):
            w1 = (w1_k if slot < N_KV_HEADS else w1_v)[half * CMP_STRIDE:(half + 1) * CMP_STRIDE]
            blk = jnp.stack([w1 if s == slot else zeros for s in range(KV_SLOTS)], axis=1)
            cols.append(blk.reshape(CMP_STRIDE * KV_ROW, CMP_HIDDEN))
    w_all = jnp.concatenate(cols, axis=1).astype(_BF16)
    z2 = jnp.zeros((CMP_HIDDEN, HEAD_DIM), _F32)
    w2 = jnp.concatenate(
        [jnp.concatenate([(w2_k if s < N_KV_HEADS else w2_v) if s == slot else z2 for s in range(KV_SLOTS)], axis=1)
         for slot in range(KV_SLOTS)], axis=0).astype(_BF16)
    pe = jnp.stack([pe_k.reshape(-1), pe_v.reshape(-1)], axis=0)
    pe = jnp.concatenate([pe, jnp.zeros((SUBLANES - 2, pe.shape[1]), _F32)], axis=0)
    w1f = jnp.stack([w1_k.reshape(-1, CMP_HIDDEN), w1_v.reshape(-1, CMP_HIDDEN)], axis=0)
    halves = lambda w1: jnp.concatenate([w1[:CMP_STRIDE].reshape(-1, CMP_HIDDEN),
                                         w1[CMP_STRIDE:].reshape(-1, CMP_HIDDEN)], axis=1)
    w_slot = jnp.stack([halves(w1_k if s < N_KV_HEADS else w1_v) for s in range(KV_SLOTS)], axis=0).astype(_BF16)
    return dict(w_all=w_all, w_slot=w_slot, tail=(w2, w2.T, pe, w1f))


def _compress_tail(h, w2_ref, w2t_ref, pe_ref, w1f_ref, n_rows):
    hw = KV_SLOTS * CMP_HIDDEN
    pe = pe_ref[...]
    bias_k = jnp.dot(pe, w1f_ref[0], precision=_HIGHEST, preferred_element_type=_F32)[0:1]
    bias_v = jnp.dot(pe, w1f_ref[1], precision=_HIGHEST, preferred_element_type=_F32)[1:2]
    bias = jnp.concatenate([bias_k, bias_k, bias_v, bias_v], axis=1)
    nxt = pltpu.roll(h[:, hw:], n_rows - 1, 0)
    hid = _gelu_tanh(h[:, :hw] + nxt + bias)
    row = lax.broadcasted_iota(jnp.int32, hid.shape, 0)
    hid = jnp.where(row < n_rows - 1, hid, 0.0).astype(_BF16)
    return (jnp.dot(hid, w2_ref[...], preferred_element_type=_F32),
            lax.dot_general(w2t_ref[...], hid, _NT, preferred_element_type=_F32))


def _store_compressed(out, out_t, o_ref, ot_ref):
    for s in range(KV_SLOTS):
        o_ref[0, s] = out[:, s * HEAD_DIM:(s + 1) * HEAD_DIM]
        ot_ref[0, s] = out_t[s * HEAD_DIM:(s + 1) * HEAD_DIM, :]


def _split3_keys(k):
    hi = k.astype(_BF16)
    lo = (k - hi.astype(_F32)).astype(_BF16)
    return jnp.concatenate([hi, lo, hi, jnp.zeros_like(hi)], axis=1)


def _split3_queries(q_t):
    hi = q_t.astype(_BF16)
    lo = (q_t - hi.astype(_F32)).astype(_BF16)
    return jnp.concatenate([hi, hi, lo, jnp.zeros_like(hi)], axis=0)


def _compress_prompt_kernel(x_ref, wall_ref, w2_ref, w2t_ref, pe_ref, w1f_ref, k3_ref, ot_ref, *, n_rows):
    h = jnp.dot(x_ref[0].astype(_BF16), wall_ref[...], preferred_element_type=_F32)
    out, out_t = _compress_tail(h, w2_ref, w2t_ref, pe_ref, w1f_ref, n_rows)
    for g in range(N_KV_HEADS):
        k3_ref[0, g] = _split3_keys(out[:, g * HEAD_DIM:(g + 1) * HEAD_DIM])
    for s in range(KV_SLOTS):
        ot_ref[0, s] = out_t[s * HEAD_DIM:(s + 1) * HEAD_DIM, :]


def _compress_out(nseq, nc, index_map, index_map_t):
    specs = [pl.BlockSpec((1, KV_SLOTS, nc, HEAD_DIM), index_map),
             pl.BlockSpec((1, KV_SLOTS, HEAD_DIM, nc), index_map_t)]
    shapes = [jax.ShapeDtypeStruct((nseq, KV_SLOTS, nc, HEAD_DIM), _F32),
              jax.ShapeDtypeStruct((nseq, KV_SLOTS, HEAD_DIM, nc), _F32)]
    return specs, shapes


def _compress_prompt(kv_cmp, ctabs):
    bsz, t, _ = kv_cmp.shape
    nc = t // CMP_STRIDE
    const = lambda a: pl.BlockSpec(a.shape, lambda b: (0,) * a.ndim)
    weights = (ctabs['w_all'],) + ctabs['tail']
    return pl.pallas_call(
        functools.partial(_compress_prompt_kernel, n_rows=nc),
        grid=(bsz,),
        in_specs=[pl.BlockSpec((1, nc, CMP_STRIDE * KV_ROW), lambda b: (b, 0, 0))] + [const(a) for a in weights],
        out_specs=[pl.BlockSpec((1, N_KV_HEADS, nc, KV_SLOTS * HEAD_DIM), lambda b: (b, 0, 0, 0)),
                   pl.BlockSpec((1, KV_SLOTS, HEAD_DIM, nc), lambda b: (b, 0, 0, 0))],
        out_shape=[jax.ShapeDtypeStruct((bsz, N_KV_HEADS, nc, KV_SLOTS * HEAD_DIM), _BF16),
                   jax.ShapeDtypeStruct((bsz, KV_SLOTS, HEAD_DIM, nc), _F32)],
        compiler_params=_params("parallel"),
        name="compress_prompt",
    )(kv_cmp.reshape(bsz, nc, CMP_STRIDE * KV_ROW), *weights)


def _page_copies(pt_ref, cache_hbm, buf, sem, seq, slot, n_pages):
    return [pltpu.make_async_copy(cache_hbm.at[pt_ref[seq, j]],
                                  buf.at[slot, :, pl.ds(j * PAGE_SIZE, PAGE_SIZE)],
                                  sem.at[slot]) for j in range(n_pages)]


def _gather_pages(pt_ref, cache_hbm, buf, sem, n_pages):
    b = pl.program_id(0)
    slot = lax.rem(b, 2)

    @pl.when(b == 0)
    def _():
        for cp in _page_copies(pt_ref, cache_hbm, buf, sem, 0, 0, n_pages):
            cp.start()

    @pl.when(b + 1 < pl.num_programs(0))
    def _():
        for cp in _page_copies(pt_ref, cache_hbm, buf, sem, b + 1, 1 - slot, n_pages):
            cp.start()

    for cp in _page_copies(pt_ref, cache_hbm, buf, sem, b, slot, n_pages):
        cp.wait()
    return slot


def _chunk_permutation():
    span = 2 * PAGE_SIZE
    r = jnp.arange(span)
    src = (r % (span // CMP_STRIDE)) * CMP_STRIDE + r // (span // CMP_STRIDE)
    return (src[:, None] == jnp.arange(span)[None, :]).astype(_BF16)


def _compress_paged_kernel(pt_ref, cache_hbm, perm_ref, wslot_ref, w2_ref, w2t_ref, pe_ref, w1f_ref,
                           o_ref, ot_ref, buf, sem, xr, *, n_pages, n_rows):
    slot = _gather_pages(pt_ref, cache_hbm, buf, sem, n_pages)
    span = 2 * PAGE_SIZE
    cps = span // CMP_STRIDE
    low = lax.broadcasted_iota(jnp.int32, (cps, LANES), 1) < HEAD_DIM

    def relayout(i, carry):
        xt = buf[slot, :, pl.ds(pl.multiple_of(i * span, span), span)].astype(_BF16)
        rows = lax.dot_general(perm_ref[...], xt, _NT, preferred_element_type=_F32)
        chunk_rows = pl.ds(pl.multiple_of(i * cps, cps), cps)
        for a in range(CMP_STRIDE // 2):
            for s in range(KV_SLOTS):
                tile = slice((s // 2) * LANES, (s // 2 + 1) * LANES)
                even = rows[2 * a * cps:(2 * a + 1) * cps, tile]
                odd = rows[(2 * a + 1) * cps:(2 * a + 2) * cps, tile]
                if s % 2 == 0:
                    pair = jnp.where(low, even, pltpu.roll(odd, HEAD_DIM, 1))
                else:
                    pair = jnp.where(low, pltpu.roll(even, HEAD_DIM, 1), odd)
                xr[s, chunk_rows, a * LANES:(a + 1) * LANES] = pair.astype(_BF16)
        return carry

    lax.fori_loop(0, n_pages // 2, relayout, 0, unroll=math.gcd(RELAYOUT_UNROLL, n_pages // 2))
    hs = [jnp.dot(xr[s], wslot_ref[s], preferred_element_type=_F32) for s in range(KV_SLOTS)]
    h = jnp.concatenate([v[:, :CMP_HIDDEN] for v in hs] + [v[:, CMP_HIDDEN:] for v in hs], axis=1)
    out, out_t = _compress_tail(h, w2_ref, w2t_ref, pe_ref, w1f_ref, n_rows)
    _store_compressed(out, out_t, o_ref, ot_ref)


def _compress_paged(cache_t, page_table, ctabs):
    nseq, n_pages = page_table.shape
    past = n_pages * PAGE_SIZE
    nc = past // CMP_STRIDE
    perm = _chunk_permutation()
    const = lambda a: pl.BlockSpec(a.shape, lambda b, pt: (0,) * a.ndim)
    out_specs, out_shape = _compress_out(nseq, nc, lambda b, pt: (b, 0, 0, 0), lambda b, pt: (b, 0, 0, 0))
    weights = (ctabs['w_slot'],) + ctabs['tail']
    grid_spec = pltpu.PrefetchScalarGridSpec(
        num_scalar_prefetch=1,
        grid=(nseq,),
        in_specs=[pl.BlockSpec(memory_space=pl.ANY), const(perm)] + [const(a) for a in weights],
        out_specs=out_specs,
        scratch_shapes=[pltpu.VMEM((2, KV_ROW, past), _F32), pltpu.SemaphoreType.DMA((2,)),
                        pltpu.VMEM((KV_SLOTS, nc, CMP_STRIDE * HEAD_DIM), _BF16)],
    )
    return pl.pallas_call(
        functools.partial(_compress_paged_kernel, n_pages=n_pages, n_rows=nc),
        grid_spec=grid_spec,
        out_shape=out_shape,
        compiler_params=_params("arbitrary"),
        name="compress_paged",
    )(page_table, cache_t, perm, *weights)


def _importance_map(n_cmp_rows, n_blocks_rows, n_cmp):
    ratio = SLC_BLOCK // CMP_STRIDE
    j = jnp.arange(n_blocks_rows)[:, None]
    n = jnp.arange(n_cmp_rows)[None, :]
    off = n - ratio * j
    w = jnp.where((off == -1) | (off == ratio - 1), 1.0, jnp.where((off >= 0) & (off < ratio - 1), 2.0, 0.0))
    return jnp.where(n < n_cmp, w, 0.0).astype(_F32)


def _softmax(s, ok, axis):
    s = jnp.where(ok, s, NEG_INF)
    m = jnp.max(s, axis=axis, keepdims=True)
    m = jnp.where(m == NEG_INF, 0.0, m)
    e = jnp.exp(s - m)
    den = jnp.sum(e, axis=axis, keepdims=True)
    return e / jnp.where(den > 0, den, 1.0)


def _nsa_prompt_kernel(q_ref, gt_ref, kc_ref, vct_ref, ksn_ref, vst_ref, kwn_ref, vwt_ref, mt_ref, o_ref,
                       bias_ref, s_a, s_b, *, tq, kc_len, n_cmp, n_blk):
    g = pl.program_id(1)
    t0 = pl.program_id(2) * tq
    cols = GROUP_SIZE * tq
    q_t = q_ref[0]
    q4 = jnp.concatenate([q_t[r * HEAD_DIM:(r + 1) * HEAD_DIM, :] for r in range(GROUP_SIZE)], axis=1)
    q4b = (q4 * LOG2_E).astype(_BF16)
    feat = lax.broadcasted_iota(jnp.int32, (KV_ROW, cols), 0)
    qp = jnp.where(lax.shift_right_logical(feat, int(math.log2(HEAD_DIM))) == g,
                   jnp.concatenate([q4b] * KV_SLOTS, axis=0), jnp.zeros((), _BF16))
    lane = lax.broadcasted_iota(jnp.int32, (1, cols), 1)
    qpos = t0 + (lane & (tq - 1))

    ncr = kc_ref.shape[2]
    s_c = jnp.dot(kc_ref[0, 0], _split3_queries(q4), preferred_element_type=_F32)
    n_idx = lax.broadcasted_iota(jnp.int32, (ncr, 1), 0)
    cmp_end = jnp.where(n_idx < n_cmp, n_idx * CMP_STRIDE + (CMP_BLOCK - 1), jnp.int32(2 ** 30))
    p_c = _softmax(s_c, cmp_end <= qpos, 0)
    o_c = _bdot(vct_ref[0, 0], p_c)

    imp = p_c[:, 0:tq]
    for r in range(1, GROUP_SIZE):
        imp = imp + p_c[:, r * tq:(r + 1) * tq]
    imp_hi = imp.astype(_BF16)
    imp_r = imp - imp_hi.astype(_F32)
    imp_mid = imp_r.astype(_BF16)
    imp_lo = (imp_r - imp_mid.astype(_F32)).astype(_BF16)
    imp_s = jnp.dot(mt_ref[...], jnp.concatenate([imp_hi, imp_mid, imp_lo], axis=0),
                    preferred_element_type=_F32)
    qrow = qpos[:, :tq]
    j_idx = lax.broadcasted_iota(jnp.int32, (n_blk, tq), 0)
    valid = j_idx * SLC_BLOCK <= qrow
    cur = lax.shift_right_arithmetic(qrow, int(math.log2(SLC_BLOCK)))
    forced = (j_idx == 0) | (valid & (j_idx > cur - N_LOCAL))
    score = jnp.where(forced, FORCED_SCORE, jnp.where(valid, imp_s, -1.0))
    rank = jnp.zeros((n_blk, tq), jnp.int32)
    row8 = lax.broadcasted_iota(jnp.int32, (SUBLANES, tq), 0)
    for j in range(n_blk):
        sj = score[j:j + 1, :]
        lo, hi = j // SUBLANES * SUBLANES, (j // SUBLANES + 1) * SUBLANES
        own = score[lo:hi]
        parts = [jnp.where(row8 > j - lo, jnp.where(sj >= own, 1, 0), jnp.where(sj > own, 1, 0))]
        if lo > 0:
            parts.insert(0, jnp.where(sj > score[:lo], 1, 0))
        if hi < n_blk:
            parts.append(jnp.where(sj >= score[hi:], 1, 0))
        rank = rank + jnp.concatenate(parts, axis=0)
    bias = jnp.where(valid & (rank < N_SELECT), 0.0, NEG_INF)
    bias_ref[...] = jnp.concatenate([bias] * GROUP_SIZE, axis=1)

    bpc = kc_len // SLC_BLOCK

    def scores(c):
        k0 = pl.multiple_of(c * kc_len, kc_len)
        s = jnp.dot(ksn_ref[0, pl.ds(k0, kc_len), :], qp, preferred_element_type=_F32)
        b8 = bias_ref[pl.ds(pl.multiple_of(c * bpc, bpc), bpc), :]
        s = jnp.concatenate([s[i * SLC_BLOCK:(i + 1) * SLC_BLOCK, :] + b8[i:i + 1, :] for i in range(bpc)], axis=0)
        kpos = k0 + lax.broadcasted_iota(jnp.int32, (kc_len, 1), 0)
        return jnp.where(kpos <= qpos, s, NEG_INF)

    def weighted(s, m, vals_t):
        ones = jnp.ones((2 * SUBLANES, vals_t.shape[1]), _BF16)
        return jnp.dot(jnp.concatenate([vals_t, ones], axis=0), jnp.exp2(s - m).astype(_BF16),
                       preferred_element_type=_F32)

    def fold_in(c, s, m_acc, acc):
        k0 = pl.multiple_of(c * kc_len, kc_len)
        m_new = jnp.maximum(m_acc, jnp.max(s, axis=0, keepdims=True))
        return m_new, jnp.exp2(m_acc - m_new) * acc + weighted(s, m_new, vst_ref[0, :, pl.ds(k0, kc_len)])

    def step(c, s_new_ref, s_old_ref, m_acc, acc):
        s_new_ref[...] = scores(c)
        return fold_in(c - 1, s_old_ref[...], m_acc, acc)

    def pair(p, carry):
        c = 2 * p + 1
        return step(c + 1, s_a, s_b, *step(c, s_b, s_a, *carry))

    c_diag = t0 // kc_len
    s_a[...] = scores(0)
    init = (jnp.full((1, cols), NEG_INF, _F32), jnp.zeros((HEAD_DIM + 2 * SUBLANES, cols), _F32))
    m_acc, acc_s = lax.fori_loop(0, c_diag // 2, pair, init)

    def odd_tail(m_acc, acc):
        m_acc, acc = step(c_diag, s_b, s_a, m_acc, acc)
        return fold_in(c_diag, s_b[...], m_acc, acc)[1]

    def even_tail(m_acc, acc):
        return fold_in(c_diag, s_a[...], m_acc, acc)[1]

    acc_s = lax.cond(c_diag % 2 == 1, odd_tail, even_tail, m_acc, acc_s)
    o_s = acc_s[:HEAD_DIM] / acc_s[HEAD_DIM:HEAD_DIM + 1]

    wk = tq + WINDOW
    w0 = pl.multiple_of(jnp.maximum(t0 - WINDOW, 0), tq)
    dpos = qpos - (w0 + lax.broadcasted_iota(jnp.int32, (wk, 1), 0))
    s_w = jnp.where(jnp.where(dpos >= 0, dpos, WINDOW) < WINDOW,
                    jnp.dot(kwn_ref[0, pl.ds(w0, wk), :], qp, preferred_element_type=_F32), NEG_INF)
    acc_w = weighted(s_w, jnp.max(s_w, axis=0, keepdims=True), vwt_ref[0, :, pl.ds(w0, wk)])
    o_w = acc_w[:HEAD_DIM] / acc_w[HEAD_DIM:HEAD_DIM + 1]

    outs = []
    for r in range(GROUP_SIZE):
        sl = slice(r * tq, (r + 1) * tq)
        gate = lambda branch: gt_ref[0, pl.ds(branch * N_HEADS + g * GROUP_SIZE + r, 1), :]
        outs.append(gate(0) * o_c[:, sl] + gate(1) * o_s[:, sl] + gate(2) * o_w[:, sl])
    o_ref[0] = jnp.concatenate(outs, axis=0).T


def _nsa_prompt(q_t, gates_t, kcmp3, kcmp_t, ks_n, ks_tb, kw_n, kw_tb, tq, kc_len):
    bsz, _, t = q_t.shape
    ncr = kcmp3.shape[2]
    n_blk = t // SLC_BLOCK
    mt = jnp.tile(_importance_map(ncr, n_blk, ncr - 1), (1, 3)).astype(_BF16)
    rows_n = pl.BlockSpec((1, t, KV_ROW), lambda b, g, i: (b, 0, 0))
    vals_t = pl.BlockSpec((1, HEAD_DIM, t), lambda b, g, i: (b, N_KV_HEADS + g, 0))
    qw = GROUP_SIZE * HEAD_DIM
    return pl.pallas_call(
        functools.partial(_nsa_prompt_kernel, tq=tq, kc_len=kc_len, n_cmp=ncr - 1, n_blk=n_blk),
        grid=(bsz, N_KV_HEADS, t // tq),
        in_specs=[pl.BlockSpec((1, qw, tq), lambda b, g, i: (b, g, i)),
                  pl.BlockSpec((1, GATE_ROWS, tq), lambda b, g, i: (b, 0, i)),
                  pl.BlockSpec((1, 1, ncr, KV_SLOTS * HEAD_DIM), lambda b, g, i: (b, g, 0, 0)),
                  pl.BlockSpec((1, 1, HEAD_DIM, ncr), lambda b, g, i: (b, N_KV_HEADS + g, 0, 0)),
                  rows_n, vals_t, rows_n, vals_t,
                  pl.BlockSpec(mt.shape, lambda b, g, i: (0, 0))],
        out_specs=pl.BlockSpec((1, tq, qw), lambda b, g, i: (b, i, g)),
        out_shape=jax.ShapeDtypeStruct((bsz, t, D_ATTN), _F32),
        scratch_shapes=[pltpu.VMEM((n_blk, GROUP_SIZE * tq), _F32),
                        pltpu.VMEM((kc_len, GROUP_SIZE * tq), _F32), pltpu.VMEM((kc_len, GROUP_SIZE * tq), _F32)],
        compiler_params=_params("parallel", "parallel", "parallel"),
        name="nsa_prompt",
    )(q_t, gates_t, kcmp3, kcmp_t, ks_n, ks_tb, kw_n, kw_tb, mt)


def _nsa_sample_kernel(pt_ref, q_ref, gt_ref, kcmp_ref, kcmpt_ref, slc_hbm, ksn_ref, win_ref, kwt_ref,
                       m_ref, e_ref, o_ref, nw_ref, buf, sem,
                       *, n_pages, past, n_cmp, blk_pad, n_buf):
    b = pl.program_id(0)
    slot = _gather_pages(pt_ref, slc_hbm, buf, sem, n_pages)
    q = q_ref[0]
    gt = gt_ref[0]
    ks_new = ksn_ref[0]
    nseq = kwt_ref.shape[1]
    seq_lane = lax.broadcasted_iota(jnp.int32, (KV_ROW, nseq), 1)
    kw_col = jnp.sum(jnp.where(seq_lane == b, kwt_ref[...], 0.0), axis=1, keepdims=True)
    win = win_ref[0]
    pos = lax.broadcasted_iota(jnp.int32, win.shape, 1)
    band = jnp.where(pos == n_buf - 1, kw_col, pltpu.roll(win, n_buf - 1, 1))
    nw_ref[0] = band
    band_b = band.astype(_BF16)
    ncr = kcmp_ref.shape[2]
    lane = lax.broadcasted_iota(jnp.int32, (SUBLANES, LANES), 1)
    hrow = lax.broadcasted_iota(jnp.int32, (SUBLANES, LANES), 0)
    q8s, o_cs, sels = [], [], []
    for g in range(N_KV_HEADS):
        q8 = jnp.concatenate([q[:, (g * GROUP_SIZE + r) * HEAD_DIM:(g * GROUP_SIZE + r + 1) * HEAD_DIM]
                              for r in range(GROUP_SIZE)]
                             + [jnp.zeros((SUBLANES - GROUP_SIZE, HEAD_DIM), _F32)], axis=0)
        q8s.append(q8)
        s_c = lax.dot_general(q8, kcmp_ref[0, g], _NT, precision=_HIGHEST, preferred_element_type=_F32)
        n_idx = lax.broadcasted_iota(jnp.int32, (SUBLANES, ncr), 1)
        cmp_end = jnp.where(n_idx < n_cmp, n_idx * CMP_STRIDE + (CMP_BLOCK - 1), jnp.int32(2 ** 30))
        p_c = _softmax(s_c, cmp_end <= past, 1)
        o_cs.append(_bdot_nt(p_c, kcmpt_ref[0, N_KV_HEADS + g]))

        imp = jnp.sum(p_c[:GROUP_SIZE], axis=0, keepdims=True)
        imp_s = jnp.dot(jnp.broadcast_to(imp, (SUBLANES, ncr)), m_ref[...], precision=_HIGHEST,
                        preferred_element_type=_F32)[0:1]
        j_idx = lax.broadcasted_iota(jnp.int32, (1, blk_pad), 1)
        valid = j_idx * SLC_BLOCK <= past
        forced = (j_idx == 0) | (valid & (j_idx > past // SLC_BLOCK - N_LOCAL))
        score = jnp.where(forced, FORCED_SCORE, jnp.where(valid, imp_s, -1.0))
        s_k = jnp.broadcast_to(score, (blk_pad, blk_pad))
        s_j = s_k.T
        jj = lax.broadcasted_iota(jnp.int32, (blk_pad, blk_pad), 0)
        kk = lax.broadcasted_iota(jnp.int32, (blk_pad, blk_pad), 1)
        beats = jnp.where(kk > jj, jnp.where(s_j >= s_k, 1.0, 0.0), jnp.where(s_j > s_k, 1.0, 0.0))
        rank = jnp.sum(beats, axis=0, keepdims=True)
        sels.append(jnp.where(valid & (rank < N_SELECT), 1.0, 0.0))
    row_g = lax.broadcasted_iota(jnp.int32, (SUBLANES, blk_pad), 0)
    sel_rows = jnp.zeros((SUBLANES, blk_pad), _F32)
    for g in range(N_KV_HEADS):
        sel_rows = jnp.where(row_g == g, sels[g], sel_rows)
    picked = jnp.dot(sel_rows.astype(_BF16), e_ref[...], preferred_element_type=_F32)

    heads = []
    for g in range(N_KV_HEADS):
        q8 = q8s[g]
        q8b = q8.astype(_BF16)
        k_rows = slice(g * HEAD_DIM, (g + 1) * HEAD_DIM)
        v_rows = slice((N_KV_HEADS + g) * HEAD_DIM, (N_KV_HEADS + g + 1) * HEAD_DIM)
        s_s = jnp.where(picked[g:g + 1, :] > 0.5, _bdot(q8b, buf[slot, k_rows, :]), NEG_INF)
        s_n = jnp.sum(q8 * ks_new[:, k_rows], axis=-1, keepdims=True)
        m = jnp.maximum(jnp.max(s_s, axis=-1, keepdims=True), s_n)
        e_p, e_n = jnp.exp(s_s - m), jnp.exp(s_n - m)
        num = _bdot_nt(e_p, buf[slot, v_rows, :])
        o_s = (num + e_n * ks_new[:, v_rows]) / (jnp.sum(e_p, axis=-1, keepdims=True) + e_n)

        s_w = jnp.dot(q8b, band_b[k_rows, :], preferred_element_type=_F32)
        i_idx = lax.broadcasted_iota(jnp.int32, (SUBLANES, n_buf), 1)
        p_w = _softmax(s_w, (n_buf - 1 - i_idx) < WINDOW, 1)
        o_w = _bdot_nt(p_w, band_b[v_rows, :])

        def gate(branch):
            tgt = branch * N_HEADS + g * GROUP_SIZE + hrow
            return jnp.sum(jnp.where(lane == tgt, jnp.broadcast_to(gt, (SUBLANES, LANES)), 0.0),
                           axis=-1, keepdims=True)

        o = gate(0) * o_cs[g] + gate(1) * o_s + gate(2) * o_w
        heads += [o[r:r + 1, :] for r in range(GROUP_SIZE)]
    o_ref[0] = jnp.concatenate(heads, axis=1)


def _expand_table(past, blk_pad):
    return (jnp.arange(past)[None, :] // SLC_BLOCK == jnp.arange(blk_pad)[:, None]).astype(_BF16)


def _nsa_sample(q, gates, kcmp, kcmp_t, cache_t, page_table, ks_new, win_t, kw_t):
    nseq, n_pages = page_table.shape
    past = n_pages * PAGE_SIZE
    ncr = kcmp.shape[2]
    n_buf = win_t.shape[2]
    n_blk = past // SLC_BLOCK + 1
    blk_pad = -(-n_blk // LANES) * LANES
    m_tab = _importance_map(ncr, blk_pad, ncr - 1).T
    e = _expand_table(past, blk_pad)
    row = lambda w: pl.BlockSpec((1, 1, w), lambda b, pt: (b, 0, 0))
    const = lambda a: pl.BlockSpec(a.shape, lambda b, pt: (0,) * a.ndim)
    win_spec = pl.BlockSpec((1, KV_ROW, n_buf), lambda b, pt: (b, 0, 0))
    grid_spec = pltpu.PrefetchScalarGridSpec(
        num_scalar_prefetch=1,
        grid=(nseq,),
        in_specs=[row(D_ATTN), row(LANES),
                  pl.BlockSpec((1, KV_SLOTS, ncr, HEAD_DIM), lambda b, pt: (b, 0, 0, 0)),
                  pl.BlockSpec((1, KV_SLOTS, HEAD_DIM, ncr), lambda b, pt: (b, 0, 0, 0)),
                  pl.BlockSpec(memory_space=pl.ANY),
                  row(KV_ROW), win_spec, const(kw_t), const(m_tab), const(e)],
        out_specs=[row(D_ATTN), win_spec],
        scratch_shapes=[pltpu.VMEM((2, KV_ROW, past), _F32), pltpu.SemaphoreType.DMA((2,))],
    )
    return pl.pallas_call(
        functools.partial(_nsa_sample_kernel, n_pages=n_pages, past=past, n_cmp=ncr - 1,
                          blk_pad=blk_pad, n_buf=n_buf),
        grid_spec=grid_spec,
        out_shape=[jax.ShapeDtypeStruct((nseq, 1, D_ATTN), _F32),
                   jax.ShapeDtypeStruct((nseq, KV_ROW, n_buf), _F32)],
        compiler_params=_params("arbitrary"),
        name="nsa_sample",
    )(page_table, q, gates, kcmp, kcmp_t, cache_t, ks_new, win_t, kw_t, m_tab, e)


def _post_kernel(x_ref, y_ref, u_ref, oa_ref, mod_ref, d_ref, wglu_ref, gs_ref, ga_ref, wout_ref, gm_ref,
                 wup_ref, wdn_ref, gf_ref, o_ref, *, ff_chunk):
    z = _gelu_tanh(y_ref[0] + d_ref[...] * u_ref[0])
    o_ssm = z * _sigmoid(jnp.dot(z.astype(_BF16), wglu_ref[...], preferred_element_type=_F32))
    mix = (jnp.dot(_rms(o_ssm, gs_ref[...]).astype(_BF16), wout_ref[:D_SSM, :], preferred_element_type=_F32)
           + jnp.dot(_rms(oa_ref[0], ga_ref[...]).astype(_BF16), wout_ref[D_SSM:, :], preferred_element_type=_F32))
    x1 = x_ref[0] + mod_ref[0, 2] * mix
    h2 = (_rms(x1, gm_ref[...]) * (1.0 + mod_ref[0, 4]) + mod_ref[0, 3]).astype(_BF16)
    acc = jnp.zeros_like(x1)
    for k in range(D_FF // ff_chunk):
        cols = slice(k * ff_chunk, (k + 1) * ff_chunk)
        hid = jnp.maximum(jnp.dot(h2, wup_ref[:, cols], preferred_element_type=_F32), 0.0)
        acc = acc + jnp.dot((hid * hid).astype(_BF16), wdn_ref[cols, :], preferred_element_type=_F32)
    o_ref[0] = _rms(x1 + mod_ref[0, 5] * acc, gf_ref[...])


def _post(x, y_ssm, u, o_attn, mod, lp, norm_final, tm):
    bsz, t, _ = x.shape
    r = mod.shape[2]
    rb = 1 if r == 1 else tm
    mod_map = (lambda b, i: (b, 0, 0, 0)) if r == 1 else (lambda b, i: (b, 0, i, 0))
    tok = lambda w: pl.BlockSpec((1, tm, w), lambda b, i: (b, i, 0))
    const = lambda shape: pl.BlockSpec(shape, lambda b, i: (0,) * len(shape), pipeline_mode=pl.Buffered(1))
    vec = lambda v: v.reshape(1, -1).astype(_F32)
    weights = [vec(lp['ssm_d']), lp['ssm_w_glu'].astype(_BF16), vec(lp['norm_out_ssm']), vec(lp['norm_out_attn']),
               lp['w_out'].astype(_BF16), vec(lp['norm_mlp']), lp['w_up'].astype(_BF16), lp['w_down'].astype(_BF16),
               vec(norm_final)]
    return pl.pallas_call(
        functools.partial(_post_kernel, ff_chunk=D_MODEL),
        grid=(bsz, t // tm),
        in_specs=[tok(D_MODEL), tok(D_SSM), tok(D_SSM), tok(D_ATTN),
                  pl.BlockSpec((1, 6, rb, D_MODEL), mod_map)] + [const(w.shape) for w in weights],
        out_specs=tok(D_MODEL),
        out_shape=jax.ShapeDtypeStruct((bsz, t, D_MODEL), _F32),
        compiler_params=_params("parallel", "parallel"),
        name="post_mlp",
    )(x, y_ssm, u, o_attn, mod, *weights)


def _feature_major(a):
    lead = a.shape[:-4]
    n = len(lead)
    return a.transpose(*range(n), n + 1, n + 2, n + 3, n).reshape(*lead, KV_ROW, a.shape[-4])


def _kv_output(a_t):
    n, _, t = a_t.shape
    return a_t.reshape(n, 2, N_KV_HEADS, HEAD_DIM, t).transpose(0, 4, 1, 2, 3)[None]


def kernel(x_prompt, x_sample, cache_cmp, cache_slc, state_win, state_ssm_re, state_ssm_im, page_table,
           c_prompt, c_sample, w_ada, b_ada, norm_attn, w_in, ssm_lambda_re, ssm_lambda_im, ssm_log_dt,
           ssm_b_re, ssm_b_im, ssm_c_re, ssm_c_im, ssm_d, ssm_w_glu, cmp_pe_k, cmp_w1_k, cmp_w2_k,
           cmp_pe_v, cmp_w1_v, cmp_w2_v, norm_out_ssm, norm_out_attn, w_out, norm_mlp, w_up, w_down,
           norm_final):
    depth = w_ada.shape[0]
    assert depth == 1, "single-layer trunk"
    l = 0
    bsz, t, _ = x_prompt.shape
    nseq = x_sample.shape[0]
    assert x_sample.shape[1] == 1
    tm = min(512, t)
    tq = min(256, t)
    kc_len = min(512, t)

    lp = dict(ssm_d=ssm_d[l], ssm_w_glu=ssm_w_glu[l], norm_out_ssm=norm_out_ssm[l],
              norm_out_attn=norm_out_attn[l], w_out=w_out[l], norm_mlp=norm_mlp[l], w_up=w_up[l], w_down=w_down[l])
    w_full = jnp.pad(w_in[l], ((0, 0), (0, D_IN_PAD - D_IN))).astype(_BF16)
    w_rows = jnp.concatenate([w_full[:, _Q0:_G0 + 3 * N_HEADS].T,
                              jnp.zeros((GATE_ROWS - 3 * N_HEADS, D_MODEL), _BF16)], axis=0)
    tq0, tkc0, tks0, tkw0, tg0 = 0, D_ATTN, D_ATTN + KV_ROW, D_ATTN + 2 * KV_ROW, D_ATTN + 3 * KV_ROW
    tabs = _ssm_tables(ssm_lambda_re[l], ssm_lambda_im[l], ssm_log_dt[l], ssm_b_re[l], ssm_b_im[l],
                       ssm_c_re[l], ssm_c_im[l])
    ctabs = _compress_tables(cmp_pe_k[l], cmp_w1_k[l], cmp_w2_k[l], cmp_pe_v[l], cmp_w1_v[l], cmp_w2_v[l])

    mod = _ada_mod(jnp.concatenate([c_prompt, c_sample], axis=0), w_ada[l], b_ada[l])
    mod_p = mod[:bsz].reshape(bsz, 6, 1, D_MODEL)
    mod_s = mod[bsz:].reshape(nseq, 6, D_MODEL).transpose(1, 0, 2)[None]

    w_tok = jnp.concatenate([w_full[:, :D_SSM], w_full[:, _KC0:_G0]], axis=1)
    outs_p = [('n', 0, D_SSM, None), ('n', D_SSM, D_SSM + KV_ROW, None),
              ('n', D_SSM + KV_ROW, D_SSM + 2 * KV_ROW, None), ('n', D_SSM + 2 * KV_ROW, D_SSM + 3 * KV_ROW, None),
              ('t', tq0, tkc0, 'scale'), ('t', tkc0, tks0, None), ('t', tks0, tkw0, None), ('t', tkw0, tg0, None),
              ('t', tks0, tkw0, None), ('t', tkw0, tg0, None), ('t', tg0, tg0 + GATE_ROWS, 'sigmoid')]
    dt_p = [_F32, _F32, _BF16, _BF16, _F32, _F32, _F32, _F32, _BF16, _BF16, _F32]
    (u, kc_n, ks_n, kw_n, q_t, kc_t, ks_t, kw_t, ks_tb, kw_tb, gates_t) = _in_proj(
        x_prompt, mod_p, norm_attn[l], w_tok, w_rows, tm, outs_p, dt_p)
    y_ssm, sp_re, sp_im = _ssm_prompt(u, tabs)
    kcmp, kcmp_t = _compress_prompt(kc_n, ctabs)
    o_attn = _nsa_prompt(q_t, gates_t, kcmp, kcmp_t, ks_n, ks_tb, kw_n, kw_tb, tq, kc_len)
    y_prompt = _post(x_prompt, y_ssm, u, o_attn, mod_p, lp, norm_final, tm)
    w_keep = min(WINDOW, t)

    xs = x_sample.reshape(1, nseq, D_MODEL)
    outs_s = [('n', 0, D_SSM, None), ('n', _Q0, _KC0, 'scale'), ('n', _KS0, _KW0, None), ('n', _G0, D_IN_PAD, 'sigmoid'),
              ('t', tkc0, tks0, None), ('t', tks0, tkw0, None), ('t', tkw0, tg0, None)]
    u_s, q_s, ks_row, gates_s, kc_ts, ks_ts, kw_ts = _in_proj(
        xs, mod_s, norm_attn[l], w_full, w_rows, nseq, outs_s, [_F32] * len(outs_s))
    ns = N_SSM_GROUPS * SSM_STATE
    y_s, ss_re, ss_im = _ssm_step(u_s[0], state_ssm_re[l].reshape(nseq, ns), state_ssm_im[l].reshape(nseq, ns), tabs)
    kcmp_s, kcmp_ts = _compress_paged(_feature_major(cache_cmp[l]), page_table, ctabs)
    o_attn_s, new_win_t = _nsa_sample(q_s.reshape(nseq, 1, D_ATTN), gates_s.reshape(nseq, 1, LANES), kcmp_s, kcmp_ts,
                                      _feature_major(cache_slc[l]), page_table, ks_row.reshape(nseq, 1, KV_ROW),
                                      _feature_major(state_win[l]), kw_ts[0])
    y_sample = _post(xs, y_s[None], u_s, o_attn_s.reshape(1, nseq, D_ATTN), mod_s, lp, norm_final, nseq)
    st_shape = (1, nseq, N_SSM_GROUPS, SSM_STATE)
    new_row = lambda a_t: _kv_output(a_t[0].T[:, :, None])

    return (y_prompt, y_sample.reshape(nseq, 1, D_MODEL),
            _kv_output(kc_t), _kv_output(ks_t), _kv_output(kw_t[:, :, t - w_keep:]),
            sp_re[None], sp_im[None],
            new_row(kc_ts), new_row(ks_ts), _kv_output(new_win_t),
            ss_re.reshape(st_shape), ss_im.reshape(st_shape))
```

```python
import functools
import math

import jax
import jax.numpy as jnp
from jax import lax
from jax.experimental import pallas as pl
from jax.experimental.pallas import tpu as pltpu

D_MODEL = 1024
D_SSM = D_MODEL // 2
SSM_GROUP = 16
N_SSM_GROUPS = D_SSM // SSM_GROUP
SSM_STATE = 64
HEAD_DIM = 64
D_ATTN = D_MODEL - D_SSM
N_HEADS = D_ATTN // HEAD_DIM
N_KV_HEADS = 2
GROUP_SIZE = N_HEADS // N_KV_HEADS
D_KV = N_KV_HEADS * HEAD_DIM
CMP_BLOCK = 32
CMP_STRIDE = 16
CMP_HIDDEN = 2 * HEAD_DIM
SLC_BLOCK = 64
N_SELECT = 16
N_LOCAL = 2
WINDOW = 512
D_FF = 4 * D_MODEL
D_IN = D_SSM + D_ATTN + 6 * D_KV + 3 * N_HEADS
EPS = 1e-6
PAGE_SIZE = 128

LANES = 128
SUBLANES = 8
D_IN_PAD = -(-D_IN // LANES) * LANES
KV_SLOTS = 2 * N_KV_HEADS
KV_ROW = KV_SLOTS * HEAD_DIM
GATE_ROWS = 32
SSM_CHUNK = 16
SSM_LANE_GROUPS = LANES // SSM_GROUP
VMEM_LIMIT = 56 * 1024 * 1024
FORCED_SCORE = 1e30
NEG_INF = float("-inf")
LOG2_E = math.log2(math.e)
RELAYOUT_UNROLL = 8

_Q0, _KC0, _KS0, _KW0, _G0 = D_SSM, D_MODEL, D_MODEL + KV_ROW, D_MODEL + 2 * KV_ROW, D_MODEL + 3 * KV_ROW

_BF16 = jnp.bfloat16
_F32 = jnp.float32
_NT = (((1,), (1,)), ((), ()))
_HIGHEST = lax.Precision.HIGHEST


def _params(*sem):
    return pltpu.CompilerParams(dimension_semantics=sem, vmem_limit_bytes=VMEM_LIMIT)


def _rms(x, g):
    return x * lax.rsqrt(jnp.mean(x * x, axis=-1, keepdims=True) + EPS) * g


def _gelu_tanh(x):
    return x * (0.5 * (1.0 + jnp.tanh(math.sqrt(2.0 / math.pi) * (x + 0.044715 * (x * x * x)))))


def _sigmoid(x):
    return 1.0 / (1.0 + jnp.exp(-x))


def _bdot(a, b):
    return jnp.dot(a.astype(_BF16), b.astype(_BF16), preferred_element_type=_F32)


def _bdot_nt(a, b):
    return lax.dot_general(a.astype(_BF16), b.astype(_BF16), _NT, preferred_element_type=_F32)


def _ada_kernel(c_ref, w_ref, b_ref, o_ref):
    c = c_ref[...]
    o_ref[...] = jnp.dot(c * _sigmoid(c), w_ref[...], precision=_HIGHEST,
                         preferred_element_type=_F32) + b_ref[...]


def _ada_mod(c, w_ada, b_ada):
    n, tn = c.shape[0], D_MODEL
    return pl.pallas_call(
        _ada_kernel,
        grid=(6 * D_MODEL // tn,),
        in_specs=[pl.BlockSpec((n, D_MODEL), lambda j: (0, 0)),
                  pl.BlockSpec((D_MODEL, tn), lambda j: (0, j)),
                  pl.BlockSpec((1, tn), lambda j: (0, j))],
        out_specs=pl.BlockSpec((n, tn), lambda j: (0, j)),
        out_shape=jax.ShapeDtypeStruct((n, 6 * D_MODEL), _F32),
        compiler_params=_params("parallel"),
        name="ada_mod",
    )(c, w_ada, b_ada.reshape(1, -1))


def _inproj_kernel(x_ref, mod_ref, g_ref, wn_ref, wt_ref, *out_refs, outs):
    x = x_ref[0]
    h = (_rms(x, g_ref[...]) * (1.0 + mod_ref[0, 1]) + mod_ref[0, 0]).astype(_BF16)
    pn = jnp.dot(h, wn_ref[...], preferred_element_type=_F32)
    pt = lax.dot_general(wt_ref[...], h, _NT, preferred_element_type=_F32)
    for ref, (kind, lo, hi, post) in zip(out_refs, outs):
        v = pn[:, lo:hi] if kind == 'n' else pt[lo:hi, :]
        if post == 'scale':
            v = v * (HEAD_DIM ** -0.5)
        elif post == 'sigmoid':
            v = _sigmoid(v)
        ref[0] = v.astype(ref.dtype)


def _in_proj(x, mod, norm_attn, wn, wt, tm, outs, dtypes):
    bsz, t, _ = x.shape
    r = mod.shape[2]
    rb = 1 if r == 1 else tm
    mod_map = (lambda b, i: (b, 0, 0, 0)) if r == 1 else (lambda b, i: (b, 0, i, 0))
    out_specs, out_shape = [], []
    for (kind, lo, hi, _), dt in zip(outs, dtypes):
        if kind == 'n':
            out_specs.append(pl.BlockSpec((1, tm, hi - lo), lambda b, i: (b, i, 0)))
            out_shape.append(jax.ShapeDtypeStruct((bsz, t, hi - lo), dt))
        else:
            out_specs.append(pl.BlockSpec((1, hi - lo, tm), lambda b, i: (b, 0, i)))
            out_shape.append(jax.ShapeDtypeStruct((bsz, hi - lo, t), dt))
    return pl.pallas_call(
        functools.partial(_inproj_kernel, outs=tuple(outs)),
        grid=(bsz, t // tm),
        in_specs=[pl.BlockSpec((1, tm, D_MODEL), lambda b, i: (b, i, 0)),
                  pl.BlockSpec((1, 6, rb, D_MODEL), mod_map),
                  pl.BlockSpec((1, D_MODEL), lambda b, i: (0, 0)),
                  pl.BlockSpec(wn.shape, lambda b, i: (0, 0)),
                  pl.BlockSpec(wt.shape, lambda b, i: (0, 0))],
        out_specs=out_specs,
        out_shape=out_shape,
        compiler_params=_params("parallel", "parallel"),
        name="in_proj",
    )(x, mod, norm_attn.reshape(1, -1), wn, wt)


def _ssm_tables(lam_re, lam_im, log_dt, b_re, b_im, c_re, c_im):
    hp = dict(precision=_HIGHEST)
    lr, li = lam_re.astype(_F32), lam_im.astype(_F32)
    dt = jnp.exp(log_dt.astype(_F32))[:, None]
    ar, ai = lr * dt, li * dt

    def power(k):
        mag = jnp.exp(ar * k)
        return mag * jnp.cos(ai * k), mag * jnp.sin(ai * k)

    n = SSM_CHUNK
    steps = jnp.arange(n + 1, dtype=_F32)[:, None, None]
    pwr, pwi = power(steps)
    lbr, lbi = pwr[1], pwi[1]
    den = lr * lr + li * li
    fr = ((lbr - 1.0) * lr + lbi * li) / den
    fi = (lbi * lr - (lbr - 1.0) * li) / den
    br_, bi_ = b_re.astype(_F32), b_im.astype(_F32)
    bbr = fr[:, :, None] * br_ - fi[:, :, None] * bi_
    bbi = fr[:, :, None] * bi_ + fi[:, :, None] * br_
    cr, ci = c_re.astype(_F32), c_im.astype(_F32)
    er = pwr[:, :, :, None] * bbr[None] - pwi[:, :, :, None] * bbi[None]
    ei = pwr[:, :, :, None] * bbi[None] + pwi[:, :, :, None] * bbr[None]
    kern = (jnp.einsum('gip,dgpj->dgij', cr, er[:n], **hp)
            - jnp.einsum('gip,dgpj->dgij', ci, ei[:n], **hp))
    nb, gpb = N_SSM_GROUPS // SSM_LANE_GROUPS, SSM_LANE_GROUPS

    def expand(a, g_axis):
        w = a.shape[-1]
        owner = jnp.arange(gpb * w) // w
        own = jnp.arange(gpb).reshape((gpb,) + (1,) * (a.ndim - 1 - g_axis))
        return jnp.where(owner == own, jnp.tile(a, (1,) * (a.ndim - 1) + (gpb,)), 0.0)

    lag_tiles = kern.reshape(n, nb, gpb, SSM_GROUP, SSM_GROUP).transpose(1, 0, 2, 4, 3)
    lag_tiles = expand(lag_tiles, 2).reshape(nb, n, LANES, LANES)
    rev = n - 1 - jnp.arange(n)

    def state_in(e):
        w = e[rev].reshape(n, nb, gpb, SSM_STATE, SSM_GROUP).transpose(1, 0, 2, 4, 3)
        return expand(w, 2).reshape(nb, n * LANES, gpb * SSM_STATE)

    w_in = jnp.concatenate([state_in(er), state_in(ei)], axis=-1)
    cvr = cr[None] * pwr[1:, :, None, :] - ci[None] * pwi[1:, :, None, :]
    cvi = cr[None] * pwi[1:, :, None, :] + ci[None] * pwr[1:, :, None, :]

    def state_out(cv):
        v = cv.reshape(n, nb, gpb, SSM_GROUP, SSM_STATE).transpose(1, 0, 2, 4, 3)
        return expand(v, 2).reshape(nb, n, gpb * SSM_STATE, LANES)

    v_out = jnp.concatenate([state_out(cvr), -state_out(cvi)], axis=2)
    a_chunk = jnp.stack([pwr[n].reshape(nb, -1), pwi[n].reshape(nb, -1)], axis=1)
    eye = jnp.eye(N_SSM_GROUPS, dtype=_F32)
    ns = N_SSM_GROUPS * SSM_STATE
    bd = lambda a: jnp.einsum('gpi,gh->gihp', a, eye).reshape(D_SSM, ns)
    cd = lambda a: jnp.einsum('gip,gh->gphi', a, eye).reshape(ns, D_SSM)
    return dict(toep=lag_tiles.astype(_BF16), w_in=w_in.astype(_BF16), v_out=v_out.astype(_BF16), a_chunk=a_chunk,
                bd_re=bd(bbr).astype(_BF16), bd_im=bd(bbi).astype(_BF16),
                cd_re=cd(cr).astype(_BF16), cd_im=cd(ci).astype(_BF16),
                lam_re=lbr.reshape(1, ns), lam_im=lbi.reshape(1, ns))


def _ssm_prompt_kernel(u_ref, toep_ref, w_ref, v_ref, a_ref, y_ref, s_ref, x_scr, sin_scr, *, n_chunks):
    n = SSM_CHUNK
    lhs = jnp.concatenate([u_ref[0, pl.ds(tau, n_chunks, stride=n), :] for tau in range(n)],
                          axis=1).astype(_BF16)
    x_scr[...] = jnp.dot(lhs, w_ref[0], preferred_element_type=_F32)
    half = x_scr.shape[1] // 2
    a_re = a_ref[0, 0:1, :]
    a_im = a_ref[0, 1:2, :]

    def step(c, carry):
        s_re, s_im = carry
        row = pl.ds(c, 1)
        sin_scr[row, :half] = s_re
        sin_scr[row, half:] = s_im
        x = x_scr[row, :]
        return (a_re * s_re - a_im * s_im + x[:, :half],
                a_re * s_im + a_im * s_re + x[:, half:])

    zero = jnp.zeros((1, half), _F32)
    s_re, s_im = lax.fori_loop(0, n_chunks, step, (zero, zero))
    s_ref[0, 0] = jnp.concatenate([s_re, s_im], axis=1)
    sin_b = sin_scr[...].astype(_BF16)
    zero_tile = jnp.zeros((LANES, LANES), _BF16)
    for t2 in range(n // 2):
        t_lo = 2 * t2
        k_hi = (t_lo + 2) * LANES
        w_lag = jnp.concatenate(
            [jnp.concatenate([toep_ref[0, t_lo - tau] if tau <= t_lo else zero_tile, toep_ref[0, t_lo + 1 - tau]],
                             axis=1) for tau in range(t_lo + 2)], axis=0)
        w_state = jnp.concatenate([v_ref[0, t_lo], v_ref[0, t_lo + 1]], axis=1)
        y = (jnp.dot(lhs[:, :k_hi], w_lag, preferred_element_type=_F32)
             + jnp.dot(sin_b, w_state, preferred_element_type=_F32))
        for d in range(2):
            y_ref[0, pl.ds(t_lo + d, n_chunks, stride=n), :] = y[:, d * LANES:(d + 1) * LANES]


def _ssm_prompt(u, tabs):
    bsz, t, _ = u.shape
    nch = t // SSM_CHUNK
    nb = D_SSM // LANES
    sw = 2 * SSM_LANE_GROUPS * SSM_STATE
    table = lambda a: pl.BlockSpec((1,) + a.shape[1:], lambda m, b: (m,) + (0,) * (a.ndim - 1),
                                   pipeline_mode=pl.Buffered(1))
    tok = pl.BlockSpec((1, t, LANES), lambda m, b: (b, 0, m))
    y, s = pl.pallas_call(
        functools.partial(_ssm_prompt_kernel, n_chunks=nch),
        grid=(nb, bsz),
        in_specs=[tok, table(tabs['toep']), table(tabs['w_in']), table(tabs['v_out']),
                  pl.BlockSpec((1, 2, sw // 2), lambda m, b: (m, 0, 0))],
        out_specs=[tok, pl.BlockSpec((1, 1, 1, sw), lambda m, b: (b, m, 0, 0))],
        out_shape=[jax.ShapeDtypeStruct((bsz, t, D_SSM), _F32),
                   jax.ShapeDtypeStruct((bsz, nb, 1, sw), _F32)],
        scratch_shapes=[pltpu.VMEM((nch, sw), _F32), pltpu.VMEM((nch, sw), _F32)],
        compiler_params=_params("parallel", "parallel"),
        name="ssm_prompt",
    )(u, tabs['toep'], tabs['w_in'], tabs['v_out'], tabs['a_chunk'])
    state = lambda a: a.reshape(bsz, N_SSM_GROUPS, SSM_STATE)
    return y, state(s[:, :, 0, :sw // 2]), state(s[:, :, 0, sw // 2:])


def _ssm_step_kernel(u_ref, s0re_ref, s0im_ref, lre_ref, lim_ref, bdre_ref, bdim_ref, cdre_ref, cdim_ref,
                     y_ref, sre_ref, sim_ref):
    u = u_ref[...]
    s_re, s_im = s0re_ref[...], s0im_ref[...]
    l_re, l_im = lre_ref[...], lim_ref[...]
    n_re = l_re * s_re - l_im * s_im + _bdot(u, bdre_ref[...])
    n_im = l_re * s_im + l_im * s_re + _bdot(u, bdim_ref[...])
    sre_ref[...] = n_re
    sim_ref[...] = n_im
    y_ref[...] = _bdot(n_re, cdre_ref[...]) - _bdot(n_im, cdim_ref[...])


def _ssm_step(u, s0_re, s0_im, tabs):
    n = u.shape[0]
    ns = N_SSM_GROUPS * SSM_STATE
    return pl.pallas_call(
        _ssm_step_kernel,
        out_shape=[jax.ShapeDtypeStruct((n, D_SSM), _F32),
                   jax.ShapeDtypeStruct((n, ns), _F32),
                   jax.ShapeDtypeStruct((n, ns), _F32)],
        compiler_params=pltpu.CompilerParams(vmem_limit_bytes=VMEM_LIMIT),
        name="ssm_step",
    )(u, s0_re, s0_im, tabs['lam_re'], tabs['lam_im'], tabs['bd_re'], tabs['bd_im'], tabs['cd_re'], tabs['cd_im'])


def _compress_tables(pe_k, w1_k, w2_k, pe_v, w1_v, w2_v):
    zeros = jnp.zeros((CMP_STRIDE, HEAD_DIM, CMP_HIDDEN), _F32)
    cols = []
    for half in range(2):
        for slot in range(KV_SLOTS):
            w1 = (w1_k if slot < N_KV_HEADS else w1_v)[half * CMP_STRIDE:(half + 1) * CMP_STRIDE]
            blk = jnp.stack([w1 if s == slot else zeros for s in range(KV_SLOTS)], axis=1)
            cols.append(blk.reshape(CMP_STRIDE * KV_ROW, CMP_HIDDEN))
    w_all = jnp.concatenate(cols, axis=1).astype(_BF16)
    z2 = jnp.zeros((CMP_HIDDEN, HEAD_DIM), _F32)
    w2 = jnp.concatenate(
        [jnp.concatenate([(w2_k if s < N_KV_HEADS else w2_v) if s == slot else z2 for s in range(KV_SLOTS)], axis=1)
         for slot in range(KV_SLOTS)], axis=0).astype(_BF16)
    pe = jnp.stack([pe_k.reshape(-1), pe_v.reshape(-1)], axis=0)
    pe = jnp.concatenate([pe, jnp.zeros((SUBLANES - 2, pe.shape[1]), _F32)], axis=0)
    w1f = jnp.stack([w1_k.reshape(-1, CMP_HIDDEN), w1_v.reshape(-1, CMP_HIDDEN)], axis=0)
    halves = lambda w1: jnp.concatenate([w1[:CMP_STRIDE].reshape(-1, CMP_HIDDEN),
                                         w1[CMP_STRIDE:].reshape(-1, CMP_HIDDEN)], axis=1)
    w_slot = jnp.stack([halves(w1_k if s < N_KV_HEADS else w1_v) for s in range(KV_SLOTS)], axis=0).astype(_BF16)
    return dict(w_all=w_all, w_slot=w_slot, tail=(w2, w2.T, pe, w1f))


def _compress_bias(pe_ref, w1f_ref):
    pe = pe_ref[...]
    bias_k = jnp.dot(pe, w1f_ref[0], precision=_HIGHEST, preferred_element_type=_F32)[0:1]
    bias_v = jnp.dot(pe, w1f_ref[1], precision=_HIGHEST, preferred_element_type=_F32)[1:2]
    return jnp.concatenate([bias_k, bias_k, bias_v, bias_v], axis=1)


def _compress_tail(h, bias, w2_ref, w2t_ref, n_rows):
    hw = KV_SLOTS * CMP_HIDDEN
    nxt = pltpu.roll(h[:, hw:], n_rows - 1, 0)
    hid = _gelu_tanh(h[:, :hw] + nxt + bias)
    row = lax.broadcasted_iota(jnp.int32, hid.shape, 0)
    hid = jnp.where(row < n_rows - 1, hid, 0.0).astype(_BF16)
    return (jnp.dot(hid, w2_ref[...], preferred_element_type=_F32),
            lax.dot_general(w2t_ref[...], hid, _NT, preferred_element_type=_F32))


def _split3_keys(k):
    hi = k.astype(_BF16)
    lo = (k - hi.astype(_F32)).astype(_BF16)
    return jnp.concatenate([hi, lo, hi, jnp.zeros_like(hi)], axis=1)


def _split3_queries(q_t):
    hi = q_t.astype(_BF16)
    lo = (q_t - hi.astype(_F32)).astype(_BF16)
    return jnp.concatenate([hi, hi, lo, jnp.zeros_like(hi)], axis=0)


def _store_compressed(out, out_t, k3_ref, ot_ref):
    for g in range(N_KV_HEADS):
        k3_ref[0, g] = _split3_keys(out[:, g * HEAD_DIM:(g + 1) * HEAD_DIM])
    for s in range(KV_SLOTS):
        ot_ref[0, s] = out_t[s * HEAD_DIM:(s + 1) * HEAD_DIM, :]


def _compress_prompt_kernel(x_ref, wall_ref, w2_ref, w2t_ref, pe_ref, w1f_ref, k3_ref, ot_ref, *, n_rows):
    h = jnp.dot(x_ref[0].astype(_BF16), wall_ref[...], preferred_element_type=_F32)
    out, out_t = _compress_tail(h, _compress_bias(pe_ref, w1f_ref), w2_ref, w2t_ref, n_rows)
    _store_compressed(out, out_t, k3_ref, ot_ref)


def _compress_out(nseq, nc, index_map):
    specs = [pl.BlockSpec((1, N_KV_HEADS, nc, KV_SLOTS * HEAD_DIM), index_map),
             pl.BlockSpec((1, KV_SLOTS, HEAD_DIM, nc), index_map)]
    shapes = [jax.ShapeDtypeStruct((nseq, N_KV_HEADS, nc, KV_SLOTS * HEAD_DIM), _BF16),
              jax.ShapeDtypeStruct((nseq, KV_SLOTS, HEAD_DIM, nc), _F32)]
    return specs, shapes


def _compress_prompt(kv_cmp, ctabs):
    bsz, t, _ = kv_cmp.shape
    nc = t // CMP_STRIDE
    const = lambda a: pl.BlockSpec(a.shape, lambda b: (0,) * a.ndim)
    weights = (ctabs['w_all'],) + ctabs['tail']
    out_specs, out_shape = _compress_out(bsz, nc, lambda b: (b, 0, 0, 0))
    return pl.pallas_call(
        functools.partial(_compress_prompt_kernel, n_rows=nc),
        grid=(bsz,),
        in_specs=[pl.BlockSpec((1, nc, CMP_STRIDE * KV_ROW), lambda b: (b, 0, 0))] + [const(a) for a in weights],
        out_specs=out_specs,
        out_shape=out_shape,
        compiler_params=_params("parallel"),
        name="compress_prompt",
    )(kv_cmp.reshape(bsz, nc, CMP_STRIDE * KV_ROW), *weights)


def _page_copies(pt_ref, cache_hbm, buf, sem, seq, slot, n_pages):
    return [pltpu.make_async_copy(cache_hbm.at[pt_ref[seq, j]],
                                  buf.at[slot, :, pl.ds(j * PAGE_SIZE, PAGE_SIZE)],
                                  sem.at[slot]) for j in range(n_pages)]


def _gather_pages(pt_ref, cache_hbm, buf, sem, n_pages):
    b = pl.program_id(0)
    slot = lax.rem(b, 2)

    @pl.when(b == 0)
    def _():
        for cp in _page_copies(pt_ref, cache_hbm, buf, sem, 0, 0, n_pages):
            cp.start()

    @pl.when(b + 1 < pl.num_programs(0))
    def _():
        for cp in _page_copies(pt_ref, cache_hbm, buf, sem, b + 1, 1 - slot, n_pages):
            cp.start()

    for cp in _page_copies(pt_ref, cache_hbm, buf, sem, b, slot, n_pages):
        cp.wait()
    return slot


def _chunk_permutation():
    span = 2 * PAGE_SIZE
    r = jnp.arange(span)
    src = (r % (span // CMP_STRIDE)) * CMP_STRIDE + r // (span // CMP_STRIDE)
    return (src[:, None] == jnp.arange(span)[None, :]).astype(_BF16)


def _compress_paged_kernel(pt_ref, cache_hbm, perm_ref, wslot_ref, w2_ref, w2t_ref, pe_ref, w1f_ref,
                           o_ref, ot_ref, buf, sem, xr, bias_scr, *, n_pages, n_rows):
    slot = _gather_pages(pt_ref, cache_hbm, buf, sem, n_pages)

    @pl.when(pl.program_id(0) == 0)
    def _():
        bias_scr[...] = jnp.broadcast_to(_compress_bias(pe_ref, w1f_ref), bias_scr.shape)

    span = 2 * PAGE_SIZE
    cps = span // CMP_STRIDE
    low = lax.broadcasted_iota(jnp.int32, (cps, LANES), 1) < HEAD_DIM

    def relayout(i, carry):
        xt = buf[slot, :, pl.ds(pl.multiple_of(i * span, span), span)].astype(_BF16)
        rows = lax.dot_general(perm_ref[...], xt, _NT, preferred_element_type=_F32)
        chunk_rows = pl.ds(pl.multiple_of(i * cps, cps), cps)
        for a in range(CMP_STRIDE // 2):
            for s in range(KV_SLOTS):
                tile = slice((s // 2) * LANES, (s // 2 + 1) * LANES)
                even = rows[2 * a * cps:(2 * a + 1) * cps, tile]
                odd = rows[(2 * a + 1) * cps:(2 * a + 2) * cps, tile]
                if s % 2 == 0:
                    pair = jnp.where(low, even, pltpu.roll(odd, HEAD_DIM, 1))
                else:
                    pair = jnp.where(low, pltpu.roll(even, HEAD_DIM, 1), odd)
                xr[s, chunk_rows, a * LANES:(a + 1) * LANES] = pair.astype(_BF16)
        return carry

    lax.fori_loop(0, n_pages // 2, relayout, 0, unroll=math.gcd(RELAYOUT_UNROLL, n_pages // 2))
    hs = [jnp.dot(xr[s], wslot_ref[s], preferred_element_type=_F32) for s in range(KV_SLOTS)]
    h = jnp.concatenate([v[:, :CMP_HIDDEN] for v in hs] + [v[:, CMP_HIDDEN:] for v in hs], axis=1)
    out, out_t = _compress_tail(h, bias_scr[0:1, :], w2_ref, w2t_ref, n_rows)
    _store_compressed(out, out_t, o_ref, ot_ref)


def _compress_paged(cache_t, page_table, ctabs):
    nseq, n_pages = page_table.shape
    past = n_pages * PAGE_SIZE
    nc = past // CMP_STRIDE
    perm = _chunk_permutation()
    const = lambda a: pl.BlockSpec(a.shape, lambda b, pt: (0,) * a.ndim)
    out_specs, out_shape = _compress_out(nseq, nc, lambda b, pt: (b, 0, 0, 0))
    weights = (ctabs['w_slot'],) + ctabs['tail']
    grid_spec = pltpu.PrefetchScalarGridSpec(
        num_scalar_prefetch=1,
        grid=(nseq,),
        in_specs=[pl.BlockSpec(memory_space=pl.ANY), const(perm)] + [const(a) for a in weights],
        out_specs=out_specs,
        scratch_shapes=[pltpu.VMEM((2, KV_ROW, past), _F32), pltpu.SemaphoreType.DMA((2,)),
                        pltpu.VMEM((KV_SLOTS, nc, CMP_STRIDE * HEAD_DIM), _BF16),
                        pltpu.VMEM((SUBLANES, KV_SLOTS * CMP_HIDDEN), _F32)],
    )
    return pl.pallas_call(
        functools.partial(_compress_paged_kernel, n_pages=n_pages, n_rows=nc),
        grid_spec=grid_spec,
        out_shape=out_shape,
        compiler_params=_params("arbitrary"),
        name="compress_paged",
    )(page_table, cache_t, perm, *weights)


def _importance_map(n_cmp_rows, n_blocks_rows, n_cmp):
    ratio = SLC_BLOCK // CMP_STRIDE
    j = jnp.arange(n_blocks_rows)[:, None]
    n = jnp.arange(n_cmp_rows)[None, :]
    off = n - ratio * j
    w = jnp.where((off == -1) | (off == ratio - 1), 1.0, jnp.where((off >= 0) & (off < ratio - 1), 2.0, 0.0))
    return jnp.where(n < n_cmp, w, 0.0).astype(_F32)


def _softmax(s, ok, axis):
    s = jnp.where(ok, s, NEG_INF)
    m = jnp.max(s, axis=axis, keepdims=True)
    m = jnp.where(m == NEG_INF, 0.0, m)
    e = jnp.exp(s - m)
    den = jnp.sum(e, axis=axis, keepdims=True)
    return e / jnp.where(den > 0, den, 1.0)


def _nsa_prompt_kernel(q_ref, gt_ref, kc_ref, vct_ref, ksn_ref, vst_ref, kwn_ref, vwt_ref, mt_ref, o_ref,
                       bias_ref, s_a, s_b, *, tq, kc_len, n_cmp, n_blk):
    g = pl.program_id(1)
    t0 = pl.program_id(2) * tq
    cols = GROUP_SIZE * tq
    q_t = q_ref[0]
    q4 = jnp.concatenate([q_t[r * HEAD_DIM:(r + 1) * HEAD_DIM, :] for r in range(GROUP_SIZE)], axis=1)
    q4b = (q4 * LOG2_E).astype(_BF16)
    feat = lax.broadcasted_iota(jnp.int32, (KV_ROW, cols), 0)
    qp = jnp.where(lax.shift_right_logical(feat, int(math.log2(HEAD_DIM))) == g,
                   jnp.concatenate([q4b] * KV_SLOTS, axis=0), jnp.zeros((), _BF16))
    lane = lax.broadcasted_iota(jnp.int32, (1, cols), 1)
    qpos = t0 + (lane & (tq - 1))

    ncr = kc_ref.shape[2]
    s_c = jnp.dot(kc_ref[0, 0], _split3_queries(q4), preferred_element_type=_F32)
    n_idx = lax.broadcasted_iota(jnp.int32, (ncr, 1), 0)
    cmp_end = jnp.where(n_idx < n_cmp, n_idx * CMP_STRIDE + (CMP_BLOCK - 1), jnp.int32(2 ** 30))
    p_c = _softmax(s_c, cmp_end <= qpos, 0)
    o_c = _bdot(vct_ref[0, 0], p_c)

    imp = p_c[:, 0:tq]
    for r in range(1, GROUP_SIZE):
        imp = imp + p_c[:, r * tq:(r + 1) * tq]
    imp_hi = imp.astype(_BF16)
    imp_r = imp - imp_hi.astype(_F32)
    imp_mid = imp_r.astype(_BF16)
    imp_lo = (imp_r - imp_mid.astype(_F32)).astype(_BF16)
    imp_s = jnp.dot(mt_ref[...], jnp.concatenate([imp_hi, imp_mid, imp_lo], axis=0),
                    preferred_element_type=_F32)
    qrow = qpos[:, :tq]
    j_idx = lax.broadcasted_iota(jnp.int32, (n_blk, tq), 0)
    valid = j_idx * SLC_BLOCK <= qrow
    cur = lax.shift_right_arithmetic(qrow, int(math.log2(SLC_BLOCK)))
    forced = (j_idx == 0) | (valid & (j_idx > cur - N_LOCAL))
    score = jnp.where(forced, FORCED_SCORE, jnp.where(valid, imp_s, -1.0))
    rank = jnp.zeros((n_blk, tq), jnp.int32)
    row8 = lax.broadcasted_iota(jnp.int32, (SUBLANES, tq), 0)
    for j in range(n_blk):
        sj = score[j:j + 1, :]
        lo, hi = j // SUBLANES * SUBLANES, (j // SUBLANES + 1) * SUBLANES
        own = score[lo:hi]
        parts = [jnp.where(row8 > j - lo, jnp.where(sj >= own, 1, 0), jnp.where(sj > own, 1, 0))]
        if lo > 0:
            parts.insert(0, jnp.where(sj > score[:lo], 1, 0))
        if hi < n_blk:
            parts.append(jnp.where(sj >= score[hi:], 1, 0))
        rank = rank + jnp.concatenate(parts, axis=0)
    bias = jnp.where(valid & (rank < N_SELECT), 0.0, NEG_INF)
    bias_ref[...] = jnp.concatenate([bias] * GROUP_SIZE, axis=1)

    bpc = kc_len // SLC_BLOCK

    def scores(c, causal):
        k0 = pl.multiple_of(c * kc_len, kc_len)
        s = jnp.dot(ksn_ref[0, pl.ds(k0, kc_len), :], qp, preferred_element_type=_F32)
        if causal:
            kpos = k0 + lax.broadcasted_iota(jnp.int32, (kc_len, 1), 0)
            s = jnp.where(kpos <= qpos, s, NEG_INF)
        return s

    def with_ones(vals_t):
        return jnp.concatenate([vals_t, jnp.ones((2 * SUBLANES, vals_t.shape[1]), _BF16)], axis=0)

    def weighted(s, m, vals_t):
        return jnp.dot(with_ones(vals_t), jnp.exp2(s - m).astype(_BF16), preferred_element_type=_F32)

    def fold_in(c, s, m_acc, acc):
        k0 = pl.multiple_of(c * kc_len, kc_len)
        b8 = bias_ref[pl.ds(pl.multiple_of(c * bpc, bpc), bpc), :]
        blocks = [s[i * SLC_BLOCK:(i + 1) * SLC_BLOCK, :] for i in range(bpc)]
        m_new = m_acc
        for i in range(bpc):
            m_new = jnp.maximum(m_new, jnp.max(blocks[i], axis=0, keepdims=True) + b8[i:i + 1, :])
        p = jnp.concatenate([jnp.exp2(blocks[i] - (m_new - b8[i:i + 1, :])) for i in range(bpc)], axis=0)
        return m_new, (jnp.exp2(m_acc - m_new) * acc
                       + jnp.dot(with_ones(vst_ref[0, :, pl.ds(k0, kc_len)]), p.astype(_BF16),
                                 preferred_element_type=_F32))

    def step(c, causal, s_new_ref, s_old_ref, m_acc, acc):
        s_new_ref[...] = scores(c, causal)
        return fold_in(c - 1, s_old_ref[...], m_acc, acc)

    def pair(p, carry):
        c = 2 * p + 1
        return step(c + 1, True, s_a, s_b, *step(c, False, s_b, s_a, *carry))

    c_diag = t0 // kc_len
    s_a[...] = scores(0, True)
    init = (jnp.full((1, cols), NEG_INF, _F32), jnp.zeros((HEAD_DIM + 2 * SUBLANES, cols), _F32))
    m_acc, acc_s = lax.fori_loop(0, c_diag // 2, pair, init)

    def odd_tail(m_acc, acc):
        m_acc, acc = step(c_diag, True, s_b, s_a, m_acc, acc)
        return fold_in(c_diag, s_b[...], m_acc, acc)[1]

    def even_tail(m_acc, acc):
        return fold_in(c_diag, s_a[...], m_acc, acc)[1]

    acc_s = lax.cond(c_diag % 2 == 1, odd_tail, even_tail, m_acc, acc_s)
    o_s = acc_s[:HEAD_DIM] / acc_s[HEAD_DIM:HEAD_DIM + 1]

    wk = tq + WINDOW
    w0 = pl.multiple_of(jnp.maximum(t0 - WINDOW, 0), tq)
    dpos = qpos - (w0 + lax.broadcasted_iota(jnp.int32, (wk, 1), 0))
    s_w = jnp.where(jnp.where(dpos >= 0, dpos, WINDOW) < WINDOW,
                    jnp.dot(kwn_ref[0, pl.ds(w0, wk), :], qp, preferred_element_type=_F32), NEG_INF)
    acc_w = weighted(s_w, jnp.max(s_w, axis=0, keepdims=True), vwt_ref[0, :, pl.ds(w0, wk)])
    o_w = acc_w[:HEAD_DIM] / acc_w[HEAD_DIM:HEAD_DIM + 1]

    outs = []
    for r in range(GROUP_SIZE):
        sl = slice(r * tq, (r + 1) * tq)
        gate = lambda branch: gt_ref[0, pl.ds(branch * N_HEADS + g * GROUP_SIZE + r, 1), :]
        outs.append(gate(0) * o_c[:, sl] + gate(1) * o_s[:, sl] + gate(2) * o_w[:, sl])
    o_ref[0] = jnp.concatenate(outs, axis=0).T


def _nsa_prompt(q_t, gates_t, kcmp3, kcmp_t, ks_n, ks_tb, kw_n, kw_tb, tq, kc_len):
    bsz, _, t = q_t.shape
    ncr = kcmp3.shape[2]
    n_blk = t // SLC_BLOCK
    mt = jnp.tile(_importance_map(ncr, n_blk, ncr - 1), (1, 3)).astype(_BF16)
    rows_n = pl.BlockSpec((1, t, KV_ROW), lambda b, g, i: (b, 0, 0))
    vals_t = pl.BlockSpec((1, HEAD_DIM, t), lambda b, g, i: (b, N_KV_HEADS + g, 0))
    qw = GROUP_SIZE * HEAD_DIM
    return pl.pallas_call(
        functools.partial(_nsa_prompt_kernel, tq=tq, kc_len=kc_len, n_cmp=ncr - 1, n_blk=n_blk),
        grid=(bsz, N_KV_HEADS, t // tq),
        in_specs=[pl.BlockSpec((1, qw, tq), lambda b, g, i: (b, g, i)),
                  pl.BlockSpec((1, GATE_ROWS, tq), lambda b, g, i: (b, 0, i)),
                  pl.BlockSpec((1, 1, ncr, KV_SLOTS * HEAD_DIM), lambda b, g, i: (b, g, 0, 0)),
                  pl.BlockSpec((1, 1, HEAD_DIM, ncr), lambda b, g, i: (b, N_KV_HEADS + g, 0, 0)),
                  rows_n, vals_t, rows_n, vals_t,
                  pl.BlockSpec(mt.shape, lambda b, g, i: (0, 0))],
        out_specs=pl.BlockSpec((1, tq, qw), lambda b, g, i: (b, i, g)),
        out_shape=jax.ShapeDtypeStruct((bsz, t, D_ATTN), _F32),
        scratch_shapes=[pltpu.VMEM((n_blk, GROUP_SIZE * tq), _F32),
                        pltpu.VMEM((kc_len, GROUP_SIZE * tq), _F32), pltpu.VMEM((kc_len, GROUP_SIZE * tq), _F32)],
        compiler_params=_params("parallel", "parallel", "parallel"),
        name="nsa_prompt",
    )(q_t, gates_t, kcmp3, kcmp_t, ks_n, ks_tb, kw_n, kw_tb, mt)


def _nsa_sample_kernel(pt_ref, q_ref, gt_ref, kcmp_ref, kcmpt_ref, slc_hbm, ksn_ref, win_ref, kwt_ref,
                       m_ref, e_ref, o_ref, nw_ref, buf, sem,
                       *, n_pages, past, n_cmp, blk_pad, n_buf):
    b = pl.program_id(0)
    slot = _gather_pages(pt_ref, slc_hbm, buf, sem, n_pages)
    q = q_ref[0]
    gt = gt_ref[0]
    ks_new = ksn_ref[0]
    nseq = kwt_ref.shape[1]
    seq_lane = lax.broadcasted_iota(jnp.int32, (KV_ROW, nseq), 1)
    kw_col = jnp.sum(jnp.where(seq_lane == b, kwt_ref[...], 0.0), axis=1, keepdims=True)
    win = win_ref[0]
    pos = lax.broadcasted_iota(jnp.int32, win.shape, 1)
    band = jnp.where(pos == n_buf - 1, kw_col, pltpu.roll(win, n_buf - 1, 1))
    nw_ref[0] = band
    band_b = band.astype(_BF16)
    ncr = kcmp_ref.shape[2]
    lane = lax.broadcasted_iota(jnp.int32, (SUBLANES, LANES), 1)
    hrow = lax.broadcasted_iota(jnp.int32, (SUBLANES, LANES), 0)
    q8s, o_cs, sels = [], [], []
    for g in range(N_KV_HEADS):
        q8 = jnp.concatenate([q[:, (g * GROUP_SIZE + r) * HEAD_DIM:(g * GROUP_SIZE + r + 1) * HEAD_DIM]
                              for r in range(GROUP_SIZE)]
                             + [jnp.zeros((SUBLANES - GROUP_SIZE, HEAD_DIM), _F32)], axis=0)
        q8s.append(q8)
        q_hi = q8.astype(_BF16)
        q_lo = (q8 - q_hi.astype(_F32)).astype(_BF16)
        q3 = jnp.concatenate([q_hi, q_hi, q_lo, jnp.zeros_like(q_hi)], axis=1)
        s_c = lax.dot_general(q3, kcmp_ref[0, g], _NT, preferred_element_type=_F32)
        n_idx = lax.broadcasted_iota(jnp.int32, (SUBLANES, ncr), 1)
        cmp_end = jnp.where(n_idx < n_cmp, n_idx * CMP_STRIDE + (CMP_BLOCK - 1), jnp.int32(2 ** 30))
        p_c = _softmax(s_c, cmp_end <= past, 1)
        o_cs.append(_bdot_nt(p_c, kcmpt_ref[0, N_KV_HEADS + g]))

        imp = jnp.sum(p_c[:GROUP_SIZE], axis=0, keepdims=True)
        imp8 = jnp.broadcast_to(imp, (SUBLANES, ncr))
        imp_hi = imp8.astype(_BF16)
        imp_r = imp8 - imp_hi.astype(_F32)
        imp_mid = imp_r.astype(_BF16)
        imp_lo = (imp_r - imp_mid.astype(_F32)).astype(_BF16)
        imp_s = jnp.dot(jnp.concatenate([imp_hi, imp_mid, imp_lo], axis=1), m_ref[...],
                        preferred_element_type=_F32)[0:1]
        j_idx = lax.broadcasted_iota(jnp.int32, (1, blk_pad), 1)
        valid = j_idx * SLC_BLOCK <= past
        forced = (j_idx == 0) | (valid & (j_idx > past // SLC_BLOCK - N_LOCAL))
        score = jnp.where(forced, FORCED_SCORE, jnp.where(valid, imp_s, -1.0))
        s_k = jnp.broadcast_to(score, (blk_pad, blk_pad))
        s_j = s_k.T
        jj = lax.broadcasted_iota(jnp.int32, (blk_pad, blk_pad), 0)
        kk = lax.broadcasted_iota(jnp.int32, (blk_pad, blk_pad), 1)
        beats = jnp.where(kk > jj, jnp.where(s_j >= s_k, 1.0, 0.0), jnp.where(s_j > s_k, 1.0, 0.0))
        rank = jnp.sum(beats, axis=0, keepdims=True)
        sels.append(jnp.where(valid & (rank < N_SELECT), 1.0, 0.0))
    row_g = lax.broadcasted_iota(jnp.int32, (SUBLANES, blk_pad), 0)
    sel_rows = jnp.zeros((SUBLANES, blk_pad), _F32)
    for g in range(N_KV_HEADS):
        sel_rows = jnp.where(row_g == g, sels[g], sel_rows)
    picked = jnp.dot(sel_rows.astype(_BF16), e_ref[...], preferred_element_type=_F32)

    heads = []
    for g in range(N_KV_HEADS):
        q8 = q8s[g]
        q8b = q8.astype(_BF16)
        k_rows = slice(g * HEAD_DIM, (g + 1) * HEAD_DIM)
        v_rows = slice((N_KV_HEADS + g) * HEAD_DIM, (N_KV_HEADS + g + 1) * HEAD_DIM)
        s_s = jnp.where(picked[g:g + 1, :] > 0.5, _bdot(q8b, buf[slot, k_rows, :]), NEG_INF)
        s_n = jnp.sum(q8 * ks_new[:, k_rows], axis=-1, keepdims=True)
        m = jnp.maximum(jnp.max(s_s, axis=-1, keepdims=True), s_n)
        e_p, e_n = jnp.exp(s_s - m), jnp.exp(s_n - m)
        num = _bdot_nt(e_p, buf[slot, v_rows, :])
        o_s = (num + e_n * ks_new[:, v_rows]) / (jnp.sum(e_p, axis=-1, keepdims=True) + e_n)

        s_w = jnp.dot(q8b, band_b[k_rows, :], preferred_element_type=_F32)
        i_idx = lax.broadcasted_iota(jnp.int32, (SUBLANES, n_buf), 1)
        p_w = _softmax(s_w, (n_buf - 1 - i_idx) < WINDOW, 1)
        o_w = _bdot_nt(p_w, band_b[v_rows, :])

        def gate(branch):
            tgt = branch * N_HEADS + g * GROUP_SIZE + hrow
            return jnp.sum(jnp.where(lane == tgt, jnp.broadcast_to(gt, (SUBLANES, LANES)), 0.0),
                           axis=-1, keepdims=True)

        o = gate(0) * o_cs[g] + gate(1) * o_s + gate(2) * o_w
        heads += [o[r:r + 1, :] for r in range(GROUP_SIZE)]
    o_ref[0] = jnp.concatenate(heads, axis=1)


def _expand_table(past, blk_pad):
    return (jnp.arange(past)[None, :] // SLC_BLOCK == jnp.arange(blk_pad)[:, None]).astype(_BF16)


def _nsa_sample(q, gates, kcmp, kcmp_t, cache_t, page_table, ks_new, win_t, kw_t):
    nseq, n_pages = page_table.shape
    past = n_pages * PAGE_SIZE
    ncr = kcmp.shape[2]
    n_buf = win_t.shape[2]
    n_blk = past // SLC_BLOCK + 1
    blk_pad = -(-n_blk // LANES) * LANES
    m_tab = jnp.tile(_importance_map(ncr, blk_pad, ncr - 1).T, (3, 1)).astype(_BF16)
    e = _expand_table(past, blk_pad)
    row = lambda w: pl.BlockSpec((1, 1, w), lambda b, pt: (b, 0, 0))
    const = lambda a: pl.BlockSpec(a.shape, lambda b, pt: (0,) * a.ndim)
    win_spec = pl.BlockSpec((1, KV_ROW, n_buf), lambda b, pt: (b, 0, 0))
    grid_spec = pltpu.PrefetchScalarGridSpec(
        num_scalar_prefetch=1,
        grid=(nseq,),
        in_specs=[row(D_ATTN), row(LANES),
                  pl.BlockSpec((1, N_KV_HEADS, ncr, KV_SLOTS * HEAD_DIM), lambda b, pt: (b, 0, 0, 0)),
                  pl.BlockSpec((1, KV_SLOTS, HEAD_DIM, ncr), lambda b, pt: (b, 0, 0, 0)),
                  pl.BlockSpec(memory_space=pl.ANY),
                  row(KV_ROW), win_spec, const(kw_t), const(m_tab), const(e)],
        out_specs=[row(D_ATTN), win_spec],
        scratch_shapes=[pltpu.VMEM((2, KV_ROW, past), _F32), pltpu.SemaphoreType.DMA((2,))],
    )
    return pl.pallas_call(
        functools.partial(_nsa_sample_kernel, n_pages=n_pages, past=past, n_cmp=ncr - 1,
                          blk_pad=blk_pad, n_buf=n_buf),
        grid_spec=grid_spec,
        out_shape=[jax.ShapeDtypeStruct((nseq, 1, D_ATTN), _F32),
                   jax.ShapeDtypeStruct((nseq, KV_ROW, n_buf), _F32)],
        compiler_params=_params("arbitrary"),
        name="nsa_sample",
    )(page_table, q, gates, kcmp, kcmp_t, cache_t, ks_new, win_t, kw_t, m_tab, e)


def _post_kernel(x_ref, y_ref, u_ref, oa_ref, mod_ref, d_ref, wglu_ref, gs_ref, ga_ref, wout_ref, gm_ref,
                 wup_ref, wdn_ref, gf_ref, o_ref, *, ff_chunk):
    z = _gelu_tanh(y_ref[0] + d_ref[...] * u_ref[0])
    o_ssm = z * _sigmoid(jnp.dot(z.astype(_BF16), wglu_ref[...], preferred_element_type=_F32))
    mix = (jnp.dot(_rms(o_ssm, gs_ref[...]).astype(_BF16), wout_ref[:D_SSM, :], preferred_element_type=_F32)
           + jnp.dot(_rms(oa_ref[0], ga_ref[...]).astype(_BF16), wout_ref[D_SSM:, :], preferred_element_type=_F32))
    x1 = x_ref[0] + mod_ref[0, 2] * mix
    h2 = (_rms(x1, gm_ref[...]) * (1.0 + mod_ref[0, 4]) + mod_ref[0, 3]).astype(_BF16)
    acc = jnp.zeros_like(x1)
    for k in range(D_FF // ff_chunk):
        cols = slice(k * ff_chunk, (k + 1) * ff_chunk)
        hid = jnp.maximum(jnp.dot(h2, wup_ref[:, cols], preferred_element_type=_F32), 0.0)
        acc = acc + jnp.dot((hid * hid).astype(_BF16), wdn_ref[cols, :], preferred_element_type=_F32)
    o_ref[0] = _rms(x1 + mod_ref[0, 5] * acc, gf_ref[...])


def _post(x, y_ssm, u, o_attn, mod, lp, norm_final, tm):
    bsz, t, _ = x.shape
    r = mod.shape[2]
    rb = 1 if r == 1 else tm
    mod_map = (lambda b, i: (b, 0, 0, 0)) if r == 1 else (lambda b, i: (b, 0, i, 0))
    tok = lambda w: pl.BlockSpec((1, tm, w), lambda b, i: (b, i, 0))
    const = lambda shape: pl.BlockSpec(shape, lambda b, i: (0,) * len(shape), pipeline_mode=pl.Buffered(1))
    vec = lambda v: v.reshape(1, -1).astype(_F32)
    weights = [vec(lp['ssm_d']), lp['ssm_w_glu'].astype(_BF16), vec(lp['norm_out_ssm']), vec(lp['norm_out_attn']),
               lp['w_out'].astype(_BF16), vec(lp['norm_mlp']), lp['w_up'].astype(_BF16), lp['w_down'].astype(_BF16),
               vec(norm_final)]
    return pl.pallas_call(
        functools.partial(_post_kernel, ff_chunk=D_MODEL),
        grid=(bsz, t // tm),
        in_specs=[tok(D_MODEL), tok(D_SSM), tok(D_SSM), tok(D_ATTN),
                  pl.BlockSpec((1, 6, rb, D_MODEL), mod_map)] + [const(w.shape) for w in weights],
        out_specs=tok(D_MODEL),
        out_shape=jax.ShapeDtypeStruct((bsz, t, D_MODEL), _F32),
        compiler_params=_params("parallel", "parallel"),
        name="post_mlp",
    )(x, y_ssm, u, o_attn, mod, *weights)


def _feature_major(a):
    lead = a.shape[:-4]
    n = len(lead)
    return a.transpose(*range(n), n + 1, n + 2, n + 3, n).reshape(*lead, KV_ROW, a.shape[-4])


def _kv_output(a_t):
    n, _, t = a_t.shape
    return a_t.reshape(n, 2, N_KV_HEADS, HEAD_DIM, t).transpose(0, 4, 1, 2, 3)[None]


def kernel(x_prompt, x_sample, cache_cmp, cache_slc, state_win, state_ssm_re, state_ssm_im, page_table,
           c_prompt, c_sample, w_ada, b_ada, norm_attn, w_in, ssm_lambda_re, ssm_lambda_im, ssm_log_dt,
           ssm_b_re, ssm_b_im, ssm_c_re, ssm_c_im, ssm_d, ssm_w_glu, cmp_pe_k, cmp_w1_k, cmp_w2_k,
           cmp_pe_v, cmp_w1_v, cmp_w2_v, norm_out_ssm, norm_out_attn, w_out, norm_mlp, w_up, w_down,
           norm_final):
    depth = w_ada.shape[0]
    assert depth == 1, "single-layer trunk"
    l = 0
    bsz, t, _ = x_prompt.shape
    nseq = x_sample.shape[0]
    assert x_sample.shape[1] == 1
    tm = min(512, t)
    tq = min(256, t)
    kc_len = min(512, t)

    lp = dict(ssm_d=ssm_d[l], ssm_w_glu=ssm_w_glu[l], norm_out_ssm=norm_out_ssm[l],
              norm_out_attn=norm_out_attn[l], w_out=w_out[l], norm_mlp=norm_mlp[l], w_up=w_up[l], w_down=w_down[l])
    w_full = jnp.pad(w_in[l], ((0, 0), (0, D_IN_PAD - D_IN))).astype(_BF16)
    w_rows = jnp.concatenate([w_full[:, _Q0:_G0 + 3 * N_HEADS].T,
                              jnp.zeros((GATE_ROWS - 3 * N_HEADS, D_MODEL), _BF16)], axis=0)
    tq0, tkc0, tks0, tkw0, tg0 = 0, D_ATTN, D_ATTN + KV_ROW, D_ATTN + 2 * KV_ROW, D_ATTN + 3 * KV_ROW
    tabs = _ssm_tables(ssm_lambda_re[l], ssm_lambda_im[l], ssm_log_dt[l], ssm_b_re[l], ssm_b_im[l],
                       ssm_c_re[l], ssm_c_im[l])
    ctabs = _compress_tables(cmp_pe_k[l], cmp_w1_k[l], cmp_w2_k[l], cmp_pe_v[l], cmp_w1_v[l], cmp_w2_v[l])

    mod = _ada_mod(jnp.concatenate([c_prompt, c_sample], axis=0), w_ada[l], b_ada[l])
    mod_p = mod[:bsz].reshape(bsz, 6, 1, D_MODEL)
    mod_s = mod[bsz:].reshape(nseq, 6, D_MODEL).transpose(1, 0, 2)[None]

    w_tok = jnp.concatenate([w_full[:, :D_SSM], w_full[:, _KC0:_G0]], axis=1)
    outs_p = [('n', 0, D_SSM, None), ('n', D_SSM, D_SSM + KV_ROW, None),
              ('n', D_SSM + KV_ROW, D_SSM + 2 * KV_ROW, None), ('n', D_SSM + 2 * KV_ROW, D_SSM + 3 * KV_ROW, None),
              ('t', tq0, tkc0, 'scale'), ('t', tkc0, tks0, None), ('t', tks0, tkw0, None), ('t', tkw0, tg0, None),
              ('t', tks0, tkw0, None), ('t', tkw0, tg0, None), ('t', tg0, tg0 + GATE_ROWS, 'sigmoid')]
    dt_p = [_F32, _F32, _BF16, _BF16, _F32, _F32, _F32, _F32, _BF16, _BF16, _F32]
    (u, kc_n, ks_n, kw_n, q_t, kc_t, ks_t, kw_t, ks_tb, kw_tb, gates_t) = _in_proj(
        x_prompt, mod_p, norm_attn[l], w_tok, w_rows, tm, outs_p, dt_p)
    y_ssm, sp_re, sp_im = _ssm_prompt(u, tabs)
    kcmp, kcmp_t = _compress_prompt(kc_n, ctabs)
    o_attn = _nsa_prompt(q_t, gates_t, kcmp, kcmp_t, ks_n, ks_tb, kw_n, kw_tb, tq, kc_len)
    y_prompt = _post(x_prompt, y_ssm, u, o_attn, mod_p, lp, norm_final, tm)
    w_keep = min(WINDOW, t)

    xs = x_sample.reshape(1, nseq, D_MODEL)
    outs_s = [('n', 0, D_SSM, None), ('n', _Q0, _KC0, 'scale'), ('n', _KS0, _KW0, None), ('n', _G0, D_IN_PAD, 'sigmoid'),
              ('t', tkc0, tks0, None), ('t', tks0, tkw0, None), ('t', tkw0, tg0, None)]
    u_s, q_s, ks_row, gates_s, kc_ts, ks_ts, kw_ts = _in_proj(
        xs, mod_s, norm_attn[l], w_full, w_rows, nseq, outs_s, [_F32] * len(outs_s))
    ns = N_SSM_GROUPS * SSM_STATE
    y_s, ss_re, ss_im = _ssm_step(u_s[0], state_ssm_re[l].reshape(nseq, ns), state_ssm_im[l].reshape(nseq, ns), tabs)
    kcmp_s, kcmp_ts = _compress_paged(_feature_major(cache_cmp[l]), page_table, ctabs)
    o_attn_s, new_win_t = _nsa_sample(q_s.reshape(nseq, 1, D_ATTN), gates_s.reshape(nseq, 1, LANES), kcmp_s, kcmp_ts,
                                      _feature_major(cache_slc[l]), page_table, ks_row.reshape(nseq, 1, KV_ROW),
                                      _feature_major(state_win[l]), kw_ts[0])
    y_sample = _post(xs, y_s[None], u_s, o_attn_s.reshape(1, nseq, D_ATTN), mod_s, lp, norm_final, nseq)
    st_shape = (1, nseq, N_SSM_GROUPS, SSM_STATE)
    new_row = lambda a_t: _kv_output(a_t[0].T[:, :, None])

    return (y_prompt, y_sample.reshape(nseq, 1, D_MODEL),
            _kv_output(kc_t), _kv_output(ks_t), _kv_output(kw_t[:, :, t - w_keep:]),
            sp_re[None], sp_im[None],
            new_row(kc_ts), new_row(ks_ts), _kv_output(new_win_t),
            ss_re.reshape(st_shape), ss_im.reshape(st_shape))
```

```python
import functools
import math

import jax
import jax.numpy as jnp
from jax import lax
from jax.experimental import pallas as pl
from jax.experimental.pallas import tpu as pltpu

D_MODEL = 1024
D_SSM = D_MODEL // 2
SSM_GROUP = 16
N_SSM_GROUPS = D_SSM // SSM_GROUP
SSM_STATE = 64
HEAD_DIM = 64
D_ATTN = D_MODEL - D_SSM
N_HEADS = D_ATTN // HEAD_DIM
N_KV_HEADS = 2
GROUP_SIZE = N_HEADS // N_KV_HEADS
D_KV = N_KV_HEADS * HEAD_DIM
CMP_BLOCK = 32
CMP_STRIDE = 16
CMP_HIDDEN = 2 * HEAD_DIM
SLC_BLOCK = 64
N_SELECT = 16
N_LOCAL = 2
WINDOW = 512
D_FF = 4 * D_MODEL
D_IN = D_SSM + D_ATTN + 6 * D_KV + 3 * N_HEADS
EPS = 1e-6
PAGE_SIZE = 128

LANES = 128
SUBLANES = 8
D_IN_PAD = -(-D_IN // LANES) * LANES
KV_SLOTS = 2 * N_KV_HEADS
KV_ROW = KV_SLOTS * HEAD_DIM
GATE_ROWS = 32
SSM_CHUNK = 16
SSM_LANE_GROUPS = LANES // SSM_GROUP
VMEM_LIMIT = 56 * 1024 * 1024
FORCED_SCORE = 1e30
NEG_INF = float("-inf")
LOG2_E = math.log2(math.e)
RELAYOUT_UNROLL = 8

_Q0, _KC0, _KS0, _KW0, _G0 = D_SSM, D_MODEL, D_MODEL + KV_ROW, D_MODEL + 2 * KV_ROW, D_MODEL + 3 * KV_ROW

_BF16 = jnp.bfloat16
_F32 = jnp.float32
_NT = (((1,), (1,)), ((), ()))
_HIGHEST = lax.Precision.HIGHEST


def _params(*sem):
    return pltpu.CompilerParams(dimension_semantics=sem, vmem_limit_bytes=VMEM_LIMIT)


def _rms(x, g):
    return x * lax.rsqrt(jnp.mean(x * x, axis=-1, keepdims=True) + EPS) * g


def _gelu_tanh(x):
    return x * (0.5 * (1.0 + jnp.tanh(math.sqrt(2.0 / math.pi) * (x + 0.044715 * (x * x * x)))))


def _sigmoid(x):
    return 1.0 / (1.0 + jnp.exp(-x))


def _bdot(a, b):
    return jnp.dot(a.astype(_BF16), b.astype(_BF16), preferred_element_type=_F32)


def _bdot_nt(a, b):
    return lax.dot_general(a.astype(_BF16), b.astype(_BF16), _NT, preferred_element_type=_F32)


def _ada_kernel(c_ref, w_ref, b_ref, o_ref):
    c = c_ref[...]
    o_ref[...] = jnp.dot(c * _sigmoid(c), w_ref[...], precision=_HIGHEST,
                         preferred_element_type=_F32) + b_ref[...]


def _ada_mod(c, w_ada, b_ada):
    n, tn = c.shape[0], D_MODEL
    return pl.pallas_call(
        _ada_kernel,
        grid=(6 * D_MODEL // tn,),
        in_specs=[pl.BlockSpec((n, D_MODEL), lambda j: (0, 0)),
                  pl.BlockSpec((D_MODEL, tn), lambda j: (0, j)),
                  pl.BlockSpec((1, tn), lambda j: (0, j))],
        out_specs=pl.BlockSpec((n, tn), lambda j: (0, j)),
        out_shape=jax.ShapeDtypeStruct((n, 6 * D_MODEL), _F32),
        compiler_params=_params("parallel"),
        name="ada_mod",
    )(c, w_ada, b_ada.reshape(1, -1))


def _inproj_kernel(x_ref, mod_ref, g_ref, wn_ref, wt_ref, *out_refs, outs):
    x = x_ref[0]
    h = (_rms(x, g_ref[...]) * (1.0 + mod_ref[0, 1]) + mod_ref[0, 0]).astype(_BF16)
    pn = jnp.dot(h, wn_ref[...], preferred_element_type=_F32)
    pt = lax.dot_general(wt_ref[...], h, _NT, preferred_element_type=_F32)
    for ref, (kind, lo, hi, post) in zip(out_refs, outs):
        v = pn[:, lo:hi] if kind == 'n' else pt[lo:hi, :]
        if post == 'scale':
            v = v * (HEAD_DIM ** -0.5)
        elif post == 'sigmoid':
            v = _sigmoid(v)
        ref[0] = v.astype(ref.dtype)


def _in_proj(x, mod, norm_attn, wn, wt, tm, outs, dtypes):
    bsz, t, _ = x.shape
    r = mod.shape[2]
    rb = 1 if r == 1 else tm
    mod_map = (lambda b, i: (b, 0, 0, 0)) if r == 1 else (lambda b, i: (b, 0, i, 0))
    out_specs, out_shape = [], []
    for (kind, lo, hi, _), dt in zip(outs, dtypes):
        if kind == 'n':
            out_specs.append(pl.BlockSpec((1, tm, hi - lo), lambda b, i: (b, i, 0)))
            out_shape.append(jax.ShapeDtypeStruct((bsz, t, hi - lo), dt))
        else:
            out_specs.append(pl.BlockSpec((1, hi - lo, tm), lambda b, i: (b, 0, i)))
            out_shape.append(jax.ShapeDtypeStruct((bsz, hi - lo, t), dt))
    return pl.pallas_call(
        functools.partial(_inproj_kernel, outs=tuple(outs)),
        grid=(bsz, t // tm),
        in_specs=[pl.BlockSpec((1, tm, D_MODEL), lambda b, i: (b, i, 0)),
                  pl.BlockSpec((1, 6, rb, D_MODEL), mod_map),
                  pl.BlockSpec((1, D_MODEL), lambda b, i: (0, 0)),
                  pl.BlockSpec(wn.shape, lambda b, i: (0, 0)),
                  pl.BlockSpec(wt.shape, lambda b, i: (0, 0))],
        out_specs=out_specs,
        out_shape=out_shape,
        compiler_params=_params("parallel", "parallel"),
        name="in_proj",
    )(x, mod, norm_attn.reshape(1, -1), wn, wt)


def _ssm_tables(lam_re, lam_im, log_dt, b_re, b_im, c_re, c_im):
    hp = dict(precision=_HIGHEST)
    lr, li = lam_re.astype(_F32), lam_im.astype(_F32)
    dt = jnp.exp(log_dt.astype(_F32))[:, None]
    ar, ai = lr * dt, li * dt

    def power(k):
        mag = jnp.exp(ar * k)
        return mag * jnp.cos(ai * k), mag * jnp.sin(ai * k)

    n = SSM_CHUNK
    steps = jnp.arange(n + 1, dtype=_F32)[:, None, None]
    pwr, pwi = power(steps)
    lbr, lbi = pwr[1], pwi[1]
    den = lr * lr + li * li
    fr = ((lbr - 1.0) * lr + lbi * li) / den
    fi = (lbi * lr - (lbr - 1.0) * li) / den
    br_, bi_ = b_re.astype(_F32), b_im.astype(_F32)
    bbr = fr[:, :, None] * br_ - fi[:, :, None] * bi_
    bbi = fr[:, :, None] * bi_ + fi[:, :, None] * br_
    cr, ci = c_re.astype(_F32), c_im.astype(_F32)
    er = pwr[:, :, :, None] * bbr[None] - pwi[:, :, :, None] * bbi[None]
    ei = pwr[:, :, :, None] * bbi[None] + pwi[:, :, :, None] * bbr[None]
    kern = (jnp.einsum('gip,dgpj->dgij', cr, er[:n], **hp)
            - jnp.einsum('gip,dgpj->dgij', ci, ei[:n], **hp))
    nb, gpb = N_SSM_GROUPS // SSM_LANE_GROUPS, SSM_LANE_GROUPS

    def expand(a, g_axis):
        w = a.shape[-1]
        owner = jnp.arange(gpb * w) // w
        own = jnp.arange(gpb).reshape((gpb,) + (1,) * (a.ndim - 1 - g_axis))
        return jnp.where(owner == own, jnp.tile(a, (1,) * (a.ndim - 1) + (gpb,)), 0.0)

    lag_tiles = kern.reshape(n, nb, gpb, SSM_GROUP, SSM_GROUP).transpose(1, 0, 2, 4, 3)
    lag_tiles = expand(lag_tiles, 2).reshape(nb, n, LANES, LANES)
    rev = n - 1 - jnp.arange(n)

    def state_in(e):
        w = e[rev].reshape(n, nb, gpb, SSM_STATE, SSM_GROUP).transpose(1, 0, 2, 4, 3)
        return expand(w, 2).reshape(nb, n * LANES, gpb * SSM_STATE)

    w_in = jnp.concatenate([state_in(er), state_in(ei)], axis=-1)
    cvr = cr[None] * pwr[1:, :, None, :] - ci[None] * pwi[1:, :, None, :]
    cvi = cr[None] * pwi[1:, :, None, :] + ci[None] * pwr[1:, :, None, :]

    def state_out(cv):
        v = cv.reshape(n, nb, gpb, SSM_GROUP, SSM_STATE).transpose(1, 0, 2, 4, 3)
        return expand(v, 2).reshape(nb, n, gpb * SSM_STATE, LANES)

    v_out = jnp.concatenate([state_out(cvr), -state_out(cvi)], axis=2)
    a_chunk = jnp.stack([pwr[n].reshape(nb, -1), pwi[n].reshape(nb, -1)], axis=1)
    eye = jnp.eye(N_SSM_GROUPS, dtype=_F32)
    ns = N_SSM_GROUPS * SSM_STATE
    bd = lambda a: jnp.einsum('gpi,gh->gihp', a, eye).reshape(D_SSM, ns)
    cd = lambda a: jnp.einsum('gip,gh->gphi', a, eye).reshape(ns, D_SSM)
    return dict(toep=lag_tiles.astype(_BF16), w_in=w_in.astype(_BF16), v_out=v_out.astype(_BF16), a_chunk=a_chunk,
                bd_re=bd(bbr).astype(_BF16), bd_im=bd(bbi).astype(_BF16),
                cd_re=cd(cr).astype(_BF16), cd_im=cd(ci).astype(_BF16),
                lam_re=lbr.reshape(1, ns), lam_im=lbi.reshape(1, ns))


def _ssm_prompt_kernel(u_ref, toep_ref, w_ref, v_ref, a_ref, y_ref, s_ref, x_scr, sin_scr, *, n_chunks):
    n = SSM_CHUNK
    lhs = jnp.concatenate([u_ref[0, pl.ds(tau, n_chunks, stride=n), :] for tau in range(n)],
                          axis=1).astype(_BF16)
    x_scr[...] = jnp.dot(lhs, w_ref[0], preferred_element_type=_F32)
    half = x_scr.shape[1] // 2
    a_re = a_ref[0, 0:1, :]
    a_im = a_ref[0, 1:2, :]

    def step(c, carry):
        s_re, s_im = carry
        row = pl.ds(c, 1)
        sin_scr[row, :half] = s_re
        sin_scr[row, half:] = s_im
        x = x_scr[row, :]
        return (a_re * s_re - a_im * s_im + x[:, :half],
                a_re * s_im + a_im * s_re + x[:, half:])

    zero = jnp.zeros((1, half), _F32)
    s_re, s_im = lax.fori_loop(0, n_chunks, step, (zero, zero))
    s_ref[0, 0] = jnp.concatenate([s_re, s_im], axis=1)
    sin_b = sin_scr[...].astype(_BF16)
    zero_tile = jnp.zeros((LANES, LANES), _BF16)
    for t2 in range(n // 2):
        t_lo = 2 * t2
        k_hi = (t_lo + 2) * LANES
        w_lag = jnp.concatenate(
            [jnp.concatenate([toep_ref[0, t_lo - tau] if tau <= t_lo else zero_tile, toep_ref[0, t_lo + 1 - tau]],
                             axis=1) for tau in range(t_lo + 2)], axis=0)
        w_state = jnp.concatenate([v_ref[0, t_lo], v_ref[0, t_lo + 1]], axis=1)
        y = (jnp.dot(lhs[:, :k_hi], w_lag, preferred_element_type=_F32)
             + jnp.dot(sin_b, w_state, preferred_element_type=_F32))
        for d in range(2):
            y_ref[0, pl.ds(t_lo + d, n_chunks, stride=n), :] = y[:, d * LANES:(d + 1) * LANES]


def _ssm_prompt(u, tabs):
    bsz, t, _ = u.shape
    nch = t // SSM_CHUNK
    nb = D_SSM // LANES
    sw = 2 * SSM_LANE_GROUPS * SSM_STATE
    table = lambda a: pl.BlockSpec((1,) + a.shape[1:], lambda m, b: (m,) + (0,) * (a.ndim - 1),
                                   pipeline_mode=pl.Buffered(1))
    tok = pl.BlockSpec((1, t, LANES), lambda m, b: (b, 0, m))
    y, s = pl.pallas_call(
        functools.partial(_ssm_prompt_kernel, n_chunks=nch),
        grid=(nb, bsz),
        in_specs=[tok, table(tabs['toep']), table(tabs['w_in']), table(tabs['v_out']),
                  pl.BlockSpec((1, 2, sw // 2), lambda m, b: (m, 0, 0))],
        out_specs=[tok, pl.BlockSpec((1, 1, 1, sw), lambda m, b: (b, m, 0, 0))],
        out_shape=[jax.ShapeDtypeStruct((bsz, t, D_SSM), _F32),
                   jax.ShapeDtypeStruct((bsz, nb, 1, sw), _F32)],
        scratch_shapes=[pltpu.VMEM((nch, sw), _F32), pltpu.VMEM((nch, sw), _F32)],
        compiler_params=_params("parallel", "parallel"),
        name="ssm_prompt",
    )(u, tabs['toep'], tabs['w_in'], tabs['v_out'], tabs['a_chunk'])
    state = lambda a: a.reshape(bsz, N_SSM_GROUPS, SSM_STATE)
    return y, state(s[:, :, 0, :sw // 2]), state(s[:, :, 0, sw // 2:])


def _ssm_step_kernel(u_ref, s0re_ref, s0im_ref, lre_ref, lim_ref, bdre_ref, bdim_ref, cdre_ref, cdim_ref,
                     y_ref, sre_ref, sim_ref):
    u = u_ref[...]
    s_re, s_im = s0re_ref[...], s0im_ref[...]
    l_re, l_im = lre_ref[...], lim_ref[...]
    n_re = l_re * s_re - l_im * s_im + _bdot(u, bdre_ref[...])
    n_im = l_re * s_im + l_im * s_re + _bdot(u, bdim_ref[...])
    sre_ref[...] = n_re
    sim_ref[...] = n_im
    y_ref[...] = _bdot(n_re, cdre_ref[...]) - _bdot(n_im, cdim_ref[...])


def _ssm_step(u, s0_re, s0_im, tabs):
    n = u.shape[0]
    ns = N_SSM_GROUPS * SSM_STATE
    return pl.pallas_call(
        _ssm_step_kernel,
        out_shape=[jax.ShapeDtypeStruct((n, D_SSM), _F32),
                   jax.ShapeDtypeStruct((n, ns), _F32),
                   jax.ShapeDtypeStruct((n, ns), _F32)],
        compiler_params=pltpu.CompilerParams(vmem_limit_bytes=VMEM_LIMIT),
        name="ssm_step",
    )(u, s0_re, s0_im, tabs['lam_re'], tabs['lam_im'], tabs['bd_re'], tabs['bd_im'], tabs['cd_re'], tabs['cd_im'])


def _compress_tables(pe_k, w1_k, w2_k, pe_v, w1_v, w2_v):
    zeros = jnp.zeros((CMP_STRIDE, HEAD_DIM, CMP_HIDDEN), _F32)
    cols = []
    for half in range(2):
        for slot in range(KV_SLOTS):
            w1 = (w1_k if slot < N_KV_HEADS else w1_v)[half * CMP_STRIDE:(half + 1) * CMP_STRIDE]
            blk = jnp.stack([w1 if s == slot else zeros for s in range(KV_SLOTS)], axis=1)
            cols.append(blk.reshape(CMP_STRIDE * KV_ROW, CMP_HIDDEN))
    w_all = jnp.concatenate(cols, axis=1).astype(_BF16)
    z2 = jnp.zeros((CMP_HIDDEN, HEAD_DIM), _F32)
    w2 = jnp.concatenate(
        [jnp.concatenate([(w2_k if s < N_KV_HEADS else w2_v) if s == slot else z2 for s in range(KV_SLOTS)], axis=1)
         for slot in range(KV_SLOTS)], axis=0).astype(_BF16)
    pe = jnp.stack([pe_k.reshape(-1), pe_v.reshape(-1)], axis=0)
    pe = jnp.concatenate([pe, jnp.zeros((SUBLANES - 2, pe.shape[1]), _F32)], axis=0)
    w1f = jnp.stack([w1_k.reshape(-1, CMP_HIDDEN), w1_v.reshape(-1, CMP_HIDDEN)], axis=0)
    halves = lambda w1: jnp.concatenate([w1[:CMP_STRIDE].reshape(-1, CMP_HIDDEN),
                                         w1[CMP_STRIDE:].reshape(-1, CMP_HIDDEN)], axis=1)
    w_slot = jnp.stack([halves(w1_k if s < N_KV_HEADS else w1_v) for s in range(KV_SLOTS)], axis=0).astype(_BF16)
    return dict(w_all=w_all, w_slot=w_slot, tail=(w2, w2.T, pe, w1f))


def _compress_bias(pe_ref, w1f_ref):
    pe = pe_ref[...]
    bias_k = jnp.dot(pe, w1f_ref[0], precision=_HIGHEST, preferred_element_type=_F32)[0:1]
    bias_v = jnp.dot(pe, w1f_ref[1], precision=_HIGHEST, preferred_element_type=_F32)[1:2]
    return jnp.concatenate([bias_k, bias_k, bias_v, bias_v], axis=1)


def _compress_tail(h, bias, w2_ref, w2t_ref, n_rows):
    hw = KV_SLOTS * CMP_HIDDEN
    nxt = pltpu.roll(h[:, hw:], n_rows - 1, 0)
    hid = _gelu_tanh(h[:, :hw] + nxt + bias)
    row = lax.broadcasted_iota(jnp.int32, hid.shape, 0)
    hid = jnp.where(row < n_rows - 1, hid, 0.0).astype(_BF16)
    return (jnp.dot(hid, w2_ref[...], preferred_element_type=_F32),
            lax.dot_general(w2t_ref[...], hid, _NT, preferred_element_type=_F32))


def _split3_keys(k):
    hi = k.astype(_BF16)
    lo = (k - hi.astype(_F32)).astype(_BF16)
    return jnp.concatenate([hi, lo, hi, jnp.zeros_like(hi)], axis=1)


def _split3_queries(q_t):
    hi = q_t.astype(_BF16)
    lo = (q_t - hi.astype(_F32)).astype(_BF16)
    return jnp.concatenate([hi, hi, lo, jnp.zeros_like(hi)], axis=0)


def _store_compressed(out, out_t, k3_ref, ot_ref):
    for g in range(N_KV_HEADS):
        k3_ref[0, g] = _split3_keys(out[:, g * HEAD_DIM:(g + 1) * HEAD_DIM])
    for s in range(KV_SLOTS):
        ot_ref[0, s] = out_t[s * HEAD_DIM:(s + 1) * HEAD_DIM, :]


def _compress_prompt_kernel(x_ref, wall_ref, w2_ref, w2t_ref, pe_ref, w1f_ref, k3_ref, ot_ref, *, n_rows):
    h = jnp.dot(x_ref[0].astype(_BF16), wall_ref[...], preferred_element_type=_F32)
    out, out_t = _compress_tail(h, _compress_bias(pe_ref, w1f_ref), w2_ref, w2t_ref, n_rows)
    _store_compressed(out, out_t, k3_ref, ot_ref)


def _compress_out(nseq, nc, index_map):
    specs = [pl.BlockSpec((1, N_KV_HEADS, nc, KV_SLOTS * HEAD_DIM), index_map),
             pl.BlockSpec((1, KV_SLOTS, HEAD_DIM, nc), index_map)]
    shapes = [jax.ShapeDtypeStruct((nseq, N_KV_HEADS, nc, KV_SLOTS * HEAD_DIM), _BF16),
              jax.ShapeDtypeStruct((nseq, KV_SLOTS, HEAD_DIM, nc), _F32)]
    return specs, shapes


def _compress_prompt(kv_cmp, ctabs):
    bsz, t, _ = kv_cmp.shape
    nc = t // CMP_STRIDE
    const = lambda a: pl.BlockSpec(a.shape, lambda b: (0,) * a.ndim)
    weights = (ctabs['w_all'],) + ctabs['tail']
    out_specs, out_shape = _compress_out(bsz, nc, lambda b: (b, 0, 0, 0))
    return pl.pallas_call(
        functools.partial(_compress_prompt_kernel, n_rows=nc),
        grid=(bsz,),
        in_specs=[pl.BlockSpec((1, nc, CMP_STRIDE * KV_ROW), lambda b: (b, 0, 0))] + [const(a) for a in weights],
        out_specs=out_specs,
        out_shape=out_shape,
        compiler_params=_params("parallel"),
        name="compress_prompt",
    )(kv_cmp.reshape(bsz, nc, CMP_STRIDE * KV_ROW), *weights)


def _page_copies(pt_ref, cache_hbm, buf, sem, seq, slot, n_pages):
    return [pltpu.make_async_copy(cache_hbm.at[pt_ref[seq, j]],
                                  buf.at[slot, :, pl.ds(j * PAGE_SIZE, PAGE_SIZE)],
                                  sem.at[slot]) for j in range(n_pages)]


def _gather_pages(pt_ref, cache_hbm, buf, sem, n_pages):
    b = pl.program_id(0)
    slot = lax.rem(b, 2)

    @pl.when(b == 0)
    def _():
        for cp in _page_copies(pt_ref, cache_hbm, buf, sem, 0, 0, n_pages):
            cp.start()

    @pl.when(b + 1 < pl.num_programs(0))
    def _():
        for cp in _page_copies(pt_ref, cache_hbm, buf, sem, b + 1, 1 - slot, n_pages):
            cp.start()

    for cp in _page_copies(pt_ref, cache_hbm, buf, sem, b, slot, n_pages):
        cp.wait()
    return slot


def _chunk_permutation():
    span = 2 * PAGE_SIZE
    r = jnp.arange(span)
    src = (r % (span // CMP_STRIDE)) * CMP_STRIDE + r // (span // CMP_STRIDE)
    return (src[:, None] == jnp.arange(span)[None, :]).astype(_BF16)


def _compress_paged_kernel(pt_ref, cache_hbm, perm_ref, wslot_ref, w2_ref, w2t_ref, pe_ref, w1f_ref,
                           o_ref, ot_ref, buf, sem, xr, bias_scr, *, n_pages, n_rows):
    slot = _gather_pages(pt_ref, cache_hbm, buf, sem, n_pages)

    @pl.when(pl.program_id(0) == 0)
    def _():
        bias_scr[...] = jnp.broadcast_to(_compress_bias(pe_ref, w1f_ref), bias_scr.shape)

    span = 2 * PAGE_SIZE
    cps = span // CMP_STRIDE
    low = lax.broadcasted_iota(jnp.int32, (cps, LANES), 1) < HEAD_DIM

    def relayout(i, carry):
        xt = buf[slot, :, pl.ds(pl.multiple_of(i * span, span), span)].astype(_BF16)
        rows = lax.dot_general(perm_ref[...], xt, _NT, preferred_element_type=_F32)
        chunk_rows = pl.ds(pl.multiple_of(i * cps, cps), cps)
        for a in range(CMP_STRIDE // 2):
            for s in range(KV_SLOTS):
                tile = slice((s // 2) * LANES, (s // 2 + 1) * LANES)
                even = rows[2 * a * cps:(2 * a + 1) * cps, tile]
                odd = rows[(2 * a + 1) * cps:(2 * a + 2) * cps, tile]
                if s % 2 == 0:
                    pair = jnp.where(low, even, pltpu.roll(odd, HEAD_DIM, 1))
                else:
                    pair = jnp.where(low, pltpu.roll(even, HEAD_DIM, 1), odd)
                xr[s, chunk_rows, a * LANES:(a + 1) * LANES] = pair.astype(_BF16)
        return carry

    lax.fori_loop(0, n_pages // 2, relayout, 0, unroll=math.gcd(RELAYOUT_UNROLL, n_pages // 2))
    hs = [jnp.dot(xr[s], wslot_ref[s], preferred_element_type=_F32) for s in range(KV_SLOTS)]
    h = jnp.concatenate([v[:, :CMP_HIDDEN] for v in hs] + [v[:, CMP_HIDDEN:] for v in hs], axis=1)
    out, out_t = _compress_tail(h, bias_scr[0:1, :], w2_ref, w2t_ref, n_rows)
    _store_compressed(out, out_t, o_ref, ot_ref)


def _compress_paged(cache_t, page_table, ctabs):
    nseq, n_pages = page_table.shape
    past = n_pages * PAGE_SIZE
    nc = past // CMP_STRIDE
    perm = _chunk_permutation()
    const = lambda a: pl.BlockSpec(a.shape, lambda b, pt: (0,) * a.ndim)
    out_specs, out_shape = _compress_out(nseq, nc, lambda b, pt: (b, 0, 0, 0))
    weights = (ctabs['w_slot'],) + ctabs['tail']
    grid_spec = pltpu.PrefetchScalarGridSpec(
        num_scalar_prefetch=1,
        grid=(nseq,),
        in_specs=[pl.BlockSpec(memory_space=pl.ANY), const(perm)] + [const(a) for a in weights],
        out_specs=out_specs,
        scratch_shapes=[pltpu.VMEM((2, KV_ROW, past), _F32), pltpu.SemaphoreType.DMA((2,)),
                        pltpu.VMEM((KV_SLOTS, nc, CMP_STRIDE * HEAD_DIM), _BF16),
                        pltpu.VMEM((SUBLANES, KV_SLOTS * CMP_HIDDEN), _F32)],
    )
    return pl.pallas_call(
        functools.partial(_compress_paged_kernel, n_pages=n_pages, n_rows=nc),
        grid_spec=grid_spec,
        out_shape=out_shape,
        compiler_params=_params("arbitrary"),
        name="compress_paged",
    )(page_table, cache_t, perm, *weights)


def _importance_map(n_cmp_rows, n_blocks_rows, n_cmp):
    ratio = SLC_BLOCK // CMP_STRIDE
    j = jnp.arange(n_blocks_rows)[:, None]
    n = jnp.arange(n_cmp_rows)[None, :]
    off = n - ratio * j
    w = jnp.where((off == -1) | (off == ratio - 1), 1.0, jnp.where((off >= 0) & (off < ratio - 1), 2.0, 0.0))
    return jnp.where(n < n_cmp, w, 0.0).astype(_F32)


def _softmax(s, ok, axis):
    s = jnp.where(ok, s, NEG_INF)
    m = jnp.max(s, axis=axis, keepdims=True)
    m = jnp.where(m == NEG_INF, 0.0, m)
    e = jnp.exp(s - m)
    den = jnp.sum(e, axis=axis, keepdims=True)
    return e / jnp.where(den > 0, den, 1.0)


def _nsa_prompt_kernel(q_ref, gt_ref, kc_ref, vct_ref, ksn_ref, vst_ref, kwn_ref, vwt_ref, mt_ref, o_ref,
                       bias_ref, s_a, s_b, ow_ref, *, tq, kc_len, n_cmp, n_blk):
    g = pl.program_id(1)
    t0 = pl.program_id(2) * tq
    cols = GROUP_SIZE * tq
    q_t = q_ref[0]
    q4 = jnp.concatenate([q_t[r * HEAD_DIM:(r + 1) * HEAD_DIM, :] for r in range(GROUP_SIZE)], axis=1)
    q4b = (q4 * LOG2_E).astype(_BF16)
    feat = lax.broadcasted_iota(jnp.int32, (KV_ROW, cols), 0)
    qp = jnp.where(lax.shift_right_logical(feat, int(math.log2(HEAD_DIM))) == g,
                   jnp.concatenate([q4b] * KV_SLOTS, axis=0), jnp.zeros((), _BF16))
    lane = lax.broadcasted_iota(jnp.int32, (1, cols), 1)
    qpos = t0 + (lane & (tq - 1))

    ncr = kc_ref.shape[2]
    s_c = jnp.dot(kc_ref[0, 0], _split3_queries(q4), preferred_element_type=_F32)
    n_idx = lax.broadcasted_iota(jnp.int32, (ncr, 1), 0)
    cmp_end = jnp.where(n_idx < n_cmp, n_idx * CMP_STRIDE + (CMP_BLOCK - 1), jnp.int32(2 ** 30))
    p_c = _softmax(s_c, cmp_end <= qpos, 0)
    o_c = _bdot(vct_ref[0, 0], p_c)

    imp = p_c[:, 0:tq]
    for r in range(1, GROUP_SIZE):
        imp = imp + p_c[:, r * tq:(r + 1) * tq]
    imp_hi = imp.astype(_BF16)
    imp_r = imp - imp_hi.astype(_F32)
    imp_mid = imp_r.astype(_BF16)
    imp_lo = (imp_r - imp_mid.astype(_F32)).astype(_BF16)
    imp_s = jnp.dot(mt_ref[...], jnp.concatenate([imp_hi, imp_mid, imp_lo], axis=0),
                    preferred_element_type=_F32)
    qrow = qpos[:, :tq]
    j_idx = lax.broadcasted_iota(jnp.int32, (n_blk, tq), 0)
    valid = j_idx * SLC_BLOCK <= qrow
    cur = lax.shift_right_arithmetic(qrow, int(math.log2(SLC_BLOCK)))
    forced = (j_idx == 0) | (valid & (j_idx > cur - N_LOCAL))
    score = jnp.where(forced, FORCED_SCORE, jnp.where(valid, imp_s, -1.0))
    rank = jnp.zeros((n_blk, tq), jnp.int32)
    row8 = lax.broadcasted_iota(jnp.int32, (SUBLANES, tq), 0)
    for j in range(n_blk):
        sj = score[j:j + 1, :]
        lo, hi = j // SUBLANES * SUBLANES, (j // SUBLANES + 1) * SUBLANES
        own = score[lo:hi]
        parts = [jnp.where(row8 > j - lo, jnp.where(sj >= own, 1, 0), jnp.where(sj > own, 1, 0))]
        if lo > 0:
            parts.insert(0, jnp.where(sj > score[:lo], 1, 0))
        if hi < n_blk:
            parts.append(jnp.where(sj >= score[hi:], 1, 0))
        rank = rank + jnp.concatenate(parts, axis=0)
    bias = jnp.where(valid & (rank < N_SELECT), 0.0, NEG_INF)
    bias_ref[...] = jnp.concatenate([bias] * GROUP_SIZE, axis=1)

    bpc = kc_len // SLC_BLOCK

    def scores(c, causal):
        k0 = pl.multiple_of(c * kc_len, kc_len)
        s = jnp.dot(ksn_ref[0, pl.ds(k0, kc_len), :], qp, preferred_element_type=_F32)
        if causal:
            kpos = k0 + lax.broadcasted_iota(jnp.int32, (kc_len, 1), 0)
            s = jnp.where(kpos <= qpos, s, NEG_INF)
        return s

    def with_ones(vals_t):
        return jnp.concatenate([vals_t, jnp.ones((2 * SUBLANES, vals_t.shape[1]), _BF16)], axis=0)

    def weighted(s, m, vals_t):
        return jnp.dot(with_ones(vals_t), jnp.exp2(s - m).astype(_BF16), preferred_element_type=_F32)

    def fold_in(c, s, m_acc, acc):
        k0 = pl.multiple_of(c * kc_len, kc_len)
        b8 = bias_ref[pl.ds(pl.multiple_of(c * bpc, bpc), bpc), :]
        blocks = [s[i * SLC_BLOCK:(i + 1) * SLC_BLOCK, :] for i in range(bpc)]
        m_tile = jnp.full((SUBLANES, cols), NEG_INF, _F32)
        for i in range(bpc):
            tiles = blocks[i].reshape(SLC_BLOCK // SUBLANES, SUBLANES, cols)
            m_tile = jnp.maximum(m_tile, jnp.max(tiles, axis=0) + b8[i:i + 1, :])
        m_new = jnp.maximum(m_acc, jnp.max(m_tile, axis=0, keepdims=True))
        p = jnp.concatenate([jnp.exp2(blocks[i] - (m_new - b8[i:i + 1, :])) for i in range(bpc)], axis=0)
        return m_new, (jnp.exp2(m_acc - m_new) * acc
                       + jnp.dot(with_ones(vst_ref[0, :, pl.ds(k0, kc_len)]), p.astype(_BF16),
                                 preferred_element_type=_F32))

    def step(c, causal, s_new_ref, s_old_ref, m_acc, acc):
        s_new_ref[...] = scores(c, causal)
        return fold_in(c - 1, s_old_ref[...], m_acc, acc)

    def pair(p, carry):
        c = 2 * p + 1
        return step(c + 1, True, s_a, s_b, *step(c, False, s_b, s_a, *carry))

    wk = tq + WINDOW
    w0 = pl.multiple_of(jnp.maximum(t0 - WINDOW, 0), tq)
    dpos = qpos - (w0 + lax.broadcasted_iota(jnp.int32, (wk, 1), 0))
    s_w = jnp.where(jnp.where(dpos >= 0, dpos, WINDOW) < WINDOW,
                    jnp.dot(kwn_ref[0, pl.ds(w0, wk), :], qp, preferred_element_type=_F32), NEG_INF)
    acc_w = weighted(s_w, jnp.max(s_w, axis=0, keepdims=True), vwt_ref[0, :, pl.ds(w0, wk)])
    ow_ref[...] = acc_w[:HEAD_DIM] / acc_w[HEAD_DIM:HEAD_DIM + 1]

    c_diag = t0 // kc_len
    s_a[...] = scores(0, True)
    init = (jnp.full((1, cols), NEG_INF, _F32), jnp.zeros((HEAD_DIM + 2 * SUBLANES, cols), _F32))
    m_acc, acc_s = lax.fori_loop(0, c_diag // 2, pair, init)

    def odd_tail(m_acc, acc):
        m_acc, acc = step(c_diag, True, s_b, s_a, m_acc, acc)
        return fold_in(c_diag, s_b[...], m_acc, acc)[1]

    def even_tail(m_acc, acc):
        return fold_in(c_diag, s_a[...], m_acc, acc)[1]

    acc_s = lax.cond(c_diag % 2 == 1, odd_tail, even_tail, m_acc, acc_s)
    o_s = acc_s[:HEAD_DIM] / acc_s[HEAD_DIM:HEAD_DIM + 1]
    o_w = ow_ref[...]

    outs = []
    for r in range(GROUP_SIZE):
        sl = slice(r * tq, (r + 1) * tq)
        gate = lambda branch: gt_ref[0, pl.ds(branch * N_HEADS + g * GROUP_SIZE + r, 1), :]
        outs.append(gate(0) * o_c[:, sl] + gate(1) * o_s[:, sl] + gate(2) * o_w[:, sl])
    o_ref[0] = jnp.concatenate(outs, axis=0).T


def _nsa_prompt(q_t, gates_t, kcmp3, kcmp_t, ks_n, ks_tb, kw_n, kw_tb, tq, kc_len):
    bsz, _, t = q_t.shape
    ncr = kcmp3.shape[2]
    n_blk = t // SLC_BLOCK
    mt = jnp.tile(_importance_map(ncr, n_blk, ncr - 1), (1, 3)).astype(_BF16)
    rows_n = pl.BlockSpec((1, t, KV_ROW), lambda b, g, i: (b, 0, 0))
    vals_t = pl.BlockSpec((1, HEAD_DIM, t), lambda b, g, i: (b, N_KV_HEADS + g, 0))
    qw = GROUP_SIZE * HEAD_DIM
    return pl.pallas_call(
        functools.partial(_nsa_prompt_kernel, tq=tq, kc_len=kc_len, n_cmp=ncr - 1, n_blk=n_blk),
        grid=(bsz, N_KV_HEADS, t // tq),
        in_specs=[pl.BlockSpec((1, qw, tq), lambda b, g, i: (b, g, i)),
                  pl.BlockSpec((1, GATE_ROWS, tq), lambda b, g, i: (b, 0, i)),
                  pl.BlockSpec((1, 1, ncr, KV_SLOTS * HEAD_DIM), lambda b, g, i: (b, g, 0, 0)),
                  pl.BlockSpec((1, 1, HEAD_DIM, ncr), lambda b, g, i: (b, N_KV_HEADS + g, 0, 0)),
                  rows_n, vals_t, rows_n, vals_t,
                  pl.BlockSpec(mt.shape, lambda b, g, i: (0, 0))],
        out_specs=pl.BlockSpec((1, tq, qw), lambda b, g, i: (b, i, g)),
        out_shape=jax.ShapeDtypeStruct((bsz, t, D_ATTN), _F32),
        scratch_shapes=[pltpu.VMEM((n_blk, GROUP_SIZE * tq), _F32),
                        pltpu.VMEM((kc_len, GROUP_SIZE * tq), _F32), pltpu.VMEM((kc_len, GROUP_SIZE * tq), _F32),
                        pltpu.VMEM((HEAD_DIM, GROUP_SIZE * tq), _F32)],
        compiler_params=_params("parallel", "parallel", "parallel"),
        name="nsa_prompt",
    )(q_t, gates_t, kcmp3, kcmp_t, ks_n, ks_tb, kw_n, kw_tb, mt)


def _nsa_sample_kernel(pt_ref, q_ref, gt_ref, kcmp_ref, kcmpt_ref, slc_hbm, ksn_ref, win_ref, kwt_ref,
                       m_ref, e_ref, o_ref, nw_ref, buf, sem,
                       *, n_pages, past, n_cmp, blk_pad, n_buf):
    b = pl.program_id(0)
    slot = _gather_pages(pt_ref, slc_hbm, buf, sem, n_pages)
    q = q_ref[0]
    gt = gt_ref[0]
    ks_new = ksn_ref[0]
    nseq = kwt_ref.shape[1]
    seq_lane = lax.broadcasted_iota(jnp.int32, (KV_ROW, nseq), 1)
    kw_col = jnp.sum(jnp.where(seq_lane == b, kwt_ref[...], 0.0), axis=1, keepdims=True)
    win = win_ref[0]
    pos = lax.broadcasted_iota(jnp.int32, win.shape, 1)
    band = jnp.where(pos == n_buf - 1, kw_col, pltpu.roll(win, n_buf - 1, 1))
    nw_ref[0] = band
    band_b = band.astype(_BF16)
    ncr = kcmp_ref.shape[2]
    lane = lax.broadcasted_iota(jnp.int32, (SUBLANES, LANES), 1)
    hrow = lax.broadcasted_iota(jnp.int32, (SUBLANES, LANES), 0)
    q8s, o_cs, sels = [], [], []
    for g in range(N_KV_HEADS):
        q8 = jnp.concatenate([q[:, (g * GROUP_SIZE + r) * HEAD_DIM:(g * GROUP_SIZE + r + 1) * HEAD_DIM]
                              for r in range(GROUP_SIZE)]
                             + [jnp.zeros((SUBLANES - GROUP_SIZE, HEAD_DIM), _F32)], axis=0)
        q8s.append(q8)
        q_hi = q8.astype(_BF16)
        q_lo = (q8 - q_hi.astype(_F32)).astype(_BF16)
        q3 = jnp.concatenate([q_hi, q_hi, q_lo, jnp.zeros_like(q_hi)], axis=1)
        s_c = lax.dot_general(q3, kcmp_ref[0, g], _NT, preferred_element_type=_F32)
        n_idx = lax.broadcasted_iota(jnp.int32, (SUBLANES, ncr), 1)
        cmp_end = jnp.where(n_idx < n_cmp, n_idx * CMP_STRIDE + (CMP_BLOCK - 1), jnp.int32(2 ** 30))
        p_c = _softmax(s_c, cmp_end <= past, 1)
        o_cs.append(_bdot_nt(p_c, kcmpt_ref[0, N_KV_HEADS + g]))

        imp = jnp.sum(p_c[:GROUP_SIZE], axis=0, keepdims=True)
        imp8 = jnp.broadcast_to(imp, (SUBLANES, ncr))
        imp_hi = imp8.astype(_BF16)
        imp_r = imp8 - imp_hi.astype(_F32)
        imp_mid = imp_r.astype(_BF16)
        imp_lo = (imp_r - imp_mid.astype(_F32)).astype(_BF16)
        imp_s = jnp.dot(jnp.concatenate([imp_hi, imp_mid, imp_lo], axis=1), m_ref[...],
                        preferred_element_type=_F32)[0:1]
        j_idx = lax.broadcasted_iota(jnp.int32, (1, blk_pad), 1)
        valid = j_idx * SLC_BLOCK <= past
        forced = (j_idx == 0) | (valid & (j_idx > past // SLC_BLOCK - N_LOCAL))
        score = jnp.where(forced, FORCED_SCORE, jnp.where(valid, imp_s, -1.0))
        s_k = jnp.broadcast_to(score, (blk_pad, blk_pad))
        s_j = s_k.T
        jj = lax.broadcasted_iota(jnp.int32, (blk_pad, blk_pad), 0)
        kk = lax.broadcasted_iota(jnp.int32, (blk_pad, blk_pad), 1)
        beats = jnp.where(kk > jj, jnp.where(s_j >= s_k, 1.0, 0.0), jnp.where(s_j > s_k, 1.0, 0.0))
        rank = jnp.sum(beats, axis=0, keepdims=True)
        sels.append(jnp.where(valid & (rank < N_SELECT), 1.0, 0.0))
    row_g = lax.broadcasted_iota(jnp.int32, (SUBLANES, blk_pad), 0)
    sel_rows = jnp.zeros((SUBLANES, blk_pad), _F32)
    for g in range(N_KV_HEADS):
        sel_rows = jnp.where(row_g == g, sels[g], sel_rows)
    picked = jnp.dot(sel_rows.astype(_BF16), e_ref[...], preferred_element_type=_F32)

    heads = []
    for g in range(N_KV_HEADS):
        q8 = q8s[g]
        q8b = q8.astype(_BF16)
        k_rows = slice(g * HEAD_DIM, (g + 1) * HEAD_DIM)
        v_rows = slice((N_KV_HEADS + g) * HEAD_DIM, (N_KV_HEADS + g + 1) * HEAD_DIM)
        s_s = jnp.where(picked[g:g + 1, :] > 0.5, _bdot(q8b, buf[slot, k_rows, :]), NEG_INF)
        s_n = jnp.sum(q8 * ks_new[:, k_rows], axis=-1, keepdims=True)
        m = jnp.maximum(jnp.max(s_s, axis=-1, keepdims=True), s_n)
        e_p, e_n = jnp.exp(s_s - m), jnp.exp(s_n - m)
        num = _bdot_nt(e_p, buf[slot, v_rows, :])
        o_s = (num + e_n * ks_new[:, v_rows]) / (jnp.sum(e_p, axis=-1, keepdims=True) + e_n)

        s_w = jnp.dot(q8b, band_b[k_rows, :], preferred_element_type=_F32)
        i_idx = lax.broadcasted_iota(jnp.int32, (SUBLANES, n_buf), 1)
        p_w = _softmax(s_w, (n_buf - 1 - i_idx) < WINDOW, 1)
        o_w = _bdot_nt(p_w, band_b[v_rows, :])

        def gate(branch):
            tgt = branch * N_HEADS + g * GROUP_SIZE + hrow
            return jnp.sum(jnp.where(lane == tgt, jnp.broadcast_to(gt, (SUBLANES, LANES)), 0.0),
                           axis=-1, keepdims=True)

        o = gate(0) * o_cs[g] + gate(1) * o_s + gate(2) * o_w
        heads += [o[r:r + 1, :] for r in range(GROUP_SIZE)]
    o_ref[0] = jnp.concatenate(heads, axis=1)


def _expand_table(past, blk_pad):
    return (jnp.arange(past)[None, :] // SLC_BLOCK == jnp.arange(blk_pad)[:, None]).astype(_BF16)


def _nsa_sample(q, gates, kcmp, kcmp_t, cache_t, page_table, ks_new, win_t, kw_t):
    nseq, n_pages = page_table.shape
    past = n_pages * PAGE_SIZE
    ncr = kcmp.shape[2]
    n_buf = win_t.shape[2]
    n_blk = past // SLC_BLOCK + 1
    blk_pad = -(-n_blk // LANES) * LANES
    m_tab = jnp.tile(_importance_map(ncr, blk_pad, ncr - 1).T, (3, 1)).astype(_BF16)
    e = _expand_table(past, blk_pad)
    row = lambda w: pl.BlockSpec((1, 1, w), lambda b, pt: (b, 0, 0))
    const = lambda a: pl.BlockSpec(a.shape, lambda b, pt: (0,) * a.ndim)
    win_spec = pl.BlockSpec((1, KV_ROW, n_buf), lambda b, pt: (b, 0, 0))
    grid_spec = pltpu.PrefetchScalarGridSpec(
        num_scalar_prefetch=1,
        grid=(nseq,),
        in_specs=[row(D_ATTN), row(LANES),
                  pl.BlockSpec((1, N_KV_HEADS, ncr, KV_SLOTS * HEAD_DIM), lambda b, pt: (b, 0, 0, 0)),
                  pl.BlockSpec((1, KV_SLOTS, HEAD_DIM, ncr), lambda b, pt: (b, 0, 0, 0)),
                  pl.BlockSpec(memory_space=pl.ANY),
                  row(KV_ROW), win_spec, const(kw_t), const(m_tab), const(e)],
        out_specs=[row(D_ATTN), win_spec],
        scratch_shapes=[pltpu.VMEM((2, KV_ROW, past), _F32), pltpu.SemaphoreType.DMA((2,))],
    )
    return pl.pallas_call(
        functools.partial(_nsa_sample_kernel, n_pages=n_pages, past=past, n_cmp=ncr - 1,
                          blk_pad=blk_pad, n_buf=n_buf),
        grid_spec=grid_spec,
        out_shape=[jax.ShapeDtypeStruct((nseq, 1, D_ATTN), _F32),
                   jax.ShapeDtypeStruct((nseq, KV_ROW, n_buf), _F32)],
        compiler_params=_params("arbitrary"),
        name="nsa_sample",
    )(page_table, q, gates, kcmp, kcmp_t, cache_t, ks_new, win_t, kw_t, m_tab, e)


def _post_kernel(x_ref, y_ref, u_ref, oa_ref, mod_ref, d_ref, wglu_ref, gs_ref, ga_ref, wout_ref, gm_ref,
                 wup_ref, wdn_ref, gf_ref, o_ref, *, ff_chunk):
    z = _gelu_tanh(y_ref[0] + d_ref[...] * u_ref[0])
    o_ssm = z * _sigmoid(jnp.dot(z.astype(_BF16), wglu_ref[...], preferred_element_type=_F32))
    mix = (jnp.dot(_rms(o_ssm, gs_ref[...]).astype(_BF16), wout_ref[:D_SSM, :], preferred_element_type=_F32)
           + jnp.dot(_rms(oa_ref[0], ga_ref[...]).astype(_BF16), wout_ref[D_SSM:, :], preferred_element_type=_F32))
    x1 = x_ref[0] + mod_ref[0, 2] * mix
    h2 = (_rms(x1, gm_ref[...]) * (1.0 + mod_ref[0, 4]) + mod_ref[0, 3]).astype(_BF16)
    acc = jnp.zeros_like(x1)
    for k in range(D_FF // ff_chunk):
        cols = slice(k * ff_chunk, (k + 1) * ff_chunk)
        hid = jnp.maximum(jnp.dot(h2, wup_ref[:, cols], preferred_element_type=_F32), 0.0)
        acc = acc + jnp.dot((hid * hid).astype(_BF16), wdn_ref[cols, :], preferred_element_type=_F32)
    o_ref[0] = _rms(x1 + mod_ref[0, 5] * acc, gf_ref[...])


def _post(x, y_ssm, u, o_attn, mod, lp, norm_final, tm):
    bsz, t, _ = x.shape
    r = mod.shape[2]
    rb = 1 if r == 1 else tm
    mod_map = (lambda b, i: (b, 0, 0, 0)) if r == 1 else (lambda b, i: (b, 0, i, 0))
    tok = lambda w: pl.BlockSpec((1, tm, w), lambda b, i: (b, i, 0))
    const = lambda shape: pl.BlockSpec(shape, lambda b, i: (0,) * len(shape), pipeline_mode=pl.Buffered(1))
    vec = lambda v: v.reshape(1, -1).astype(_F32)
    weights = [vec(lp['ssm_d']), lp['ssm_w_glu'].astype(_BF16), vec(lp['norm_out_ssm']), vec(lp['norm_out_attn']),
               lp['w_out'].astype(_BF16), vec(lp['norm_mlp']), lp['w_up'].astype(_BF16), lp['w_down'].astype(_BF16),
               vec(norm_final)]
    return pl.pallas_call(
        functools.partial(_post_kernel, ff_chunk=D_MODEL),
        grid=(bsz, t // tm),
        in_specs=[tok(D_MODEL), tok(D_SSM), tok(D_SSM), tok(D_ATTN),
                  pl.BlockSpec((1, 6, rb, D_MODEL), mod_map)] + [const(w.shape) for w in weights],
        out_specs=tok(D_MODEL),
        out_shape=jax.ShapeDtypeStruct((bsz, t, D_MODEL), _F32),
        compiler_params=_params("parallel", "parallel"),
        name="post_mlp",
    )(x, y_ssm, u, o_attn, mod, *weights)


def _feature_major(a):
    lead = a.shape[:-4]
    n = len(lead)
    return a.transpose(*range(n), n + 1, n + 2, n + 3, n).reshape(*lead, KV_ROW, a.shape[-4])


def _kv_output(a_t):
    n, _, t = a_t.shape
    return a_t.reshape(n, 2, N_KV_HEADS, HEAD_DIM, t).transpose(0, 4, 1, 2, 3)[None]


def kernel(x_prompt, x_sample, cache_cmp, cache_slc, state_win, state_ssm_re, state_ssm_im, page_table,
           c_prompt, c_sample, w_ada, b_ada, norm_attn, w_in, ssm_lambda_re, ssm_lambda_im, ssm_log_dt,
           ssm_b_re, ssm_b_im, ssm_c_re, ssm_c_im, ssm_d, ssm_w_glu, cmp_pe_k, cmp_w1_k, cmp_w2_k,
           cmp_pe_v, cmp_w1_v, cmp_w2_v, norm_out_ssm, norm_out_attn, w_out, norm_mlp, w_up, w_down,
           norm_final):
    depth = w_ada.shape[0]
    assert depth == 1, "single-layer trunk"
    l = 0
    bsz, t, _ = x_prompt.shape
    nseq = x_sample.shape[0]
    assert x_sample.shape[1] == 1
    tm = min(512, t)
    tq = min(256, t)
    kc_len = min(512, t)

    lp = dict(ssm_d=ssm_d[l], ssm_w_glu=ssm_w_glu[l], norm_out_ssm=norm_out_ssm[l],
              norm_out_attn=norm_out_attn[l], w_out=w_out[l], norm_mlp=norm_mlp[l], w_up=w_up[l], w_down=w_down[l])
    w_full = jnp.pad(w_in[l], ((0, 0), (0, D_IN_PAD - D_IN))).astype(_BF16)
    w_rows = jnp.concatenate([w_full[:, _Q0:_G0 + 3 * N_HEADS].T,
                              jnp.zeros((GATE_ROWS - 3 * N_HEADS, D_MODEL), _BF16)], axis=0)
    tq0, tkc0, tks0, tkw0, tg0 = 0, D_ATTN, D_ATTN + KV_ROW, D_ATTN + 2 * KV_ROW, D_ATTN + 3 * KV_ROW
    tabs = _ssm_tables(ssm_lambda_re[l], ssm_lambda_im[l], ssm_log_dt[l], ssm_b_re[l], ssm_b_im[l],
                       ssm_c_re[l], ssm_c_im[l])
    ctabs = _compress_tables(cmp_pe_k[l], cmp_w1_k[l], cmp_w2_k[l], cmp_pe_v[l], cmp_w1_v[l], cmp_w2_v[l])

    mod = _ada_mod(jnp.concatenate([c_prompt, c_sample], axis=0), w_ada[l], b_ada[l])
    mod_p = mod[:bsz].reshape(bsz, 6, 1, D_MODEL)
    mod_s = mod[bsz:].reshape(nseq, 6, D_MODEL).transpose(1, 0, 2)[None]

    w_tok = jnp.concatenate([w_full[:, :D_SSM], w_full[:, _KC0:_G0]], axis=1)
    outs_p = [('n', 0, D_SSM, None), ('n', D_SSM, D_SSM + KV_ROW, None),
              ('n', D_SSM + KV_ROW, D_SSM + 2 * KV_ROW, None), ('n', D_SSM + 2 * KV_ROW, D_SSM + 3 * KV_ROW, None),
              ('t', tq0, tkc0, 'scale'), ('t', tkc0, tks0, None), ('t', tks0, tkw0, None), ('t', tkw0, tg0, None),
              ('t', tks0, tkw0, None), ('t', tkw0, tg0, None), ('t', tg0, tg0 + GATE_ROWS, 'sigmoid')]
    dt_p = [_F32, _F32, _BF16, _BF16, _F32, _F32, _F32, _F32, _BF16, _BF16, _F32]
    (u, kc_n, ks_n, kw_n, q_t, kc_t, ks_t, kw_t, ks_tb, kw_tb, gates_t) = _in_proj(
        x_prompt, mod_p, norm_attn[l], w_tok, w_rows, tm, outs_p, dt_p)
    y_ssm, sp_re, sp_im = _ssm_prompt(u, tabs)
    kcmp, kcmp_t = _compress_prompt(kc_n, ctabs)
    o_attn = _nsa_prompt(q_t, gates_t, kcmp, kcmp_t, ks_n, ks_tb, kw_n, kw_tb, tq, kc_len)
    y_prompt = _post(x_prompt, y_ssm, u, o_attn, mod_p, lp, norm_final, tm)
    w_keep = min(WINDOW, t)

    xs = x_sample.reshape(1, nseq, D_MODEL)
    outs_s = [('n', 0, D_SSM, None), ('n', _Q0, _KC0, 'scale'), ('n', _KS0, _KW0, None), ('n', _G0, D_IN_PAD, 'sigmoid'),
              ('t', tkc0, tks0, None), ('t', tks0, tkw0, None), ('t', tkw0, tg0, None)]
    u_s, q_s, ks_row, gates_s, kc_ts, ks_ts, kw_ts = _in_proj(
        xs, mod_s, norm_attn[l], w_full, w_rows, nseq, outs_s, [_F32] * len(outs_s))
    ns = N_SSM_GROUPS * SSM_STATE
    y_s, ss_re, ss_im = _ssm_step(u_s[0], state_ssm_re[l].reshape(nseq, ns), state_ssm_im[l].reshape(nseq, ns), tabs)
    kcmp_s, kcmp_ts = _compress_paged(_feature_major(cache_cmp[l]), page_table, ctabs)
    o_attn_s, new_win_t = _nsa_sample(q_s.reshape(nseq, 1, D_ATTN), gates_s.reshape(nseq, 1, LANES), kcmp_s, kcmp_ts,
                                      _feature_major(cache_slc[l]), page_table, ks_row.reshape(nseq, 1, KV_ROW),
                                      _feature_major(state_win[l]), kw_ts[0])
    y_sample = _post(xs, y_s[None], u_s, o_attn_s.reshape(1, nseq, D_ATTN), mod_s, lp, norm_final, nseq)
    st_shape = (1, nseq, N_SSM_GROUPS, SSM_STATE)
    new_row = lambda a_t: _kv_output(a_t[0].T[:, :, None])

    return (y_prompt, y_sample.reshape(nseq, 1, D_MODEL),
            _kv_output(kc_t), _kv_output(ks_t), _kv_output(kw_t[:, :, t - w_keep:]),
            sp_re[None], sp_im[None],
            new_row(kc_ts), new_row(ks_ts), _kv_output(new_win_t),
            ss_re.reshape(st_shape), ss_im.reshape(st_shape))
```

```python
import functools
import math

import jax
import jax.numpy as jnp
from jax import lax
from jax.experimental import pallas as pl
from jax.experimental.pallas import tpu as pltpu

D_MODEL = 1024
D_SSM = D_MODEL // 2
SSM_GROUP = 16
N_SSM_GROUPS = D_SSM // SSM_GROUP
SSM_STATE = 64
HEAD_DIM = 64
D_ATTN = D_MODEL - D_SSM
N_HEADS = D_ATTN // HEAD_DIM
N_KV_HEADS = 2
GROUP_SIZE = N_HEADS // N_KV_HEADS
D_KV = N_KV_HEADS * HEAD_DIM
CMP_BLOCK = 32
CMP_STRIDE = 16
CMP_HIDDEN = 2 * HEAD_DIM
SLC_BLOCK = 64
N_SELECT = 16
N_LOCAL = 2
WINDOW = 512
D_FF = 4 * D_MODEL
D_IN = D_SSM + D_ATTN + 6 * D_KV + 3 * N_HEADS
EPS = 1e-6
PAGE_SIZE = 128

LANES = 128
SUBLANES = 8
D_IN_PAD = -(-D_IN // LANES) * LANES
KV_SLOTS = 2 * N_KV_HEADS
KV_ROW = KV_SLOTS * HEAD_DIM
GATE_ROWS = 32
SSM_CHUNK = 16
SSM_LANE_GROUPS = LANES // SSM_GROUP
VMEM_LIMIT = 56 * 1024 * 1024
FORCED_SCORE = 1e30
NEG_INF = float("-inf")
LOG2_E = math.log2(math.e)
RELAYOUT_UNROLL = 16
SCAN_UNROLL = 8

_Q0, _KC0, _KS0, _KW0, _G0 = D_SSM, D_MODEL, D_MODEL + KV_ROW, D_MODEL + 2 * KV_ROW, D_MODEL + 3 * KV_ROW

_BF16 = jnp.bfloat16
_F32 = jnp.float32
_NT = (((1,), (1,)), ((), ()))
_HIGHEST = lax.Precision.HIGHEST


def _params(*sem):
    return pltpu.CompilerParams(dimension_semantics=sem, vmem_limit_bytes=VMEM_LIMIT)


def _rms(x, g):
    return x * lax.rsqrt(jnp.mean(x * x, axis=-1, keepdims=True) + EPS) * g


def _gelu_tanh(x):
    return x * (0.5 * (1.0 + jnp.tanh(math.sqrt(2.0 / math.pi) * (x + 0.044715 * (x * x * x)))))


def _sigmoid(x):
    return 1.0 / (1.0 + jnp.exp(-x))


def _bdot(a, b):
    return jnp.dot(a.astype(_BF16), b.astype(_BF16), preferred_element_type=_F32)


def _bdot_nt(a, b):
    return lax.dot_general(a.astype(_BF16), b.astype(_BF16), _NT, preferred_element_type=_F32)


def _ada_kernel(c_ref, w_ref, b_ref, o_ref):
    c = c_ref[...]
    o_ref[...] = jnp.dot(c * _sigmoid(c), w_ref[...], precision=_HIGHEST,
                         preferred_element_type=_F32) + b_ref[...]


def _ada_mod(c, w_ada, b_ada):
    n, tn = c.shape[0], D_MODEL
    return pl.pallas_call(
        _ada_kernel,
        grid=(6 * D_MODEL // tn,),
        in_specs=[pl.BlockSpec((n, D_MODEL), lambda j: (0, 0)),
                  pl.BlockSpec((D_MODEL, tn), lambda j: (0, j)),
                  pl.BlockSpec((1, tn), lambda j: (0, j))],
        out_specs=pl.BlockSpec((n, tn), lambda j: (0, j)),
        out_shape=jax.ShapeDtypeStruct((n, 6 * D_MODEL), _F32),
        compiler_params=_params("parallel"),
        name="ada_mod",
    )(c, w_ada, b_ada.reshape(1, -1))


def _inproj_kernel(x_ref, mod_ref, g_ref, wn_ref, wt_ref, *out_refs, outs):
    x = x_ref[0]
    h = (_rms(x, g_ref[...]) * (1.0 + mod_ref[0, 1]) + mod_ref[0, 0]).astype(_BF16)
    pn = jnp.dot(h, wn_ref[...], preferred_element_type=_F32)
    pt = lax.dot_general(wt_ref[...], h, _NT, preferred_element_type=_F32)
    for ref, (kind, lo, hi, post) in zip(out_refs, outs):
        v = pn[:, lo:hi] if kind == 'n' else pt[lo:hi, :]
        if post == 'scale':
            v = v * (HEAD_DIM ** -0.5)
        elif post == 'sigmoid':
            v = _sigmoid(v)
        ref[0] = v.astype(ref.dtype)


def _in_proj(x, mod, norm_attn, wn, wt, tm, outs, dtypes):
    bsz, t, _ = x.shape
    r = mod.shape[2]
    rb = 1 if r == 1 else tm
    mod_map = (lambda b, i: (b, 0, 0, 0)) if r == 1 else (lambda b, i: (b, 0, i, 0))
    out_specs, out_shape = [], []
    for (kind, lo, hi, _), dt in zip(outs, dtypes):
        if kind == 'n':
            out_specs.append(pl.BlockSpec((1, tm, hi - lo), lambda b, i: (b, i, 0)))
            out_shape.append(jax.ShapeDtypeStruct((bsz, t, hi - lo), dt))
        else:
            out_specs.append(pl.BlockSpec((1, hi - lo, tm), lambda b, i: (b, 0, i)))
            out_shape.append(jax.ShapeDtypeStruct((bsz, hi - lo, t), dt))
    return pl.pallas_call(
        functools.partial(_inproj_kernel, outs=tuple(outs)),
        grid=(bsz, t // tm),
        in_specs=[pl.BlockSpec((1, tm, D_MODEL), lambda b, i: (b, i, 0)),
                  pl.BlockSpec((1, 6, rb, D_MODEL), mod_map),
                  pl.BlockSpec((1, D_MODEL), lambda b, i: (0, 0)),
                  pl.BlockSpec(wn.shape, lambda b, i: (0, 0)),
                  pl.BlockSpec(wt.shape, lambda b, i: (0, 0))],
        out_specs=out_specs,
        out_shape=out_shape,
        compiler_params=_params("parallel", "parallel"),
        name="in_proj",
    )(x, mod, norm_attn.reshape(1, -1), wn, wt)


def _ssm_tables(lam_re, lam_im, log_dt, b_re, b_im, c_re, c_im):
    hp = dict(precision=_HIGHEST)
    lr, li = lam_re.astype(_F32), lam_im.astype(_F32)
    dt = jnp.exp(log_dt.astype(_F32))[:, None]
    ar, ai = lr * dt, li * dt

    def power(k):
        mag = jnp.exp(ar * k)
        return mag * jnp.cos(ai * k), mag * jnp.sin(ai * k)

    n = SSM_CHUNK
    steps = jnp.arange(n + 1, dtype=_F32)[:, None, None]
    pwr, pwi = power(steps)
    lbr, lbi = pwr[1], pwi[1]
    den = lr * lr + li * li
    fr = ((lbr - 1.0) * lr + lbi * li) / den
    fi = (lbi * lr - (lbr - 1.0) * li) / den
    br_, bi_ = b_re.astype(_F32), b_im.astype(_F32)
    bbr = fr[:, :, None] * br_ - fi[:, :, None] * bi_
    bbi = fr[:, :, None] * bi_ + fi[:, :, None] * br_
    cr, ci = c_re.astype(_F32), c_im.astype(_F32)
    er = pwr[:, :, :, None] * bbr[None] - pwi[:, :, :, None] * bbi[None]
    ei = pwr[:, :, :, None] * bbi[None] + pwi[:, :, :, None] * bbr[None]
    kern = (jnp.einsum('gip,dgpj->dgij', cr, er[:n], **hp)
            - jnp.einsum('gip,dgpj->dgij', ci, ei[:n], **hp))
    nb, gpb = N_SSM_GROUPS // SSM_LANE_GROUPS, SSM_LANE_GROUPS

    def expand(a, g_axis):
        w = a.shape[-1]
        owner = jnp.arange(gpb * w) // w
        own = jnp.arange(gpb).reshape((gpb,) + (1,) * (a.ndim - 1 - g_axis))
        return jnp.where(owner == own, jnp.tile(a, (1,) * (a.ndim - 1) + (gpb,)), 0.0)

    lag_tiles = kern.reshape(n, nb, gpb, SSM_GROUP, SSM_GROUP).transpose(1, 0, 2, 4, 3)
    lag_tiles = expand(lag_tiles, 2).reshape(nb, n, LANES, LANES)
    rev = n - 1 - jnp.arange(n)

    def state_in(e):
        w = e[rev].reshape(n, nb, gpb, SSM_STATE, SSM_GROUP).transpose(1, 0, 2, 4, 3)
        return expand(w, 2).reshape(nb, n * LANES, gpb * SSM_STATE)

    w_in = jnp.concatenate([state_in(er), state_in(ei)], axis=-1)
    cvr = cr[None] * pwr[1:, :, None, :] - ci[None] * pwi[1:, :, None, :]
    cvi = cr[None] * pwi[1:, :, None, :] + ci[None] * pwr[1:, :, None, :]

    def state_out(cv):
        v = cv.reshape(n, nb, gpb, SSM_GROUP, SSM_STATE).transpose(1, 0, 2, 4, 3)
        return expand(v, 2).reshape(nb, n, gpb * SSM_STATE, LANES)

    v_out = jnp.concatenate([state_out(cvr), -state_out(cvi)], axis=2)
    a_chunk = jnp.stack([pwr[n].reshape(nb, -1), pwi[n].reshape(nb, -1)], axis=1)
    eye = jnp.eye(N_SSM_GROUPS, dtype=_F32)
    ns = N_SSM_GROUPS * SSM_STATE
    bd = lambda a: jnp.einsum('gpi,gh->gihp', a, eye).reshape(D_SSM, ns)
    cd = lambda a: jnp.einsum('gip,gh->gphi', a, eye).reshape(ns, D_SSM)
    return dict(toep=lag_tiles.astype(_BF16), w_in=w_in.astype(_BF16), v_out=v_out.astype(_BF16), a_chunk=a_chunk,
                bd_re=bd(bbr).astype(_BF16), bd_im=bd(bbi).astype(_BF16),
                cd_re=cd(cr).astype(_BF16), cd_im=cd(ci).astype(_BF16),
                lam_re=lbr.reshape(1, ns), lam_im=lbi.reshape(1, ns))


def _ssm_prompt_kernel(u_ref, toep_ref, w_ref, v_ref, a_ref, y_ref, s_ref, x_scr, sin_scr, *, n_chunks):
    n = SSM_CHUNK
    lhs = jnp.concatenate([u_ref[0, pl.ds(tau, n_chunks, stride=n), :] for tau in range(n)],
                          axis=1).astype(_BF16)
    x_scr[...] = jnp.dot(lhs, w_ref[0], preferred_element_type=_F32)
    half = x_scr.shape[1] // 2
    a_re = a_ref[0, 0:1, :]
    a_im = a_ref[0, 1:2, :]

    def step(c, carry):
        s_re, s_im = carry
        row = pl.ds(c, 1)
        sin_scr[row, :half] = s_re
        sin_scr[row, half:] = s_im
        x = x_scr[row, :]
        return (a_re * s_re - a_im * s_im + x[:, :half],
                a_re * s_im + a_im * s_re + x[:, half:])

    zero = jnp.zeros((1, half), _F32)
    s_re, s_im = lax.fori_loop(0, n_chunks, step, (zero, zero), unroll=math.gcd(SCAN_UNROLL, n_chunks))
    s_ref[0, 0] = jnp.concatenate([s_re, s_im], axis=1)
    sin_b = sin_scr[...].astype(_BF16)
    zero_tile = jnp.zeros((LANES, LANES), _BF16)
    for t2 in range(n // 2):
        t_lo = 2 * t2
        k_hi = (t_lo + 2) * LANES
        w_lag = jnp.concatenate(
            [jnp.concatenate([toep_ref[0, t_lo - tau] if tau <= t_lo else zero_tile, toep_ref[0, t_lo + 1 - tau]],
                             axis=1) for tau in range(t_lo + 2)], axis=0)
        w_state = jnp.concatenate([v_ref[0, t_lo], v_ref[0, t_lo + 1]], axis=1)
        y = (jnp.dot(lhs[:, :k_hi], w_lag, preferred_element_type=_F32)
             + jnp.dot(sin_b, w_state, preferred_element_type=_F32))
        for d in range(2):
            y_ref[0, pl.ds(t_lo + d, n_chunks, stride=n), :] = y[:, d * LANES:(d + 1) * LANES]


def _ssm_prompt(u, tabs):
    bsz, t, _ = u.shape
    nch = t // SSM_CHUNK
    nb = D_SSM // LANES
    sw = 2 * SSM_LANE_GROUPS * SSM_STATE
    table = lambda a: pl.BlockSpec((1,) + a.shape[1:], lambda m, b: (m,) + (0,) * (a.ndim - 1),
                                   pipeline_mode=pl.Buffered(1))
    tok = pl.BlockSpec((1, t, LANES), lambda m, b: (b, 0, m))
    y, s = pl.pallas_call(
        functools.partial(_ssm_prompt_kernel, n_chunks=nch),
        grid=(nb, bsz),
        in_specs=[tok, table(tabs['toep']), table(tabs['w_in']), table(tabs['v_out']),
                  pl.BlockSpec((1, 2, sw // 2), lambda m, b: (m, 0, 0))],
        out_specs=[tok, pl.BlockSpec((1, 1, 1, sw), lambda m, b: (b, m, 0, 0))],
        out_shape=[jax.ShapeDtypeStruct((bsz, t, D_SSM), _F32),
                   jax.ShapeDtypeStruct((bsz, nb, 1, sw), _F32)],
        scratch_shapes=[pltpu.VMEM((nch, sw), _F32), pltpu.VMEM((nch, sw), _F32)],
        compiler_params=_params("parallel", "parallel"),
        name="ssm_prompt",
    )(u, tabs['toep'], tabs['w_in'], tabs['v_out'], tabs['a_chunk'])
    state = lambda a: a.reshape(bsz, N_SSM_GROUPS, SSM_STATE)
    return y, state(s[:, :, 0, :sw // 2]), state(s[:, :, 0, sw // 2:])


def _ssm_step_kernel(u_ref, s0re_ref, s0im_ref, lre_ref, lim_ref, bdre_ref, bdim_ref, cdre_ref, cdim_ref,
                     y_ref, sre_ref, sim_ref):
    u = u_ref[...]
    s_re, s_im = s0re_ref[...], s0im_ref[...]
    l_re, l_im = lre_ref[...], lim_ref[...]
    n_re = l_re * s_re - l_im * s_im + _bdot(u, bdre_ref[...])
    n_im = l_re * s_im + l_im * s_re + _bdot(u, bdim_ref[...])
    sre_ref[...] = n_re
    sim_ref[...] = n_im
    y_ref[...] = _bdot(n_re, cdre_ref[...]) - _bdot(n_im, cdim_ref[...])


def _ssm_step(u, s0_re, s0_im, tabs):
    n = u.shape[0]
    ns = N_SSM_GROUPS * SSM_STATE
    return pl.pallas_call(
        _ssm_step_kernel,
        out_shape=[jax.ShapeDtypeStruct((n, D_SSM), _F32),
                   jax.ShapeDtypeStruct((n, ns), _F32),
                   jax.ShapeDtypeStruct((n, ns), _F32)],
        compiler_params=pltpu.CompilerParams(vmem_limit_bytes=VMEM_LIMIT),
        name="ssm_step",
    )(u, s0_re, s0_im, tabs['lam_re'], tabs['lam_im'], tabs['bd_re'], tabs['bd_im'], tabs['cd_re'], tabs['cd_im'])


def _compress_tables(pe_k, w1_k, w2_k, pe_v, w1_v, w2_v):
    zeros = jnp.zeros((CMP_STRIDE, HEAD_DIM, CMP_HIDDEN), _F32)
    cols = []
    for half in range(2):
        for slot in range(KV_SLOTS):
            w1 = (w1_k if slot < N_KV_HEADS else w1_v)[half * CMP_STRIDE:(half + 1) * CMP_STRIDE]
            blk = jnp.stack([w1 if s == slot else zeros for s in range(KV_SLOTS)], axis=1)
            cols.append(blk.reshape(CMP_STRIDE * KV_ROW, CMP_HIDDEN))
    w_all = jnp.concatenate(cols, axis=1).astype(_BF16)
    z2 = jnp.zeros((CMP_HIDDEN, HEAD_DIM), _F32)
    w2 = jnp.concatenate(
        [jnp.concatenate([(w2_k if s < N_KV_HEADS else w2_v) if s == slot else z2 for s in range(KV_SLOTS)], axis=1)
         for slot in range(KV_SLOTS)], axis=0).astype(_BF16)
    pe = jnp.stack([pe_k.reshape(-1), pe_v.reshape(-1)], axis=0)
    pe = jnp.concatenate([pe, jnp.zeros((SUBLANES - 2, pe.shape[1]), _F32)], axis=0)
    w1f = jnp.stack([w1_k.reshape(-1, CMP_HIDDEN), w1_v.reshape(-1, CMP_HIDDEN)], axis=0)
    halves = lambda w1: jnp.concatenate([w1[:CMP_STRIDE].reshape(-1, CMP_HIDDEN),
                                         w1[CMP_STRIDE:].reshape(-1, CMP_HIDDEN)], axis=1)
    w_slot = jnp.stack([halves(w1_k if s < N_KV_HEADS else w1_v) for s in range(KV_SLOTS)], axis=0).astype(_BF16)
    return dict(w_all=w_all, w_slot=w_slot, tail=(w2, w2.T, pe, w1f))


def _compress_bias(pe_ref, w1f_ref):
    pe = pe_ref[...]
    bias_k = jnp.dot(pe, w1f_ref[0], precision=_HIGHEST, preferred_element_type=_F32)[0:1]
    bias_v = jnp.dot(pe, w1f_ref[1], precision=_HIGHEST, preferred_element_type=_F32)[1:2]
    return jnp.concatenate([bias_k, bias_k, bias_v, bias_v], axis=1)


def _compress_tail(h, bias, w2_ref, w2t_ref, n_rows):
    hw = KV_SLOTS * CMP_HIDDEN
    nxt = pltpu.roll(h[:, hw:], n_rows - 1, 0)
    hid = _gelu_tanh(h[:, :hw] + nxt + bias)
    row = lax.broadcasted_iota(jnp.int32, hid.shape, 0)
    hid = jnp.where(row < n_rows - 1, hid, 0.0).astype(_BF16)
    return (jnp.dot(hid, w2_ref[...], preferred_element_type=_F32),
            lax.dot_general(w2t_ref[...], hid, _NT, preferred_element_type=_F32))


def _split3_keys(k):
    hi = k.astype(_BF16)
    lo = (k - hi.astype(_F32)).astype(_BF16)
    return jnp.concatenate([hi, lo, hi, jnp.zeros_like(hi)], axis=1)


def _split3_queries(q_t):
    hi = q_t.astype(_BF16)
    lo = (q_t - hi.astype(_F32)).astype(_BF16)
    return jnp.concatenate([hi, hi, lo, jnp.zeros_like(hi)], axis=0)


def _store_compressed(out, out_t, k3_ref, ot_ref):
    for g in range(N_KV_HEADS):
        k3_ref[0, g] = _split3_keys(out[:, g * HEAD_DIM:(g + 1) * HEAD_DIM])
    for s in range(KV_SLOTS):
        ot_ref[0, s] = out_t[s * HEAD_DIM:(s + 1) * HEAD_DIM, :]


def _compress_prompt_kernel(x_ref, wall_ref, w2_ref, w2t_ref, pe_ref, w1f_ref, k3_ref, ot_ref, *, n_rows):
    h = jnp.dot(x_ref[0].astype(_BF16), wall_ref[...], preferred_element_type=_F32)
    out, out_t = _compress_tail(h, _compress_bias(pe_ref, w1f_ref), w2_ref, w2t_ref, n_rows)
    _store_compressed(out, out_t, k3_ref, ot_ref)


def _compress_out(nseq, nc, index_map):
    specs = [pl.BlockSpec((1, N_KV_HEADS, nc, KV_SLOTS * HEAD_DIM), index_map),
             pl.BlockSpec((1, KV_SLOTS, HEAD_DIM, nc), index_map)]
    shapes = [jax.ShapeDtypeStruct((nseq, N_KV_HEADS, nc, KV_SLOTS * HEAD_DIM), _BF16),
              jax.ShapeDtypeStruct((nseq, KV_SLOTS, HEAD_DIM, nc), _F32)]
    return specs, shapes


def _compress_prompt(kv_cmp, ctabs):
    bsz, t, _ = kv_cmp.shape
    nc = t // CMP_STRIDE
    const = lambda a: pl.BlockSpec(a.shape, lambda b: (0,) * a.ndim)
    weights = (ctabs['w_all'],) + ctabs['tail']
    out_specs, out_shape = _compress_out(bsz, nc, lambda b: (b, 0, 0, 0))
    return pl.pallas_call(
        functools.partial(_compress_prompt_kernel, n_rows=nc),
        grid=(bsz,),
        in_specs=[pl.BlockSpec((1, nc, CMP_STRIDE * KV_ROW), lambda b: (b, 0, 0))] + [const(a) for a in weights],
        out_specs=out_specs,
        out_shape=out_shape,
        compiler_params=_params("parallel"),
        name="compress_prompt",
    )(kv_cmp.reshape(bsz, nc, CMP_STRIDE * KV_ROW), *weights)


def _page_copies(pt_ref, cache_hbm, buf, sem, seq, slot, n_pages):
    return [pltpu.make_async_copy(cache_hbm.at[pt_ref[seq, j]],
                                  buf.at[slot, :, pl.ds(j * PAGE_SIZE, PAGE_SIZE)],
                                  sem.at[slot]) for j in range(n_pages)]


def _gather_pages(pt_ref, cache_hbm, buf, sem, n_pages):
    b = pl.program_id(0)
    slot = lax.rem(b, 2)

    @pl.when(b == 0)
    def _():
        for cp in _page_copies(pt_ref, cache_hbm, buf, sem, 0, 0, n_pages):
            cp.start()

    @pl.when(b + 1 < pl.num_programs(0))
    def _():
        for cp in _page_copies(pt_ref, cache_hbm, buf, sem, b + 1, 1 - slot, n_pages):
            cp.start()

    for cp in _page_copies(pt_ref, cache_hbm, buf, sem, b, slot, n_pages):
        cp.wait()
    return slot


def _chunk_permutation():
    span = 2 * PAGE_SIZE
    r = jnp.arange(span)
    src = (r % (span // CMP_STRIDE)) * CMP_STRIDE + r // (span // CMP_STRIDE)
    return (src[:, None] == jnp.arange(span)[None, :]).astype(_BF16)


def _compress_paged_kernel(pt_ref, cache_hbm, perm_ref, wslot_ref, w2_ref, w2t_ref, pe_ref, w1f_ref,
                           o_ref, ot_ref, buf, sem, xr, bias_scr, *, n_pages, n_rows):
    slot = _gather_pages(pt_ref, cache_hbm, buf, sem, n_pages)

    @pl.when(pl.program_id(0) == 0)
    def _():
        bias_scr[...] = jnp.broadcast_to(_compress_bias(pe_ref, w1f_ref), bias_scr.shape)

    span = 2 * PAGE_SIZE
    cps = span // CMP_STRIDE
    low = lax.broadcasted_iota(jnp.int32, (cps, LANES), 1) < HEAD_DIM

    def relayout(i, carry):
        xt = buf[slot, :, pl.ds(pl.multiple_of(i * span, span), span)].astype(_BF16)
        rows = lax.dot_general(perm_ref[...], xt, _NT, preferred_element_type=_F32)
        chunk_rows = pl.ds(pl.multiple_of(i * cps, cps), cps)
        for a in range(CMP_STRIDE // 2):
            for s in range(KV_SLOTS):
                tile = slice((s // 2) * LANES, (s // 2 + 1) * LANES)
                even = rows[2 * a * cps:(2 * a + 1) * cps, tile]
                odd = rows[(2 * a + 1) * cps:(2 * a + 2) * cps, tile]
                if s % 2 == 0:
                    pair = jnp.where(low, even, pltpu.roll(odd, HEAD_DIM, 1))
                else:
                    pair = jnp.where(low, pltpu.roll(even, HEAD_DIM, 1), odd)
                xr[s, chunk_rows, a * LANES:(a + 1) * LANES] = pair.astype(_BF16)
        return carry

    lax.fori_loop(0, n_pages // 2, relayout, 0, unroll=math.gcd(RELAYOUT_UNROLL, n_pages // 2))
    hs = [jnp.dot(xr[s], wslot_ref[s], preferred_element_type=_F32) for s in range(KV_SLOTS)]
    h = jnp.concatenate([v[:, :CMP_HIDDEN] for v in hs] + [v[:, CMP_HIDDEN:] for v in hs], axis=1)
    out, out_t = _compress_tail(h, bias_scr[0:1, :], w2_ref, w2t_ref, n_rows)
    _store_compressed(out, out_t, o_ref, ot_ref)


def _compress_paged(cache_t, page_table, ctabs):
    nseq, n_pages = page_table.shape
    past = n_pages * PAGE_SIZE
    nc = past // CMP_STRIDE
    perm = _chunk_permutation()
    const = lambda a: pl.BlockSpec(a.shape, lambda b, pt: (0,) * a.ndim)
    out_specs, out_shape = _compress_out(nseq, nc, lambda b, pt: (b, 0, 0, 0))
    weights = (ctabs['w_slot'],) + ctabs['tail']
    grid_spec = pltpu.PrefetchScalarGridSpec(
        num_scalar_prefetch=1,
        grid=(nseq,),
        in_specs=[pl.BlockSpec(memory_space=pl.ANY), const(perm)] + [const(a) for a in weights],
        out_specs=out_specs,
        scratch_shapes=[pltpu.VMEM((2, KV_ROW, past), _F32), pltpu.SemaphoreType.DMA((2,)),
                        pltpu.VMEM((KV_SLOTS, nc, CMP_STRIDE * HEAD_DIM), _BF16),
                        pltpu.VMEM((SUBLANES, KV_SLOTS * CMP_HIDDEN), _F32)],
    )
    return pl.pallas_call(
        functools.partial(_compress_paged_kernel, n_pages=n_pages, n_rows=nc),
        grid_spec=grid_spec,
        out_shape=out_shape,
        compiler_params=_params("arbitrary"),
        name="compress_paged",
    )(page_table, cache_t, perm, *weights)


def _importance_map(n_cmp_rows, n_blocks_rows, n_cmp):
    ratio = SLC_BLOCK // CMP_STRIDE
    j = jnp.arange(n_blocks_rows)[:, None]
    n = jnp.arange(n_cmp_rows)[None, :]
    off = n - ratio * j
    w = jnp.where((off == -1) | (off == ratio - 1), 1.0, jnp.where((off >= 0) & (off < ratio - 1), 2.0, 0.0))
    return jnp.where(n < n_cmp, w, 0.0).astype(_F32)


def _softmax(s, ok, axis):
    s = jnp.where(ok, s, NEG_INF)
    m = jnp.max(s, axis=axis, keepdims=True)
    m = jnp.where(m == NEG_INF, 0.0, m)
    e = jnp.exp(s - m)
    den = jnp.sum(e, axis=axis, keepdims=True)
    return e / jnp.where(den > 0, den, 1.0)


def _nsa_prompt_kernel(q_ref, gt_ref, kc_ref, vct_ref, ksn_ref, vst_ref, kwn_ref, vwt_ref, mt_ref, o_ref,
                       bias_ref, s_a, s_b, ow_ref, *, tq, kc_len, n_cmp, n_blk):
    g = pl.program_id(1)
    t0 = pl.program_id(2) * tq
    cols = GROUP_SIZE * tq
    q_t = q_ref[0]
    q4 = jnp.concatenate([q_t[r * HEAD_DIM:(r + 1) * HEAD_DIM, :] for r in range(GROUP_SIZE)], axis=1)
    q4b = (q4 * LOG2_E).astype(_BF16)
    feat = lax.broadcasted_iota(jnp.int32, (KV_ROW, cols), 0)
    qp = jnp.where(lax.shift_right_logical(feat, int(math.log2(HEAD_DIM))) == g,
                   jnp.concatenate([q4b] * KV_SLOTS, axis=0), jnp.zeros((), _BF16))
    lane = lax.broadcasted_iota(jnp.int32, (1, cols), 1)
    qpos = t0 + (lane & (tq - 1))

    ncr = kc_ref.shape[2]
    s_c = jnp.dot(kc_ref[0, 0], _split3_queries(q4), preferred_element_type=_F32)
    n_idx = lax.broadcasted_iota(jnp.int32, (ncr, 1), 0)
    cmp_end = jnp.where(n_idx < n_cmp, n_idx * CMP_STRIDE + (CMP_BLOCK - 1), jnp.int32(2 ** 30))
    p_c = _softmax(s_c, cmp_end <= qpos, 0)
    o_c = _bdot(vct_ref[0, 0], p_c)

    imp = p_c[:, 0:tq]
    for r in range(1, GROUP_SIZE):
        imp = imp + p_c[:, r * tq:(r + 1) * tq]
    imp_hi = imp.astype(_BF16)
    imp_r = imp - imp_hi.astype(_F32)
    imp_mid = imp_r.astype(_BF16)
    imp_lo = (imp_r - imp_mid.astype(_F32)).astype(_BF16)
    imp_s = jnp.dot(mt_ref[...], jnp.concatenate([imp_hi, imp_mid, imp_lo], axis=0),
                    preferred_element_type=_F32)
    qrow = qpos[:, :tq]
    j_idx = lax.broadcasted_iota(jnp.int32, (n_blk, tq), 0)
    valid = j_idx * SLC_BLOCK <= qrow
    cur = lax.shift_right_arithmetic(qrow, int(math.log2(SLC_BLOCK)))
    forced = (j_idx == 0) | (valid & (j_idx > cur - N_LOCAL))
    score = jnp.where(forced, FORCED_SCORE, jnp.where(valid, imp_s, -1.0))
    rank = jnp.zeros((n_blk, tq), jnp.int32)
    row8 = lax.broadcasted_iota(jnp.int32, (SUBLANES, tq), 0)
    for j in range(n_blk):
        sj = score[j:j + 1, :]
        lo, hi = j // SUBLANES * SUBLANES, (j // SUBLANES + 1) * SUBLANES
        own = score[lo:hi]
        parts = [jnp.where(row8 > j - lo, jnp.where(sj >= own, 1, 0), jnp.where(sj > own, 1, 0))]
        if lo > 0:
            parts.insert(0, jnp.where(sj > score[:lo], 1, 0))
        if hi < n_blk:
            parts.append(jnp.where(sj >= score[hi:], 1, 0))
        rank = rank + jnp.concatenate(parts, axis=0)
    bias = jnp.where(valid & (rank < N_SELECT), 0.0, NEG_INF)
    bias_ref[...] = jnp.concatenate([bias] * GROUP_SIZE, axis=1)

    bpc = kc_len // SLC_BLOCK

    def scores(c, causal):
        k0 = pl.multiple_of(c * kc_len, kc_len)
        s = jnp.dot(ksn_ref[0, pl.ds(k0, kc_len), :], qp, preferred_element_type=_F32)
        if causal:
            kpos = k0 + lax.broadcasted_iota(jnp.int32, (kc_len, 1), 0)
            s = jnp.where(kpos <= qpos, s, NEG_INF)
        return s

    def with_ones(vals_t):
        return jnp.concatenate([vals_t, jnp.ones((2 * SUBLANES, vals_t.shape[1]), _BF16)], axis=0)

    def weighted(s, m, vals_t):
        return jnp.dot(with_ones(vals_t), jnp.exp2(s - m).astype(_BF16), preferred_element_type=_F32)

    def fold_in(c, s, m_acc, acc):
        k0 = pl.multiple_of(c * kc_len, kc_len)
        b8 = bias_ref[pl.ds(pl.multiple_of(c * bpc, bpc), bpc), :]
        blocks = [s[i * SLC_BLOCK:(i + 1) * SLC_BLOCK, :] for i in range(bpc)]
        m_tile = jnp.full((SUBLANES, cols), NEG_INF, _F32)
        for i in range(bpc):
            tiles = blocks[i].reshape(SLC_BLOCK // SUBLANES, SUBLANES, cols)
            m_tile = jnp.maximum(m_tile, jnp.max(tiles, axis=0) + b8[i:i + 1, :])
        m_new = jnp.maximum(m_acc, jnp.max(m_tile, axis=0, keepdims=True))
        p = jnp.concatenate([jnp.exp2(blocks[i] - (m_new - b8[i:i + 1, :])) for i in range(bpc)], axis=0)
        return m_new, (jnp.exp2(m_acc - m_new) * acc
                       + jnp.dot(with_ones(vst_ref[0, :, pl.ds(k0, kc_len)]), p.astype(_BF16),
                                 preferred_element_type=_F32))

    def step(c, causal, s_new_ref, s_old_ref, m_acc, acc):
        s_new_ref[...] = scores(c, causal)
        return fold_in(c - 1, s_old_ref[...], m_acc, acc)

    def pair(p, carry):
        c = 2 * p + 1
        return step(c + 1, True, s_a, s_b, *step(c, False, s_b, s_a, *carry))

    wk = tq + WINDOW
    w0 = pl.multiple_of(jnp.maximum(t0 - WINDOW, 0), tq)
    dpos = qpos - (w0 + lax.broadcasted_iota(jnp.int32, (wk, 1), 0))
    s_w = jnp.where(jnp.where(dpos >= 0, dpos, WINDOW) < WINDOW,
                    jnp.dot(kwn_ref[0, pl.ds(w0, wk), :], qp, preferred_element_type=_F32), NEG_INF)
    acc_w = weighted(s_w, jnp.max(s_w, axis=0, keepdims=True), vwt_ref[0, :, pl.ds(w0, wk)])
    ow_ref[...] = acc_w[:HEAD_DIM] / acc_w[HEAD_DIM:HEAD_DIM + 1]

    c_diag = t0 // kc_len
    s_a[...] = scores(0, True)
    init = (jnp.full((1, cols), NEG_INF, _F32), jnp.zeros((HEAD_DIM + 2 * SUBLANES, cols), _F32))
    m_acc, acc_s = lax.fori_loop(0, c_diag // 2, pair, init)

    def odd_tail(m_acc, acc):
        m_acc, acc = step(c_diag, True, s_b, s_a, m_acc, acc)
        return fold_in(c_diag, s_b[...], m_acc, acc)[1]

    def even_tail(m_acc, acc):
        return fold_in(c_diag, s_a[...], m_acc, acc)[1]

    acc_s = lax.cond(c_diag % 2 == 1, odd_tail, even_tail, m_acc, acc_s)
    o_s = acc_s[:HEAD_DIM] / acc_s[HEAD_DIM:HEAD_DIM + 1]
    o_w = ow_ref[...]

    outs = []
    for r in range(GROUP_SIZE):
        sl = slice(r * tq, (r + 1) * tq)
        gate = lambda branch: gt_ref[0, pl.ds(branch * N_HEADS + g * GROUP_SIZE + r, 1), :]
        outs.append(gate(0) * o_c[:, sl] + gate(1) * o_s[:, sl] + gate(2) * o_w[:, sl])
    o_ref[0] = jnp.concatenate(outs, axis=0).T


def _nsa_prompt(q_t, gates_t, kcmp3, kcmp_t, ks_n, ks_tb, kw_n, kw_tb, tq, kc_len):
    bsz, _, t = q_t.shape
    ncr = kcmp3.shape[2]
    n_blk = t // SLC_BLOCK
    mt = jnp.tile(_importance_map(ncr, n_blk, ncr - 1), (1, 3)).astype(_BF16)
    rows_n = pl.BlockSpec((1, t, KV_ROW), lambda b, g, i: (b, 0, 0))
    vals_t = pl.BlockSpec((1, HEAD_DIM, t), lambda b, g, i: (b, N_KV_HEADS + g, 0))
    qw = GROUP_SIZE * HEAD_DIM
    return pl.pallas_call(
        functools.partial(_nsa_prompt_kernel, tq=tq, kc_len=kc_len, n_cmp=ncr - 1, n_blk=n_blk),
        grid=(bsz, N_KV_HEADS, t // tq),
        in_specs=[pl.BlockSpec((1, qw, tq), lambda b, g, i: (b, g, i)),
                  pl.BlockSpec((1, GATE_ROWS, tq), lambda b, g, i: (b, 0, i)),
                  pl.BlockSpec((1, 1, ncr, KV_SLOTS * HEAD_DIM), lambda b, g, i: (b, g, 0, 0)),
                  pl.BlockSpec((1, 1, HEAD_DIM, ncr), lambda b, g, i: (b, N_KV_HEADS + g, 0, 0)),
                  rows_n, vals_t, rows_n, vals_t,
                  pl.BlockSpec(mt.shape, lambda b, g, i: (0, 0))],
        out_specs=pl.BlockSpec((1, tq, qw), lambda b, g, i: (b, i, g)),
        out_shape=jax.ShapeDtypeStruct((bsz, t, D_ATTN), _F32),
        scratch_shapes=[pltpu.VMEM((n_blk, GROUP_SIZE * tq), _F32),
                        pltpu.VMEM((kc_len, GROUP_SIZE * tq), _F32), pltpu.VMEM((kc_len, GROUP_SIZE * tq), _F32),
                        pltpu.VMEM((HEAD_DIM, GROUP_SIZE * tq), _F32)],
        compiler_params=_params("parallel", "parallel", "parallel"),
        name="nsa_prompt",
    )(q_t, gates_t, kcmp3, kcmp_t, ks_n, ks_tb, kw_n, kw_tb, mt)


def _nsa_sample_kernel(pt_ref, q_ref, gt_ref, kcmp_ref, kcmpt_ref, slc_hbm, ksn_ref, win_ref, kwt_ref,
                       m_ref, e_ref, o_ref, nw_ref, buf, sem,
                       *, n_pages, past, n_cmp, blk_pad, n_buf):
    b = pl.program_id(0)
    slot = _gather_pages(pt_ref, slc_hbm, buf, sem, n_pages)
    q = q_ref[0]
    gt = gt_ref[0]
    ks_new = ksn_ref[0]
    nseq = kwt_ref.shape[1]
    seq_lane = lax.broadcasted_iota(jnp.int32, (KV_ROW, nseq), 1)
    kw_col = jnp.sum(jnp.where(seq_lane == b, kwt_ref[...], 0.0), axis=1, keepdims=True)
    win = win_ref[0]
    pos = lax.broadcasted_iota(jnp.int32, win.shape, 1)
    band = jnp.where(pos == n_buf - 1, kw_col, pltpu.roll(win, n_buf - 1, 1))
    nw_ref[0] = band
    band_b = band.astype(_BF16)
    ncr = kcmp_ref.shape[2]
    lane = lax.broadcasted_iota(jnp.int32, (SUBLANES, LANES), 1)
    hrow = lax.broadcasted_iota(jnp.int32, (SUBLANES, LANES), 0)
    q8s, o_cs, sels = [], [], []
    for g in range(N_KV_HEADS):
        q8 = jnp.concatenate([q[:, (g * GROUP_SIZE + r) * HEAD_DIM:(g * GROUP_SIZE + r + 1) * HEAD_DIM]
                              for r in range(GROUP_SIZE)]
                             + [jnp.zeros((SUBLANES - GROUP_SIZE, HEAD_DIM), _F32)], axis=0)
        q8s.append(q8)
        q_hi = q8.astype(_BF16)
        q_lo = (q8 - q_hi.astype(_F32)).astype(_BF16)
        q3 = jnp.concatenate([q_hi, q_hi, q_lo, jnp.zeros_like(q_hi)], axis=1)
        s_c = lax.dot_general(q3, kcmp_ref[0, g], _NT, preferred_element_type=_F32)
        n_idx = lax.broadcasted_iota(jnp.int32, (SUBLANES, ncr), 1)
        cmp_end = jnp.where(n_idx < n_cmp, n_idx * CMP_STRIDE + (CMP_BLOCK - 1), jnp.int32(2 ** 30))
        p_c = _softmax(s_c, cmp_end <= past, 1)
        o_cs.append(_bdot_nt(p_c, kcmpt_ref[0, N_KV_HEADS + g]))

        imp = jnp.sum(p_c[:GROUP_SIZE], axis=0, keepdims=True)
        imp8 = jnp.broadcast_to(imp, (SUBLANES, ncr))
        imp_hi = imp8.astype(_BF16)
        imp_r = imp8 - imp_hi.astype(_F32)
        imp_mid = imp_r.astype(_BF16)
        imp_lo = (imp_r - imp_mid.astype(_F32)).astype(_BF16)
        imp_s = jnp.dot(jnp.concatenate([imp_hi, imp_mid, imp_lo], axis=1), m_ref[...],
                        preferred_element_type=_F32)[0:1]
        j_idx = lax.broadcasted_iota(jnp.int32, (1, blk_pad), 1)
        valid = j_idx * SLC_BLOCK <= past
        forced = (j_idx == 0) | (valid & (j_idx > past // SLC_BLOCK - N_LOCAL))
        score = jnp.where(forced, FORCED_SCORE, jnp.where(valid, imp_s, -1.0))
        s_k = jnp.broadcast_to(score, (blk_pad, blk_pad))
        s_j = s_k.T
        jj = lax.broadcasted_iota(jnp.int32, (blk_pad, blk_pad), 0)
        kk = lax.broadcasted_iota(jnp.int32, (blk_pad, blk_pad), 1)
        beats = jnp.where(kk > jj, jnp.where(s_j >= s_k, 1.0, 0.0), jnp.where(s_j > s_k, 1.0, 0.0))
        rank = jnp.sum(beats, axis=0, keepdims=True)
        sels.append(jnp.where(valid & (rank < N_SELECT), 1.0, 0.0))
    row_g = lax.broadcasted_iota(jnp.int32, (SUBLANES, blk_pad), 0)
    sel_rows = jnp.zeros((SUBLANES, blk_pad), _F32)
    for g in range(N_KV_HEADS):
        sel_rows = jnp.where(row_g == g, sels[g], sel_rows)
    picked = jnp.dot(sel_rows.astype(_BF16), e_ref[...], preferred_element_type=_F32)

    heads = []
    for g in range(N_KV_HEADS):
        q8 = q8s[g]
        q8b = q8.astype(_BF16)
        k_rows = slice(g * HEAD_DIM, (g + 1) * HEAD_DIM)
        v_rows = slice((N_KV_HEADS + g) * HEAD_DIM, (N_KV_HEADS + g + 1) * HEAD_DIM)
        s_s = jnp.where(picked[g:g + 1, :] > 0.5, _bdot(q8b, buf[slot, k_rows, :]), NEG_INF)
        s_n = jnp.sum(q8 * ks_new[:, k_rows], axis=-1, keepdims=True)
        m = jnp.maximum(jnp.max(s_s, axis=-1, keepdims=True), s_n)
        e_p, e_n = jnp.exp(s_s - m), jnp.exp(s_n - m)
        num = _bdot_nt(e_p, buf[slot, v_rows, :])
        o_s = (num + e_n * ks_new[:, v_rows]) / (jnp.sum(e_p, axis=-1, keepdims=True) + e_n)

        s_w = jnp.dot(q8b, band_b[k_rows, :], preferred_element_type=_F32)
        i_idx = lax.broadcasted_iota(jnp.int32, (SUBLANES, n_buf), 1)
        p_w = _softmax(s_w, (n_buf - 1 - i_idx) < WINDOW, 1)
        o_w = _bdot_nt(p_w, band_b[v_rows, :])

        def gate(branch):
            tgt = branch * N_HEADS + g * GROUP_SIZE + hrow
            return jnp.sum(jnp.where(lane == tgt, jnp.broadcast_to(gt, (SUBLANES, LANES)), 0.0),
                           axis=-1, keepdims=True)

        o = gate(0) * o_cs[g] + gate(1) * o_s + gate(2) * o_w
        heads += [o[r:r + 1, :] for r in range(GROUP_SIZE)]
    o_ref[0] = jnp.concatenate(heads, axis=1)


def _expand_table(past, blk_pad):
    return (jnp.arange(past)[None, :] // SLC_BLOCK == jnp.arange(blk_pad)[:, None]).astype(_BF16)


def _nsa_sample(q, gates, kcmp, kcmp_t, cache_t, page_table, ks_new, win_t, kw_t):
    nseq, n_pages = page_table.shape
    past = n_pages * PAGE_SIZE
    ncr = kcmp.shape[2]
    n_buf = win_t.shape[2]
    n_blk = past // SLC_BLOCK + 1
    blk_pad = -(-n_blk // LANES) * LANES
    m_tab = jnp.tile(_importance_map(ncr, blk_pad, ncr - 1).T, (3, 1)).astype(_BF16)
    e = _expand_table(past, blk_pad)
    row = lambda w: pl.BlockSpec((1, 1, w), lambda b, pt: (b, 0, 0))
    const = lambda a: pl.BlockSpec(a.shape, lambda b, pt: (0,) * a.ndim)
    win_spec = pl.BlockSpec((1, KV_ROW, n_buf), lambda b, pt: (b, 0, 0))
    grid_spec = pltpu.PrefetchScalarGridSpec(
        num_scalar_prefetch=1,
        grid=(nseq,),
        in_specs=[row(D_ATTN), row(LANES),
                  pl.BlockSpec((1, N_KV_HEADS, ncr, KV_SLOTS * HEAD_DIM), lambda b, pt: (b, 0, 0, 0)),
                  pl.BlockSpec((1, KV_SLOTS, HEAD_DIM, ncr), lambda b, pt: (b, 0, 0, 0)),
                  pl.BlockSpec(memory_space=pl.ANY),
                  row(KV_ROW), win_spec, const(kw_t), const(m_tab), const(e)],
        out_specs=[row(D_ATTN), win_spec],
        scratch_shapes=[pltpu.VMEM((2, KV_ROW, past), _F32), pltpu.SemaphoreType.DMA((2,))],
    )
    return pl.pallas_call(
        functools.partial(_nsa_sample_kernel, n_pages=n_pages, past=past, n_cmp=ncr - 1,
                          blk_pad=blk_pad, n_buf=n_buf),
        grid_spec=grid_spec,
        out_shape=[jax.ShapeDtypeStruct((nseq, 1, D_ATTN), _F32),
                   jax.ShapeDtypeStruct((nseq, KV_ROW, n_buf), _F32)],
        compiler_params=_params("arbitrary"),
        name="nsa_sample",
    )(page_table, q, gates, kcmp, kcmp_t, cache_t, ks_new, win_t, kw_t, m_tab, e)


def _post_kernel(x_ref, y_ref, u_ref, oa_ref, mod_ref, d_ref, wglu_ref, gs_ref, ga_ref, wout_ref, gm_ref,
                 wup_ref, wdn_ref, gf_ref, o_ref, *, ff_chunk):
    z = _gelu_tanh(y_ref[0] + d_ref[...] * u_ref[0])
    o_ssm = z * _sigmoid(jnp.dot(z.astype(_BF16), wglu_ref[...], preferred_element_type=_F32))
    mix = (jnp.dot(_rms(o_ssm, gs_ref[...]).astype(_BF16), wout_ref[:D_SSM, :], preferred_element_type=_F32)
           + jnp.dot(_rms(oa_ref[0], ga_ref[...]).astype(_BF16), wout_ref[D_SSM:, :], preferred_element_type=_F32))
    x1 = x_ref[0] + mod_ref[0, 2] * mix
    h2 = (_rms(x1, gm_ref[...]) * (1.0 + mod_ref[0, 4]) + mod_ref[0, 3]).astype(_BF16)
    acc = jnp.zeros_like(x1)
    for k in range(D_FF // ff_chunk):
        cols = slice(k * ff_chunk, (k + 1) * ff_chunk)
        hid = jnp.maximum(jnp.dot(h2, wup_ref[:, cols], preferred_element_type=_F32), 0.0)
        acc = acc + jnp.dot((hid * hid).astype(_BF16), wdn_ref[cols, :], preferred_element_type=_F32)
    o_ref[0] = _rms(x1 + mod_ref[0, 5] * acc, gf_ref[...])


def _post(x, y_ssm, u, o_attn, mod, lp, norm_final, tm):
    bsz, t, _ = x.shape
    r = mod.shape[2]
    rb = 1 if r == 1 else tm
    mod_map = (lambda b, i: (b, 0, 0, 0)) if r == 1 else (lambda b, i: (b, 0, i, 0))
    tok = lambda w: pl.BlockSpec((1, tm, w), lambda b, i: (b, i, 0))
    const = lambda shape: pl.BlockSpec(shape, lambda b, i: (0,) * len(shape), pipeline_mode=pl.Buffered(1))
    vec = lambda v: v.reshape(1, -1).astype(_F32)
    weights = [vec(lp['ssm_d']), lp['ssm_w_glu'].astype(_BF16), vec(lp['norm_out_ssm']), vec(lp['norm_out_attn']),
               lp['w_out'].astype(_BF16), vec(lp['norm_mlp']), lp['w_up'].astype(_BF16), lp['w_down'].astype(_BF16),
               vec(norm_final)]
    return pl.pallas_call(
        functools.partial(_post_kernel, ff_chunk=D_MODEL),
        grid=(bsz, t // tm),
        in_specs=[tok(D_MODEL), tok(D_SSM), tok(D_SSM), tok(D_ATTN),
                  pl.BlockSpec((1, 6, rb, D_MODEL), mod_map)] + [const(w.shape) for w in weights],
        out_specs=tok(D_MODEL),
        out_shape=jax.ShapeDtypeStruct((bsz, t, D_MODEL), _F32),
        compiler_params=_params("parallel", "parallel"),
        name="post_mlp",
    )(x, y_ssm, u, o_attn, mod, *weights)


def _feature_major(a):
    lead = a.shape[:-4]
    n = len(lead)
    return a.transpose(*range(n), n + 1, n + 2, n + 3, n).reshape(*lead, KV_ROW, a.shape[-4])


def _kv_output(a_t):
    n, _, t = a_t.shape
    return a_t.reshape(n, 2, N_KV_HEADS, HEAD_DIM, t).transpose(0, 4, 1, 2, 3)[None]


def kernel(x_prompt, x_sample, cache_cmp, cache_slc, state_win, state_ssm_re, state_ssm_im, page_table,
           c_prompt, c_sample, w_ada, b_ada, norm_attn, w_in, ssm_lambda_re, ssm_lambda_im, ssm_log_dt,
           ssm_b_re, ssm_b_im, ssm_c_re, ssm_c_im, ssm_d, ssm_w_glu, cmp_pe_k, cmp_w1_k, cmp_w2_k,
           cmp_pe_v, cmp_w1_v, cmp_w2_v, norm_out_ssm, norm_out_attn, w_out, norm_mlp, w_up, w_down,
           norm_final):
    depth = w_ada.shape[0]
    assert depth == 1, "single-layer trunk"
    l = 0
    bsz, t, _ = x_prompt.shape
    nseq = x_sample.shape[0]
    assert x_sample.shape[1] == 1
    tm = min(512, t)
    tq = min(256, t)
    kc_len = min(512, t)

    lp = dict(ssm_d=ssm_d[l], ssm_w_glu=ssm_w_glu[l], norm_out_ssm=norm_out_ssm[l],
              norm_out_attn=norm_out_attn[l], w_out=w_out[l], norm_mlp=norm_mlp[l], w_up=w_up[l], w_down=w_down[l])
    w_full = jnp.pad(w_in[l], ((0, 0), (0, D_IN_PAD - D_IN))).astype(_BF16)
    w_rows = jnp.concatenate([w_full[:, _Q0:_G0 + 3 * N_HEADS].T,
                              jnp.zeros((GATE_ROWS - 3 * N_HEADS, D_MODEL), _BF16)], axis=0)
    tq0, tkc0, tks0, tkw0, tg0 = 0, D_ATTN, D_ATTN + KV_ROW, D_ATTN + 2 * KV_ROW, D_ATTN + 3 * KV_ROW
    tabs = _ssm_tables(ssm_lambda_re[l], ssm_lambda_im[l], ssm_log_dt[l], ssm_b_re[l], ssm_b_im[l],
                       ssm_c_re[l], ssm_c_im[l])
    ctabs = _compress_tables(cmp_pe_k[l], cmp_w1_k[l], cmp_w2_k[l], cmp_pe_v[l], cmp_w1_v[l], cmp_w2_v[l])

    mod = _ada_mod(jnp.concatenate([c_prompt, c_sample], axis=0), w_ada[l], b_ada[l])
    mod_p = mod[:bsz].reshape(bsz, 6, 1, D_MODEL)
    mod_s = mod[bsz:].reshape(nseq, 6, D_MODEL).transpose(1, 0, 2)[None]

    w_tok = jnp.concatenate([w_full[:, :D_SSM], w_full[:, _KC0:_G0]], axis=1)
    outs_p = [('n', 0, D_SSM, None), ('n', D_SSM, D_SSM + KV_ROW, None),
              ('n', D_SSM + KV_ROW, D_SSM + 2 * KV_ROW, None), ('n', D_SSM + 2 * KV_ROW, D_SSM + 3 * KV_ROW, None),
              ('t', tq0, tkc0, 'scale'), ('t', tkc0, tks0, None), ('t', tks0, tkw0, None), ('t', tkw0, tg0, None),
              ('t', tks0, tkw0, None), ('t', tkw0, tg0, None), ('t', tg0, tg0 + GATE_ROWS, 'sigmoid')]
    dt_p = [_F32, _F32, _BF16, _BF16, _F32, _F32, _F32, _F32, _BF16, _BF16, _F32]
    (u, kc_n, ks_n, kw_n, q_t, kc_t, ks_t, kw_t, ks_tb, kw_tb, gates_t) = _in_proj(
        x_prompt, mod_p, norm_attn[l], w_tok, w_rows, tm, outs_p, dt_p)
    y_ssm, sp_re, sp_im = _ssm_prompt(u, tabs)
    kcmp, kcmp_t = _compress_prompt(kc_n, ctabs)
    o_attn = _nsa_prompt(q_t, gates_t, kcmp, kcmp_t, ks_n, ks_tb, kw_n, kw_tb, tq, kc_len)
    y_prompt = _post(x_prompt, y_ssm, u, o_attn, mod_p, lp, norm_final, tm)
    w_keep = min(WINDOW, t)

    xs = x_sample.reshape(1, nseq, D_MODEL)
    outs_s = [('n', 0, D_SSM, None), ('n', _Q0, _KC0, 'scale'), ('n', _KS0, _KW0, None), ('n', _G0, D_IN_PAD, 'sigmoid'),
              ('t', tkc0, tks0, None), ('t', tks0, tkw0, None), ('t', tkw0, tg0, None)]
    u_s, q_s, ks_row, gates_s, kc_ts, ks_ts, kw_ts = _in_proj(
        xs, mod_s, norm_attn[l], w_full, w_rows, nseq, outs_s, [_F32] * len(outs_s))
    ns = N_SSM_GROUPS * SSM_STATE
    y_s, ss_re, ss_im = _ssm_step(u_s[0], state_ssm_re[l].reshape(nseq, ns), state_ssm_im[l].reshape(nseq, ns), tabs)
    kcmp_s, kcmp_ts = _compress_paged(_feature_major(cache_cmp[l]), page_table, ctabs)
    o_attn_s, new_win_t = _nsa_sample(q_s.reshape(nseq, 1, D_ATTN), gates_s.reshape(nseq, 1, LANES), kcmp_s, kcmp_ts,
                                      _feature_major(cache_slc[l]), page_table, ks_row.reshape(nseq, 1, KV_ROW),
                                      _feature_major(state_win[l]), kw_ts[0])
    y_sample = _post(xs, y_s[None], u_s, o_attn_s.reshape(1, nseq, D_ATTN), mod_s, lp, norm_final, nseq)
    st_shape = (1, nseq, N_SSM_GROUPS, SSM_STATE)
    new_row = lambda a_t: _kv_output(a_t[0].T[:, :, None])

    return (y_prompt, y_sample.reshape(nseq, 1, D_MODEL),
            _kv_output(kc_t), _kv_output(ks_t), _kv_output(kw_t[:, :, t - w_keep:]),
            sp_re[None], sp_im[None],
            new_row(kc_ts), new_row(ks_ts), _kv_output(new_win_t),
            ss_re.reshape(st_shape), ss_im.reshape(st_shape))
```
